```python
import math
import jax, jax.numpy as jnp
from jax import lax
import numpy as np

D_MODEL = 1024
BATCH = 16
SEQ = 2048
DEPTH = 2

PLE_DIM = 256
N_EVEN = (DEPTH + 1) // 2
N_ODD = DEPTH // 2
NORM_EPS = 1e-6
CONV_WIDTH = 4

FOX_HEAD_DIM = 64
FOX_HEADS = (D_MODEL // 2) // FOX_HEAD_DIM
FOX_W = FOX_HEADS * FOX_HEAD_DIM
FOX_Q_BLOCK = 128
MLSTM_HEAD_DIM = 128
MLSTM_HEADS = (D_MODEL // 2) // MLSTM_HEAD_DIM
MLSTM_W = MLSTM_HEADS * MLSTM_HEAD_DIM
MLSTM_CHUNK = 64
MOBA_HEAD_DIM = 64
MOBA_HEADS = (D_MODEL // 2) // MOBA_HEAD_DIM
MOBA_W = MOBA_HEADS * MOBA_HEAD_DIM
MOBA_BLOCK = 256
MOBA_TOPK = 3
MOBA_Q_CHUNK = 16
SSD_HEAD_DIM = 64
SSD_HEADS = (D_MODEL // 2) // SSD_HEAD_DIM
SSD_W = SSD_HEADS * SSD_HEAD_DIM
SSD_GROUPS = 2
SSD_STATE = 128
SSD_BC = SSD_GROUPS * SSD_STATE
SSD_CONV_CH = SSD_W + 2 * SSD_BC
SSD_CHUNK = 64
MOE_GROUPS = 4
MOE_EXPERTS_PER_GROUP = 4
MOE_EXPERTS = MOE_GROUPS * MOE_EXPERTS_PER_GROUP
MOE_TOPK = 2
MOE_HIDDEN = D_MODEL // 4

EVEN_SIZES = (FOX_W, FOX_W, FOX_W, FOX_HEADS, FOX_W,
              MLSTM_W, MLSTM_W, MLSTM_W, MLSTM_HEADS, MLSTM_HEADS, MLSTM_W)
EVEN_COLS = sum(EVEN_SIZES)
EVEN_MIX = FOX_W + MLSTM_W
ODD_SIZES = (MOBA_W, MOBA_W, MOBA_W, SSD_W, SSD_W, SSD_BC, SSD_BC, SSD_HEADS)
ODD_COLS = sum(ODD_SIZES)
ODD_MIX = MOBA_W + SSD_W

kernel_name = 'hybrid_fox_mlstm_moba_ssd_hmoe'


def rms_norm(x, g, eps=NORM_EPS):
    xf = x.astype(jnp.float32)
    y = xf * lax.rsqrt(jnp.mean(xf * xf, axis=-1, keepdims=True) + eps)
    return (y * g.astype(jnp.float32)).astype(x.dtype)


def split_cols(u, sizes):
    offs = np.cumsum(np.array(sizes))[:-1].tolist()
    return jnp.split(u, offs, axis=-1)


def split_heads(y, n_heads):
    bsz, seq, width = y.shape
    return y.reshape(bsz, seq, n_heads, width // n_heads).transpose(0, 2, 1, 3)


def merge_heads(y):
    bsz, n_heads, seq, hd = y.shape
    return y.transpose(0, 2, 1, 3).reshape(bsz, seq, n_heads * hd)


def causal_dwconv(x, w, b):
    width = w.shape[0]
    seq = x.shape[1]
    xp = jnp.pad(x, ((0, 0), (width - 1, 0), (0, 0)))
    y = b
    for j in range(width):
        y = y + xp[:, j:j + seq] * w[j]
    return y


def forgetting_attention(q, k, v, log_f):
    bsz, n_heads, seq, hd = q.shape
    cum_f = jnp.cumsum(log_f, axis=-1)
    scale = hd ** -0.5
    kpos = jnp.arange(seq)

    def q_block(bi):
        start = bi * FOX_Q_BLOCK
        qb = lax.dynamic_slice_in_dim(q, start, FOX_Q_BLOCK, axis=2)
        fb = lax.dynamic_slice_in_dim(cum_f, start, FOX_Q_BLOCK, axis=2)
        qpos = start + jnp.arange(FOX_Q_BLOCK)
        logits = jnp.einsum('bhqd,bhkd->bhqk', qb, k).astype(jnp.float32) * scale
        logits = logits + fb[..., :, None] - cum_f[..., None, :]
        logits = jnp.where(kpos[None, :] <= qpos[:, None], logits, -jnp.inf)
        probs = jax.nn.softmax(logits, axis=-1).astype(v.dtype)
        return jnp.einsum('bhqk,bhkd->bhqd', probs, v)

    out = lax.map(q_block, jnp.arange(seq // FOX_Q_BLOCK))
    return out.transpose(1, 2, 0, 3, 4).reshape(bsz, n_heads, seq, hd)


def mlstm_chunkwise(q, k, v, log_i, log_f):
    bsz, n_heads, seq, hd = q.shape
    L = MLSTM_CHUNK
    nc = seq // L
    f32 = jnp.float32
    q = q.astype(f32)
    k = k.astype(f32) * (hd ** -0.5)
    v = v.astype(f32)

    def chunks(a):
        a = a.reshape(a.shape[:2] + (nc, L) + a.shape[3:])
        return jnp.moveaxis(a, 2, 0)

    tri = jnp.tril(jnp.ones((L, L), dtype=bool))

    def step(carry, inp):
        c_st, n_st, m_st = carry
        qc, kc, vc, ic, fc = inp
        b = jnp.cumsum(fc, axis=-1)
        log_d = b[..., :, None] - b[..., None, :] + ic[..., None, :]
        log_d = jnp.where(tri, log_d, -jnp.inf)
        m_inter = b + m_st[..., None]
        m_t = jnp.maximum(m_inter, jnp.max(log_d, axis=-1))
        d_mat = jnp.exp(log_d - m_t[..., None])
        inter = jnp.exp(m_inter - m_t)
        w_qk = jnp.einsum('bhld,bhsd->bhls', qc, kc) * d_mat
        num = (jnp.einsum('bhls,bhsd->bhld', w_qk, vc)
               + inter[..., None] * jnp.einsum('bhld,bhde->bhle', qc, c_st))
        den = w_qk.sum(axis=-1) + inter * jnp.einsum('bhld,bhd->bhl', qc, n_st)
        h = num / jnp.maximum(jnp.abs(den), jnp.exp(-m_t))[..., None]
        b_last = b[..., -1]
        log_w = b_last[..., None] - b + ic
        m_new = jnp.maximum(b_last + m_st, jnp.max(log_w, axis=-1))
        w = jnp.exp(log_w - m_new[..., None])
        decay = jnp.exp(b_last + m_st - m_new)
        c_new = decay[..., None, None] * c_st + jnp.einsum('bhl,bhld,bhle->bhde', w, kc, vc)
        n_new = decay[..., None] * n_st + jnp.einsum('bhl,bhld->bhd', w, kc)
        return (c_new, n_new, m_new), h

    init = (jnp.zeros((bsz, n_heads, hd, hd), f32),
            jnp.zeros((bsz, n_heads, hd), f32),
            jnp.zeros((bsz, n_heads), f32))
    _, hs = lax.scan(step, init, (chunks(q), chunks(k), chunks(v),
                                  chunks(log_i.astype(f32)), chunks(log_f.astype(f32))))
    return jnp.moveaxis(hs, 0, 2).reshape(bsz, n_heads, seq, hd)


def moba_attention(q, k, v):
    bsz, n_heads, seq, hd = q.shape
    nb = -(-seq // MOBA_BLOCK)
    seq_p = nb * MOBA_BLOCK
    pad = ((0, 0), (0, 0), (0, seq_p - seq), (0, 0))
    qp, kp, vp = jnp.pad(q, pad), jnp.pad(k, pad), jnp.pad(v, pad)
    kb = kp.reshape(bsz, n_heads, nb, MOBA_BLOCK, hd)
    vb = vp.reshape(bsz, n_heads, nb, MOBA_BLOCK, hd)
    k_mean = jnp.mean(kb.astype(jnp.float32), axis=3)
    k_sel = min(MOBA_TOPK, nb)
    scale = hd ** -0.5
    bi = jnp.arange(bsz)[:, None, None, None]
    hi = jnp.arange(n_heads)[None, :, None, None]

    def q_chunk(ci):
        start = ci * MOBA_Q_CHUNK
        qc = lax.dynamic_slice_in_dim(qp, start, MOBA_Q_CHUNK, axis=2)
        qpos = start + jnp.arange(MOBA_Q_CHUNK)
        own = start // MOBA_BLOCK
        gate = jnp.einsum('bhqd,bhnd->bhqn', qc.astype(jnp.float32), k_mean)
        gate = jnp.where(jnp.arange(nb) < own, gate, -jnp.inf)
        g_val, g_idx = lax.top_k(gate, k_sel)
        valid = jnp.isfinite(g_val)
        k_g = kb[bi, hi, g_idx].reshape(bsz, n_heads, MOBA_Q_CHUNK, k_sel * MOBA_BLOCK, hd)
        v_g = vb[bi, hi, g_idx].reshape(bsz, n_heads, MOBA_Q_CHUNK, k_sel * MOBA_BLOCK, hd)
        logit_past = jnp.einsum('bhqd,bhqnd->bhqn', qc, k_g).astype(jnp.float32) * scale
        logit_past = jnp.where(jnp.repeat(valid, MOBA_BLOCK, axis=-1), logit_past, -jnp.inf)
        k_own = lax.dynamic_slice_in_dim(kp, own * MOBA_BLOCK, MOBA_BLOCK, axis=2)
        v_own = lax.dynamic_slice_in_dim(vp, own * MOBA_BLOCK, MOBA_BLOCK, axis=2)
        kpos = own * MOBA_BLOCK + jnp.arange(MOBA_BLOCK)
        logit_own = jnp.einsum('bhqd,bhkd->bhqk', qc, k_own).astype(jnp.float32) * scale
        logit_own = jnp.where(kpos[None, :] <= qpos[:, None], logit_own, -jnp.inf)
        probs = jax.nn.softmax(jnp.concatenate([logit_past, logit_own], axis=-1), axis=-1)
        probs = probs.astype(v.dtype)
        n_past = k_sel * MOBA_BLOCK
        return (jnp.einsum('bhqn,bhqnd->bhqd', probs[..., :n_past], v_g)
                + jnp.einsum('bhqk,bhkd->bhqd', probs[..., n_past:], v_own))

    out = lax.map(q_chunk, jnp.arange(seq_p // MOBA_Q_CHUNK))
    out = out.transpose(1, 2, 0, 3, 4).reshape(bsz, n_heads, seq_p, hd)
    return out[:, :, :seq]


def segsum(a):
    L = a.shape[-1]
    cs = jnp.cumsum(a, axis=-1)
    out = cs[..., :, None] - cs[..., None, :]
    return jnp.where(jnp.tril(jnp.ones((L, L), dtype=bool)), out, -jnp.inf)


def ssd_chunked(x, dt, a, b_in, c_in):
    bsz, seq, n_heads, hp = x.shape
    ns = b_in.shape[-1]
    L = SSD_CHUNK
    nc = seq // L
    f32 = jnp.float32
    xdt = (x.astype(f32) * dt[..., None]).reshape(bsz, nc, L, n_heads, hp)
    a_dt = (a * dt).reshape(bsz, nc, L, n_heads).transpose(0, 1, 3, 2)
    bc = b_in.astype(f32).reshape(bsz, nc, L, n_heads, ns)
    cc = c_in.astype(f32).reshape(bsz, nc, L, n_heads, ns)
    a_cum = jnp.cumsum(a_dt, axis=-1)
    l_mat = jnp.exp(segsum(a_dt))
    cb = jnp.einsum('bclhn,bcshn->bchls', cc, bc)
    y_diag = jnp.einsum('bchls,bcshp->bclhp', cb * l_mat, xdt)
    decay_states = jnp.exp(a_cum[..., -1:] - a_cum)
    states = jnp.einsum('bclhn,bchl,bclhp->bchpn', bc, decay_states, xdt)
    chunk_decay = jnp.exp(a_cum[..., -1])

    def step(h, inp):
        st, dec = inp
        return dec[..., None, None] * h + st, h

    h0 = jnp.zeros((bsz, n_heads, hp, ns), f32)
    _, prev = lax.scan(step, h0, (jnp.moveaxis(states, 1, 0), jnp.moveaxis(chunk_decay, 1, 0)))
    prev = jnp.moveaxis(prev, 0, 1)
    y_off = jnp.einsum('bclhn,bchpn,bchl->bclhp', cc, prev, jnp.exp(a_cum))
    return (y_diag + y_off).reshape(bsz, seq, n_heads, hp)


def even_mixers(h, w_in, fox_b_f, fox_qn_g, fox_kn_g, conv_w, conv_b, b_i, b_f, norm_g):
    f32 = jnp.float32
    u = jnp.einsum('bsd,de->bse', h, w_in)
    fq, fk, fv, ff, fo, mq, mk, mv, mi, mf, mo = split_cols(u, EVEN_SIZES)
    q = rms_norm(split_heads(fq, FOX_HEADS), fox_qn_g)
    k = rms_norm(split_heads(fk, FOX_HEADS), fox_kn_g)
    v = split_heads(fv, FOX_HEADS)
    log_f = jax.nn.log_sigmoid(ff.astype(f32) + fox_b_f).transpose(0, 2, 1)
    out_a = merge_heads(forgetting_attention(q, k, v, log_f)) * jax.nn.sigmoid(fo)
    qk = jax.nn.silu(causal_dwconv(jnp.concatenate([mq, mk], axis=-1), conv_w, conv_b))
    mq, mk = jnp.split(qk, 2, axis=-1)
    log_i = (mi.astype(f32) + b_i).transpose(0, 2, 1)
    log_fm = jax.nn.log_sigmoid(mf.astype(f32) + b_f).transpose(0, 2, 1)
    hc = mlstm_chunkwise(split_heads(mq, MLSTM_HEADS), split_heads(mk, MLSTM_HEADS),
                         split_heads(mv, MLSTM_HEADS), log_i, log_fm)
    hc = rms_norm(hc, norm_g.reshape(MLSTM_HEADS, 1, MLSTM_HEAD_DIM)).astype(h.dtype)
    out_b = merge_heads(hc) * jax.nn.sigmoid(mo)
    return jnp.concatenate([out_a, out_b], axis=-1)


def odd_mixers(h, w_in, moba_qn_g, moba_kn_g, conv_w, conv_b, dt_bias, a_log, d_skip, norm_g):
    bsz, seq, _ = h.shape
    u = jnp.einsum('bsd,de->bse', h, w_in)
    cq, ck, cv, z, xs, bs, cs, dts = split_cols(u, ODD_SIZES)
    q = rms_norm(split_heads(cq, MOBA_HEADS), moba_qn_g)
    k = rms_norm(split_heads(ck, MOBA_HEADS), moba_kn_g)
    v = split_heads(cv, MOBA_HEADS)
    out_c = merge_heads(moba_attention(q, k, v))
    xbc = jax.nn.silu(causal_dwconv(jnp.concatenate([xs, bs, cs], axis=-1), conv_w, conv_b))
    xs, bs, cs = split_cols(xbc, (SSD_W, SSD_BC, SSD_BC))
    dt = jax.nn.softplus(dts.astype(jnp.float32) + dt_bias)
    a = -jnp.exp(a_log.astype(jnp.float32))
    xh = xs.reshape(bsz, seq, SSD_HEADS, SSD_HEAD_DIM)
    rep = SSD_HEADS // SSD_GROUPS
    bh = jnp.repeat(bs.reshape(bsz, seq, SSD_GROUPS, SSD_STATE), rep, axis=2)
    ch = jnp.repeat(cs.reshape(bsz, seq, SSD_GROUPS, SSD_STATE), rep, axis=2)
    y = ssd_chunked(xh, dt, a, bh, ch) + d_skip[:, None] * xh
    y = y.reshape(bsz, seq, SSD_W) * jax.nn.silu(z)
    y = rms_norm(y.reshape(bsz, seq, SSD_GROUPS, SSD_W // SSD_GROUPS),
                 norm_g.reshape(SSD_GROUPS, SSD_W // SSD_GROUPS))
    out_d = y.reshape(bsz, seq, SSD_W).astype(h.dtype)
    return jnp.concatenate([out_c, out_d], axis=-1)


def hier_moe(h, w_group, b_group, w_router, b_router, w_gate, w_up, w_down):
    bsz, seq, dm = h.shape
    t = h.reshape(-1, dm)
    g_logits = jnp.einsum('td,dg->tg', t, w_group).astype(jnp.float32) + b_group
    g_prob = jax.nn.softmax(g_logits, axis=-1)
    g_idx = jnp.argmax(g_logits, axis=-1)
    g_w = jnp.take_along_axis(g_prob, g_idx[:, None], axis=1)[:, 0]
    e_logits = (jnp.einsum('td,de->te', t, w_router).astype(jnp.float32) + b_router)
    e_logits = e_logits.reshape(-1, MOE_GROUPS, MOE_EXPERTS_PER_GROUP)
    e_in = jnp.take_along_axis(e_logits, g_idx[:, None, None], axis=1)[:, 0]
    top_v, top_i = lax.top_k(e_in, MOE_TOPK)
    top_w = jax.nn.softmax(top_v, axis=-1)
    w_in_group = jnp.sum(top_w[..., None] * jax.nn.one_hot(top_i, MOE_EXPERTS_PER_GROUP), axis=1)
    combine = (jax.nn.one_hot(g_idx, MOE_GROUPS)[:, :, None]
               * (g_w[:, None] * w_in_group)[:, None, :]).astype(h.dtype)
    out = jnp.zeros_like(t)
    for g in range(MOE_GROUPS):
        a = jnp.einsum('td,edf->tef', t, w_gate[g])
        b = jnp.einsum('td,edf->tef', t, w_up[g])
        hid = jax.nn.silu(a) * b * combine[:, g, :, None]
        out = out + jnp.einsum('tef,efd->td', hid, w_down[g])
    return out.reshape(bsz, seq, dm)


def per_layer_embedding(x, p_i, w_proj, w_gate, gate_norm_g, out_norm_g):
    e = jnp.einsum('bsk,kd->bsd', p_i, w_proj)
    gate = jax.nn.sigmoid(jnp.einsum('bsd,de->bse', rms_norm(x, gate_norm_g), w_gate))
    return rms_norm(e * gate, out_norm_g)


def setup_inputs(seed: int = 0) -> dict:
    key = jax.random.key(seed)
    k = jax.random.split(key, 35)
    f32 = jnp.float32

    def nrm(kk, shape, s):
        return s * jax.random.normal(kk, shape, f32)

    def gain(kk, shape):
        return 1.0 + 0.05 * jax.random.normal(kk, shape, f32)

    dt_u = jax.random.uniform(k[19], (N_ODD, SSD_HEADS), f32)
    dt0 = jnp.exp(dt_u * (math.log(0.1) - math.log(1e-3)) + math.log(1e-3))
    return {
        'x': nrm(k[0], (BATCH, SEQ, D_MODEL), 1.0),
        'p': nrm(k[1], (DEPTH, BATCH, SEQ, PLE_DIM), 1.0),
        'norm1_g': gain(k[2], (DEPTH, D_MODEL)),
        'norm2_g': gain(k[3], (DEPTH, D_MODEL)),
        'ev_w_in': nrm(k[4], (N_EVEN, D_MODEL, EVEN_COLS), D_MODEL ** -0.5),
        'ev_fox_b_f': 2.0 + nrm(k[5], (N_EVEN, FOX_HEADS), 0.5),
        'ev_fox_qn_g': gain(k[6], (N_EVEN, FOX_HEAD_DIM)),
        'ev_fox_kn_g': gain(k[7], (N_EVEN, FOX_HEAD_DIM)),
        'ev_mlstm_conv_w': nrm(k[8], (N_EVEN, CONV_WIDTH, 2 * MLSTM_W), CONV_WIDTH ** -0.5),
        'ev_mlstm_conv_b': nrm(k[9], (N_EVEN, 2 * MLSTM_W), 0.01),
        'ev_mlstm_b_i': nrm(k[10], (N_EVEN, MLSTM_HEADS), 0.1),
        'ev_mlstm_b_f': jnp.linspace(3.0, 6.0, MLSTM_HEADS, dtype=f32)[None, :]
                        + nrm(k[11], (N_EVEN, MLSTM_HEADS), 0.1),
        'ev_mlstm_norm_g': gain(k[12], (N_EVEN, MLSTM_W)),
        'ev_w_out': nrm(k[13], (N_EVEN, EVEN_MIX, D_MODEL), EVEN_MIX ** -0.5),
        'od_w_in': nrm(k[14], (N_ODD, D_MODEL, ODD_COLS), D_MODEL ** -0.5),
        'od_moba_qn_g': gain(k[15], (N_ODD, MOBA_HEAD_DIM)),
        'od_moba_kn_g': gain(k[16], (N_ODD, MOBA_HEAD_DIM)),
        'od_ssd_conv_w': nrm(k[17], (N_ODD, CONV_WIDTH, SSD_CONV_CH), CONV_WIDTH ** -0.5),
        'od_ssd_conv_b': nrm(k[18], (N_ODD, SSD_CONV_CH), 0.01),
        'od_ssd_dt_bias': dt0 + jnp.log(-jnp.expm1(-dt0)),
        'od_ssd_A_log': jnp.log(jax.random.uniform(k[20], (N_ODD, SSD_HEADS), f32, 1.0, 16.0)),
        'od_ssd_D': 1.0 + nrm(k[21], (N_ODD, SSD_HEADS), 0.1),
        'od_ssd_norm_g': gain(k[22], (N_ODD, SSD_W)),
        'od_w_out': nrm(k[23], (N_ODD, ODD_MIX, D_MODEL), ODD_MIX ** -0.5),
        'moe_w_group': nrm(k[24], (DEPTH, D_MODEL, MOE_GROUPS), D_MODEL ** -0.5),
        'moe_b_group': nrm(k[25], (DEPTH, MOE_GROUPS), 0.01),
        'moe_w_router': nrm(k[26], (DEPTH, D_MODEL, MOE_EXPERTS), D_MODEL ** -0.5),
        'moe_b_router': nrm(k[27], (DEPTH, MOE_EXPERTS), 0.01),
        'moe_w_gate': nrm(k[28], (DEPTH, MOE_GROUPS, MOE_EXPERTS_PER_GROUP, D_MODEL, MOE_HIDDEN), D_MODEL ** -0.5),
        'moe_w_up': nrm(k[29], (DEPTH, MOE_GROUPS, MOE_EXPERTS_PER_GROUP, D_MODEL, MOE_HIDDEN), D_MODEL ** -0.5),
        'moe_w_down': nrm(k[30], (DEPTH, MOE_GROUPS, MOE_EXPERTS_PER_GROUP, MOE_HIDDEN, D_MODEL), MOE_HIDDEN ** -0.5),
        'ple_w_proj': nrm(k[31], (DEPTH, PLE_DIM, D_MODEL), PLE_DIM ** -0.5),
        'ple_w_gate': nrm(k[32], (DEPTH, D_MODEL, D_MODEL), D_MODEL ** -0.5),
        'ple_gate_norm_g': gain(k[33], (DEPTH, D_MODEL)),
        'ple_out_norm_g': gain(k[34], (DEPTH, D_MODEL)),
    }


def reference(x, p, norm1_g, norm2_g, ev_w_in, ev_fox_b_f, ev_fox_qn_g, ev_fox_kn_g,
              ev_mlstm_conv_w, ev_mlstm_conv_b, ev_mlstm_b_i, ev_mlstm_b_f, ev_mlstm_norm_g,
              ev_w_out, od_w_in, od_moba_qn_g, od_moba_kn_g, od_ssd_conv_w, od_ssd_conv_b,
              od_ssd_dt_bias, od_ssd_A_log, od_ssd_D, od_ssd_norm_g, od_w_out,
              moe_w_group, moe_b_group, moe_w_router, moe_b_router, moe_w_gate, moe_w_up,
              moe_w_down, ple_w_proj, ple_w_gate, ple_gate_norm_g, ple_out_norm_g):
    for i in range(DEPTH):
        h = rms_norm(x, norm1_g[i])
        if i % 2 == 0:
            j = i // 2
            mix = even_mixers(h, ev_w_in[j], ev_fox_b_f[j], ev_fox_qn_g[j], ev_fox_kn_g[j],
                              ev_mlstm_conv_w[j], ev_mlstm_conv_b[j], ev_mlstm_b_i[j],
                              ev_mlstm_b_f[j], ev_mlstm_norm_g[j])
            x = x + jnp.einsum('bse,ed->bsd', mix, ev_w_out[j])
        else:
            j = i // 2
            mix = odd_mixers(h, od_w_in[j], od_moba_qn_g[j], od_moba_kn_g[j], od_ssd_conv_w[j],
                             od_ssd_conv_b[j], od_ssd_dt_bias[j], od_ssd_A_log[j], od_ssd_D[j],
                             od_ssd_norm_g[j])
            x = x + jnp.einsum('bse,ed->bsd', mix, od_w_out[j])
        x = x + hier_moe(rms_norm(x, norm2_g[i]), moe_w_group[i], moe_b_group[i],
                         moe_w_router[i], moe_b_router[i], moe_w_gate[i], moe_w_up[i],
                         moe_w_down[i])
        x = x + per_layer_embedding(x, p[i], ple_w_proj[i], ple_w_gate[i],
                                    ple_gate_norm_g[i], ple_out_norm_g[i])
    return x
```

```python
import functools

import jax
import jax.numpy as jnp
import numpy as np
from jax import lax
from jax.experimental import pallas as pl
from jax.experimental.pallas import tpu as pltpu

F32 = jnp.float32
BF16 = jnp.bfloat16
NEG_INF = float("-inf")

NORM_EPS = 1e-6
D_MODEL = 1024
HALF = D_MODEL // 2
ATT_HEAD_DIM = 64
ATT_HEADS = HALF // ATT_HEAD_DIM
ATT_PAIRS = ATT_HEADS // 2
MLSTM_HEAD_DIM = 128
MLSTM_HEADS = HALF // MLSTM_HEAD_DIM
SSD_HEAD_DIM = 64
SSD_HEADS = HALF // SSD_HEAD_DIM
SSD_GROUPS = 2
SSD_STATE = 128
SSD_GROUP_W = HALF // SSD_GROUPS
SSD_HEADS_PER_GROUP = SSD_HEADS // SSD_GROUPS
CONV_WIDTH = 4
MOBA_BLOCK = 256
MOBA_TOPK = 3
MOE_GROUPS = 4
MOE_EPG = 4
MOE_EXPERTS = MOE_GROUPS * MOE_EPG
MOE_HIDDEN = D_MODEL // 4
PLE_DIM = 256

LANES = 128
MXU_N = 256
SEQ_TILE = 256
MIB = 1024 * 1024


def _cparams(sem, vmem_mib):
    return pltpu.CompilerParams(dimension_semantics=sem, vmem_limit_bytes=vmem_mib * MIB)


def _split2(x):
    hi = x.astype(BF16)
    lo = (x - hi.astype(F32)).astype(BF16)
    return hi, lo


def _split3(x):
    hi = x.astype(BF16)
    r = x - hi.astype(F32)
    mid = r.astype(BF16)
    lo = (r - mid.astype(F32)).astype(BF16)
    return hi, mid, lo


def _dot(a, b):
    return jnp.dot(a, b, preferred_element_type=F32)


def _dot_nt(a, b):
    return lax.dot_general(a, b, (((1,), (1,)), ((), ())), preferred_element_type=F32)


def _log_sigmoid(x):
    return jnp.minimum(x, 0.0) - jnp.log1p(jnp.exp(-jnp.abs(x)))


def _softplus(x):
    return jnp.maximum(x, 0.0) + jnp.log1p(jnp.exp(-jnp.abs(x)))


def _rms_rows(xf, g, eps=NORM_EPS):
    ms = jnp.mean(xf * xf, axis=-1, keepdims=True)
    return xf * lax.rsqrt(ms + eps) * g


def _block_index(i, block):
    shift = block.bit_length() - 1
    assert 1 << shift == block
    return lax.shift_right_logical(i, shift)


def _lane_col(x, idx):
    lane = lax.broadcasted_iota(jnp.int32, (1, x.shape[1]), 1)
    return jnp.sum(jnp.where(lane == idx, x, 0.0), axis=1, keepdims=True)


def _in_proj_kernel(x_ref, g_ref, w_ref, gain_ref, *out_refs, widths, n_norm, head_dim):
    h = _rms_rows(x_ref[...], g_ref[...]).astype(BF16)
    r = _block_index(lax.broadcasted_iota(jnp.int32, (MXU_N, MXU_N), 0), head_dim)
    c = _block_index(lax.broadcasted_iota(jnp.int32, (MXU_N, MXU_N), 1), head_dim)
    seg = jnp.where(r == c, 1.0, 0.0).astype(BF16)
    off = 0
    for o_ref, wdt in zip(out_refs, widths):
        for c0 in range(0, wdt, MXU_N):
            n = min(MXU_N, wdt - c0)
            y = _dot(h, w_ref[:, off + c0: off + c0 + n])
            if off + c0 < n_norm:
                hi, lo = _split2(y * y)
                ss = _dot(hi, seg) + _dot(lo, seg)
                y = y * lax.rsqrt(ss * (1.0 / head_dim) + NORM_EPS) * gain_ref[:, off + c0: off + c0 + n]
            o_ref[:, c0:c0 + n] = y
        off += wdt


def _in_proj(x2d, g, w, gain, widths, n_norm, tm=256):
    t, d = x2d.shape
    n = w.shape[1]
    kern = functools.partial(_in_proj_kernel, widths=widths, n_norm=n_norm, head_dim=ATT_HEAD_DIM)
    return pl.pallas_call(
        kern,
        grid=(t // tm,),
        in_specs=[
            pl.BlockSpec((tm, d), lambda i: (i, 0)),
            pl.BlockSpec((1, d), lambda i: (0, 0)),
            pl.BlockSpec((d, n), lambda i: (0, 0)),
            pl.BlockSpec((1, n_norm), lambda i: (0, 0)),
        ],
        out_specs=[pl.BlockSpec((tm, wd), lambda i: (i, 0)) for wd in widths],
        out_shape=[jax.ShapeDtypeStruct((t, wd), F32) for wd in widths],
        compiler_params=_cparams(("parallel",), 48),
        name="in_proj",
    )(x2d, g, w, gain)


def _tri_cumsum(vals, tri):
    hi, mid, lo = _split3(vals)
    return _dot(tri, hi) + _dot(tri, mid) + _dot(tri, lo)


def _lower_tri(n):
    r = lax.broadcasted_iota(jnp.int32, (n, n), 0)
    c = lax.broadcasted_iota(jnp.int32, (n, n), 1)
    return jnp.where(c <= r, 1.0, 0.0).astype(BF16)


def _prep_even_kernel(raw_ref, bias_ref, g_ref, rf_ref, rm_ref):
    s = raw_ref.shape[0]
    ch = SEQ_TILE
    lane = lax.broadcasted_iota(jnp.int32, (1, LANES), 1)
    is_cum = (lane < ATT_HEADS) | ((lane >= ATT_HEADS + MLSTM_HEADS) & (lane < ATT_HEADS + 2 * MLSTM_HEADS))
    tri = _lower_tri(ch)
    carry = jnp.zeros((1, LANES), F32)
    pad = jnp.zeros((6, ch), F32)
    for c in range(s // ch):
        z = raw_ref[c * ch:(c + 1) * ch, :] + bias_ref[...]
        cs = _tri_cumsum(jnp.where(is_cum, _log_sigmoid(z), 0.0), tri) + carry
        carry = cs[ch - 1:ch, :]
        g = jnp.where(is_cum, cs, z)
        g_ref[c * ch:(c + 1) * ch, :] = g
        gt = g.T
        for hp in range(ATT_PAIRS):
            rf_ref[hp, :, c * ch:(c + 1) * ch] = jnp.concatenate([gt[2 * hp:2 * hp + 2], pad], axis=0)
        for hh in range(MLSTM_HEADS):
            i_row = gt[ATT_HEADS + hh:ATT_HEADS + hh + 1]
            f_row = gt[ATT_HEADS + MLSTM_HEADS + hh:ATT_HEADS + MLSTM_HEADS + hh + 1]
            rm_ref[hh, :, c * ch:(c + 1) * ch] = jnp.concatenate([i_row - f_row, f_row, pad], axis=0)


def _prep_even(raw, bias):
    b, s, _ = raw.shape
    return pl.pallas_call(
        _prep_even_kernel,
        grid=(b,),
        in_specs=[pl.BlockSpec((None, s, LANES), lambda i: (i, 0, 0)),
                  pl.BlockSpec((1, LANES), lambda i: (0, 0))],
        out_specs=[pl.BlockSpec((None, s, LANES), lambda i: (i, 0, 0)),
                   pl.BlockSpec((None, ATT_PAIRS, 8, s), lambda i: (i, 0, 0, 0)),
                   pl.BlockSpec((None, MLSTM_HEADS, 8, s), lambda i: (i, 0, 0, 0))],
        out_shape=[jax.ShapeDtypeStruct((b, s, LANES), F32),
                   jax.ShapeDtypeStruct((b, ATT_PAIRS, 8, s), F32),
                   jax.ShapeDtypeStruct((b, MLSTM_HEADS, 8, s), F32)],
        compiler_params=_cparams(("parallel",), 32),
        name="prep_even",
    )(raw, bias)


def _prep_odd_kernel(raw_ref, bias_ref, alog_ref, g_ref, rs_ref):
    s = raw_ref.shape[0]
    ch = SEQ_TILE
    lane = lax.broadcasted_iota(jnp.int32, (1, LANES), 1)
    is_dt = lane < SSD_HEADS
    tri = _lower_tri(ch)
    a_row = jnp.where(is_dt, -jnp.exp(alog_ref[...]), 0.0)
    pad = jnp.zeros((8 - SSD_HEADS_PER_GROUP, ch), F32)
    for c in range(s // ch):
        dt = jnp.where(is_dt, _softplus(raw_ref[c * ch:(c + 1) * ch, :] + bias_ref[...]), 0.0)
        acum = _tri_cumsum(a_row * dt, tri)
        g = jnp.where(is_dt, dt, pltpu.roll(acum, SSD_HEADS, 1))
        g_ref[c * ch:(c + 1) * ch, :] = g
        gt = g.T
        for gi in range(SSD_GROUPS):
            lo = SSD_HEADS + gi * SSD_HEADS_PER_GROUP
            rs_ref[gi, :, c * ch:(c + 1) * ch] = jnp.concatenate([gt[lo:lo + SSD_HEADS_PER_GROUP], pad], axis=0)


def _prep_odd(raw, bias, alog):
    b, s, _ = raw.shape
    return pl.pallas_call(
        _prep_odd_kernel,
        grid=(b,),
        in_specs=[pl.BlockSpec((None, s, LANES), lambda i: (i, 0, 0)),
                  pl.BlockSpec((1, LANES), lambda i: (0, 0)),
                  pl.BlockSpec((1, LANES), lambda i: (0, 0))],
        out_specs=[pl.BlockSpec((None, s, LANES), lambda i: (i, 0, 0)),
                   pl.BlockSpec((None, SSD_GROUPS, 8, s), lambda i: (i, 0, 0, 0))],
        out_shape=[jax.ShapeDtypeStruct((b, s, LANES), F32),
                   jax.ShapeDtypeStruct((b, SSD_GROUPS, 8, s), F32)],
        compiler_params=_cparams(("parallel",), 32),
        name="prep_odd",
    )(raw, bias, alog)


def _softmax_block(s, v, carry):
    m, l, acc = carry
    m_new = jnp.maximum(m, jnp.max(s, axis=1, keepdims=True))
    alpha = jnp.exp(m - m_new)
    p = jnp.exp(s - m_new)
    l = alpha * l + jnp.sum(p, axis=1, keepdims=True)
    acc = alpha * acc + _dot(p.astype(BF16), v)
    return m_new, l, acc


def _fox_kernel(q_ref, k_ref, v_ref, fo_ref, g_ref, r_ref, o_ref, kb, vb):
    hp = pl.program_id(1)
    qi = pl.program_id(2)
    tq = q_ref.shape[0]

    @pl.when(qi == 0)
    def _():
        kb[...] = k_ref[...].astype(BF16)
        vb[...] = v_ref[...].astype(BF16)

    lane = lax.broadcasted_iota(jnp.int32, (1, LANES), 1)
    row = lax.broadcasted_iota(jnp.int32, (tq, tq), 0)
    col = lax.broadcasted_iota(jnp.int32, (tq, tq), 1)
    q2 = q_ref[...]
    gq = g_ref[...]
    outs = []
    for hh in range(2):
        hmask = (lane < ATT_HEAD_DIM) if hh == 0 else (lane >= ATT_HEAD_DIM)
        qh = jnp.where(hmask, q2, 0.0).astype(BF16)
        cfq = _lane_col(gq, 2 * hp + hh)

        def logits(j):
            st = pl.multiple_of(j * tq, tq)
            s = _dot_nt(qh, kb[pl.ds(st, tq), :])
            return s + (cfq - r_ref[hh:hh + 1, pl.ds(st, tq)]), vb[pl.ds(st, tq), :]

        def past(j, carry):
            s, v = logits(j)
            return _softmax_block(s, v, carry)

        init = (jnp.full((tq, 1), NEG_INF, F32), jnp.zeros((tq, 1), F32), jnp.zeros((tq, LANES), F32))
        carry = lax.fori_loop(0, qi, past, init)
        s, v = logits(qi)
        _, l, acc = _softmax_block(jnp.where(col <= row, s, NEG_INF), v, carry)
        outs.append(acc / l)
    out = jnp.where(lane < ATT_HEAD_DIM, outs[0], outs[1])
    o_ref[...] = out * jax.nn.sigmoid(fo_ref[...])


def _fox_attention(qkvo, gcol, grow, tq=SEQ_TILE):
    b, s, _ = qkvo.shape
    np_ = ATT_PAIRS
    return pl.pallas_call(
        _fox_kernel,
        grid=(b, np_, s // tq),
        in_specs=[
            pl.BlockSpec((None, tq, LANES), lambda bi, hp, qi: (bi, qi, hp)),
            pl.BlockSpec((None, s, LANES), lambda bi, hp, qi: (bi, 0, np_ + hp)),
            pl.BlockSpec((None, s, LANES), lambda bi, hp, qi: (bi, 0, 2 * np_ + hp)),
            pl.BlockSpec((None, tq, LANES), lambda bi, hp, qi: (bi, qi, 3 * np_ + hp)),
            pl.BlockSpec((None, tq, LANES), lambda bi, hp, qi: (bi, qi, 0)),
            pl.BlockSpec((None, None, 8, s), lambda bi, hp, qi: (bi, hp, 0, 0)),
        ],
        out_specs=pl.BlockSpec((None, tq, LANES), lambda bi, hp, qi: (bi, qi, hp)),
        out_shape=jax.ShapeDtypeStruct((b, s, HALF), F32),
        scratch_shapes=[pltpu.VMEM((s, LANES), BF16), pltpu.VMEM((s, LANES), BF16)],
        compiler_params=_cparams(("parallel", "parallel", "arbitrary"), 32),
        name="fox_attention",
    )(qkvo, qkvo, qkvo, qkvo, gcol, grow)


def _moba_kernel(q_ref, k_ref, v_ref, o_ref, kb, vb, kmh, kml, bias):
    qi = pl.program_id(2)
    tq = q_ref.shape[0]
    s_len = k_ref.shape[0]
    nb = s_len // MOBA_BLOCK

    @pl.when(qi == 0)
    def _():
        kb[...] = k_ref[...].astype(BF16)
        vb[...] = v_ref[...].astype(BF16)
        means = [jnp.mean(k_ref[n * MOBA_BLOCK:(n + 1) * MOBA_BLOCK, :], axis=0, keepdims=True)
                 for n in range(nb)]
        km = jnp.concatenate(means + [jnp.zeros((LANES - nb, LANES), F32)], axis=0)
        hi, lo = _split2(km)
        kmh[...] = hi
        kml[...] = lo

    lane = lax.broadcasted_iota(jnp.int32, (1, LANES), 1)
    row = lax.broadcasted_iota(jnp.int32, (tq, tq), 0)
    col = lax.broadcasted_iota(jnp.int32, (tq, tq), 1)
    q2 = q_ref[...]
    outs = []
    for hh in range(2):
        hmask = (lane < ATT_HEAD_DIM) if hh == 0 else (lane >= ATT_HEAD_DIM)
        qm = jnp.where(hmask, q2, 0.0)
        q_hi, q_lo = _split2(qm)
        gate = _dot_nt(q_hi, kmh[...]) + _dot_nt(q_lo, kmh[...]) + _dot_nt(q_hi, kml[...])
        gate = jnp.where(lane < qi, gate, NEG_INF)
        rank = jnp.zeros((tq, LANES), F32)
        for m in range(nb):
            gm = gate[:, m:m + 1]
            ahead = (gm > gate) | ((gm == gate) & (lane > m))
            rank = rank + jnp.where(ahead, 1.0, 0.0)
        sel = (rank < float(MOBA_TOPK)) & (lane < qi)
        selb = jnp.where(sel, 0.0, NEG_INF)
        for n in range(nb):
            bias[hh, n] = jnp.broadcast_to(selb[:, n:n + 1], (tq, LANES))

        def logits(j):
            st = pl.multiple_of(j * tq, tq)
            return _dot_nt(q_hi, kb[pl.ds(st, tq), :]), vb[pl.ds(st, tq), :]

        s, v = logits(qi)
        init = (jnp.full((tq, 1), NEG_INF, F32), jnp.zeros((tq, 1), F32), jnp.zeros((tq, LANES), F32))
        carry = _softmax_block(jnp.where(col <= row, s, NEG_INF), v, init)

        def past(j, carry):
            s, v = logits(j)
            bj = bias[hh, j]
            return _softmax_block(s + jnp.concatenate([bj] * (tq // LANES), axis=1), v, carry)

        _, l, acc = lax.fori_loop(0, qi, past, carry)
        outs.append(acc / l)
    o_ref[...] = jnp.where(lane < ATT_HEAD_DIM, outs[0], outs[1])


def _moba_attention(qkv):
    b, s, _ = qkv.shape
    tq = MOBA_BLOCK
    np_ = ATT_PAIRS
    nb = s // MOBA_BLOCK
    return pl.pallas_call(
        _moba_kernel,
        grid=(b, np_, s // tq),
        in_specs=[
            pl.BlockSpec((None, tq, LANES), lambda bi, hp, qi: (bi, qi, hp)),
            pl.BlockSpec((None, s, LANES), lambda bi, hp, qi: (bi, 0, np_ + hp)),
            pl.BlockSpec((None, s, LANES), lambda bi, hp, qi: (bi, 0, 2 * np_ + hp)),
        ],
        out_specs=pl.BlockSpec((None, tq, LANES), lambda bi, hp, qi: (bi, qi, hp)),
        out_shape=jax.ShapeDtypeStruct((b, s, HALF), F32),
        scratch_shapes=[pltpu.VMEM((s, LANES), BF16), pltpu.VMEM((s, LANES), BF16),
                        pltpu.VMEM((LANES, LANES), BF16), pltpu.VMEM((LANES, LANES), BF16),
                        pltpu.VMEM((2, nb, tq, LANES), F32)],
        compiler_params=_cparams(("parallel", "parallel", "arbitrary"), 32),
        name="moba_attention",
    )(qkv, qkv, qkv)


def _conv_silu(x_ref, w_ref, b_ref, o_ref, scale):
    s_len, ch = x_ref.shape
    lc = SEQ_TILE
    w = w_ref[...]
    b = b_ref[...]
    row = lax.broadcasted_iota(jnp.int32, (lc, ch), 0)

    def body(c, _):
        st = pl.multiple_of(c * lc, lc)
        pst = pl.multiple_of(jnp.maximum(c - 1, 0) * lc, lc)
        cur = x_ref[pl.ds(st, lc), :]
        prev = jnp.where(c > 0, x_ref[pl.ds(pst, lc), :], 0.0)
        y = b
        for j in range(CONV_WIDTH - 1):
            sh = CONV_WIDTH - 1 - j
            shifted = jnp.where(row < sh, pltpu.roll(prev, sh, 0), pltpu.roll(cur, sh, 0))
            y = y + shifted * w[j:j + 1]
        y = y + cur * w[CONV_WIDTH - 1:CONV_WIDTH]
        y = y * jax.nn.sigmoid(y)
        o_ref[pl.ds(st, lc), :] = y * scale if scale != 1.0 else y
        return 0

    lax.fori_loop(0, s_len // lc, body, 0)


def _mlstm_kernel(q_ref, k_ref, v_ref, og_ref, cwq_ref, cwk_ref, cbq_ref, cbk_ref, g_ref, r_ref, ng_ref,
                  o_ref, qc, kc, cst):
    hh = pl.program_id(1)
    s_len, d = q_ref.shape
    ln = SEQ_TILE
    _conv_silu(q_ref, cwq_ref, cbq_ref, qc, 1.0)
    _conv_silu(k_ref, cwk_ref, cbk_ref, kc, d ** -0.5)
    cst[...] = jnp.zeros_like(cst)
    row = lax.broadcasted_iota(jnp.int32, (ln, ln), 0)
    col = lax.broadcasted_iota(jnp.int32, (ln, ln), 1)
    tri = col <= row
    lane = lax.broadcasted_iota(jnp.int32, (1, LANES), 1)
    ones_blk = jnp.broadcast_to(jnp.where(lane == 0, 1.0, 0.0), (ln, LANES))
    f_lane = ATT_HEADS + MLSTM_HEADS + hh

    def step(c, m_prev):
        st = pl.multiple_of(c * ln, ln)
        q = qc[pl.ds(st, ln), :]
        k = kc[pl.ds(st, ln), :]
        v = v_ref[pl.ds(st, ln), :]
        f_col = _lane_col(g_ref[pl.ds(st, ln), :], f_lane)
        a_row = r_ref[0:1, pl.ds(st, ln)]
        a_mat = jnp.where(tri, a_row, NEG_INF)
        m_col = jnp.maximum(m_prev, jnp.max(a_mat, axis=1, keepdims=True))
        qb = q.astype(BF16)
        w_qk = _dot_nt(qb, k.astype(BF16)) * jnp.exp(a_mat - m_col)
        v_aug = jnp.concatenate([v, ones_blk], axis=1).astype(BF16)
        c_prev = cst[...]
        inter = jnp.exp(m_prev - m_col)
        q_c = _dot(qb, c_prev.astype(BF16))
        num = _dot(w_qk.astype(BF16), v_aug[:, :d]) + inter * q_c[:, :d]
        den = jnp.sum(w_qk, axis=1, keepdims=True) + inter * q_c[:, d:d + 1]
        h = num / jnp.maximum(jnp.abs(den), jnp.exp(-f_col - m_col))
        m_end = jnp.maximum(m_prev, jnp.max(a_row, axis=1, keepdims=True))
        k_w = (k.T * jnp.exp(a_row - m_end)).astype(BF16)
        cst[...] = jnp.exp(m_prev - m_end) * c_prev + _dot(k_w, v_aug)
        hn = _rms_rows(h, ng_ref[...])
        o_ref[pl.ds(st, ln), :] = hn * jax.nn.sigmoid(og_ref[pl.ds(st, ln), :])
        return m_end

    lax.fori_loop(0, s_len // ln, step, jnp.zeros((1, 1), F32))


def _mlstm(qkvo, conv_w, conv_b, gcol, grow, norm_g):
    b, s, _ = qkvo.shape
    nh = MLSTM_HEADS
    d = MLSTM_HEAD_DIM
    big = lambda off: pl.BlockSpec((None, s, d), lambda bi, h: (bi, 0, off + h))
    return pl.pallas_call(
        _mlstm_kernel,
        grid=(b, nh),
        in_specs=[
            big(0), big(nh), big(2 * nh), big(3 * nh),
            pl.BlockSpec((CONV_WIDTH, d), lambda bi, h: (0, h)),
            pl.BlockSpec((CONV_WIDTH, d), lambda bi, h: (0, nh + h)),
            pl.BlockSpec((1, d), lambda bi, h: (0, h)),
            pl.BlockSpec((1, d), lambda bi, h: (0, nh + h)),
            pl.BlockSpec((None, s, LANES), lambda bi, h: (bi, 0, 0)),
            pl.BlockSpec((None, None, 8, s), lambda bi, h: (bi, h, 0, 0)),
            pl.BlockSpec((1, d), lambda bi, h: (0, h)),
        ],
        out_specs=pl.BlockSpec((None, s, d), lambda bi, h: (bi, 0, h)),
        out_shape=jax.ShapeDtypeStruct((b, s, HALF), F32),
        scratch_shapes=[pltpu.VMEM((s, d), F32), pltpu.VMEM((s, d), F32), pltpu.VMEM((d, 2 * d), F32)],
        compiler_params=_cparams(("parallel", "parallel"), 40),
        name="mlstm",
    )(qkvo, qkvo, qkvo, qkvo, conv_w, conv_w, conv_b, conv_b, gcol, grow, norm_g)


def _ssd_kernel(z_ref, x_ref, b_ref, c_ref, cwx_ref, cwb_ref, cwc_ref, cbx_ref, cbb_ref, cbc_ref,
                g_ref, e_ref, r_ref, dsk_ref, ng_ref, o_ref, xc, bc, cc, sst):
    s_len, gw = x_ref.shape
    ln = SEQ_TILE
    _conv_silu(x_ref, cwx_ref, cbx_ref, xc, 1.0)
    _conv_silu(b_ref, cwb_ref, cbb_ref, bc, 1.0)
    _conv_silu(c_ref, cwc_ref, cbc_ref, cc, 1.0)
    sst[...] = jnp.zeros_like(sst)
    row = lax.broadcasted_iota(jnp.int32, (ln, ln), 0)
    col = lax.broadcasted_iota(jnp.int32, (ln, ln), 1)
    tri = col <= row
    head_of_lane = _block_index(lax.broadcasted_iota(jnp.int32, (1, gw), 1), SSD_HEAD_DIM)

    def step(c, _):
        st = pl.multiple_of(c * ln, ln)
        x = xc[pl.ds(st, ln), :]
        bm = bc[pl.ds(st, ln), :]
        cm = cc[pl.ds(st, ln), :]
        z = z_ref[pl.ds(st, ln), :]
        parts = _split3(g_ref[pl.ds(st, ln), :])
        dt_e = sum(_dot(p, e_ref[0]) for p in parts)
        ac_e = sum(_dot(p, e_ref[1]) for p in parts)
        xdt = x * dt_e
        xdt_b = xdt.astype(BF16)
        cmb = cm.astype(BF16)
        cb = _dot_nt(cmb, bm.astype(BF16))
        ac_rows = r_ref[0:SSD_HEADS_PER_GROUP, pl.ds(st, ln)]
        y = jnp.zeros((ln, gw), F32)
        for hh in range(SSD_HEADS_PER_GROUP):
            ac_col = ac_e[:, hh * SSD_HEAD_DIM:hh * SSD_HEAD_DIM + 1]
            l_mat = jnp.exp(jnp.where(tri, ac_col - ac_rows[hh:hh + 1], NEG_INF))
            y = jnp.where(head_of_lane == hh, _dot((cb * l_mat).astype(BF16), xdt_b), y)
        ac_end = ac_e[ln - 1:ln, :]
        prev = sst[...]
        y = y + _dot(cmb, prev.astype(BF16)) * jnp.exp(ac_e)
        states = _dot(bm.T.astype(BF16), (xdt * jnp.exp(ac_end - ac_e)).astype(BF16))
        sst[...] = prev * jnp.exp(ac_end) + states
        y = y + dsk_ref[...] * x
        y = y * (z * jax.nn.sigmoid(z))
        o_ref[pl.ds(st, ln), :] = _rms_rows(y, ng_ref[...])
        return 0

    lax.fori_loop(0, s_len // ln, step, 0)


def _ssd(zx, bcin, conv_w, conv_b, gcol, expand, grow, d_row, norm_g):
    b, s, _ = zx.shape
    gw = SSD_GROUP_W
    ns = SSD_STATE
    ng = SSD_GROUPS
    xoff = HALF // gw
    boff = HALF // ns
    coff = (HALF + ng * ns) // ns
    return pl.pallas_call(
        _ssd_kernel,
        grid=(b, ng),
        in_specs=[
            pl.BlockSpec((None, s, gw), lambda bi, g: (bi, 0, g)),
            pl.BlockSpec((None, s, gw), lambda bi, g: (bi, 0, xoff + g)),
            pl.BlockSpec((None, s, ns), lambda bi, g: (bi, 0, g)),
            pl.BlockSpec((None, s, ns), lambda bi, g: (bi, 0, ng + g)),
            pl.BlockSpec((CONV_WIDTH, gw), lambda bi, g: (0, g)),
            pl.BlockSpec((CONV_WIDTH, ns), lambda bi, g: (0, boff + g)),
            pl.BlockSpec((CONV_WIDTH, ns), lambda bi, g: (0, coff + g)),
            pl.BlockSpec((1, gw), lambda bi, g: (0, g)),
            pl.BlockSpec((1, ns), lambda bi, g: (0, boff + g)),
            pl.BlockSpec((1, ns), lambda bi, g: (0, coff + g)),
            pl.BlockSpec((None, s, LANES), lambda bi, g: (bi, 0, 0)),
            pl.BlockSpec((None, 2, LANES, gw), lambda bi, g: (g, 0, 0, 0)),
            pl.BlockSpec((None, None, 8, s), lambda bi, g: (bi, g, 0, 0)),
            pl.BlockSpec((1, gw), lambda bi, g: (0, g)),
            pl.BlockSpec((1, gw), lambda bi, g: (0, g)),
        ],
        out_specs=pl.BlockSpec((None, s, gw), lambda bi, g: (bi, 0, g)),
        out_shape=jax.ShapeDtypeStruct((b, s, HALF), F32),
        scratch_shapes=[pltpu.VMEM((s, gw), F32), pltpu.VMEM((s, ns), F32), pltpu.VMEM((s, ns), F32),
                        pltpu.VMEM((ns, gw), F32)],
        compiler_params=_cparams(("parallel", "parallel"), 48),
        name="ssd",
    )(zx, zx, bcin, bcin, conv_w, conv_w, conv_w, conv_b, conv_b, conv_b, gcol, expand, grow, d_row, norm_g)


def _ssd_expand_matrices():
    e = np.zeros((SSD_GROUPS, 2, LANES, SSD_GROUP_W), np.float32)
    for g in range(SSD_GROUPS):
        for h in range(SSD_HEADS_PER_GROUP):
            head = g * SSD_HEADS_PER_GROUP + h
            e[g, 0, head, h * SSD_HEAD_DIM:(h + 1) * SSD_HEAD_DIM] = 1.0
            e[g, 1, SSD_HEADS + head, h * SSD_HEAD_DIM:(h + 1) * SSD_HEAD_DIM] = 1.0
    return jnp.asarray(e, BF16)


def _out_proj_kernel(a_ref, b_ref, x_ref, w_ref, o_ref):
    half = a_ref.shape[1]
    o_ref[...] = (x_ref[...] + _dot(a_ref[...].astype(BF16), w_ref[:half, :])
                  + _dot(b_ref[...].astype(BF16), w_ref[half:, :]))


def _out_proj(mix_a, mix_b, x2d, w, tm=512):
    t, d = x2d.shape
    half = mix_a.shape[1]
    return pl.pallas_call(
        _out_proj_kernel,
        grid=(t // tm,),
        in_specs=[pl.BlockSpec((tm, half), lambda i: (i, 0)),
                  pl.BlockSpec((tm, half), lambda i: (i, 0)),
                  pl.BlockSpec((tm, d), lambda i: (i, 0)),
                  pl.BlockSpec((2 * half, d), lambda i: (0, 0))],
        out_specs=pl.BlockSpec((tm, d), lambda i: (i, 0)),
        out_shape=jax.ShapeDtypeStruct((t, d), F32),
        compiler_params=_cparams(("parallel",), 32),
        name="out_proj",
    )(mix_a, mix_b, x2d, w)


def _moe_combine(logits):
    lane = lax.broadcasted_iota(jnp.int32, (1, LANES), 1)
    gl = [logits[:, g:g + 1] for g in range(MOE_GROUPS)]
    g_max = functools.reduce(jnp.maximum, gl)
    g_den = sum(jnp.exp(x - g_max) for x in gl)
    g_w = 1.0 / g_den
    taken = jnp.zeros_like(g_max) > 1.0
    is_g = []
    for g in range(MOE_GROUPS):
        hit = (gl[g] == g_max) & jnp.logical_not(taken)
        is_g.append(hit)
        taken = taken | hit
    e_in = []
    for j in range(MOE_EPG):
        v = jnp.zeros_like(g_max)
        for g in range(MOE_GROUPS):
            v = jnp.where(is_g[g], logits[:, MOE_GROUPS + g * MOE_EPG + j:MOE_GROUPS + g * MOE_EPG + j + 1], v)
        e_in.append(v)
    rank = []
    for j in range(MOE_EPG):
        r = jnp.zeros_like(g_max)
        for m in range(MOE_EPG):
            if m == j:
                continue
            ahead = (e_in[m] > e_in[j]) | ((e_in[m] == e_in[j]) & (m < j))
            r = r + jnp.where(ahead, 1.0, 0.0)
        rank.append(r)
    v0 = sum(jnp.where(rank[j] == 0.0, e_in[j], 0.0) for j in range(MOE_EPG))
    v1 = sum(jnp.where(rank[j] == 1.0, e_in[j], 0.0) for j in range(MOE_EPG))
    e1 = jnp.exp(v1 - v0)
    w0 = 1.0 / (1.0 + e1)
    w1 = e1 / (1.0 + e1)
    comb = jnp.zeros(logits.shape, F32)
    for g in range(MOE_GROUPS):
        for j in range(MOE_EPG):
            wj = jnp.where(rank[j] == 0.0, w0, jnp.where(rank[j] == 1.0, w1, 0.0))
            cw = jnp.where(is_g[g], g_w * wj, 0.0)
            comb = jnp.where(lane == g * MOE_EPG + j, cw, comb)
    return comb


def _moe_kernel(x_ref, g_ref, wrh_ref, wrl_ref, rb_ref, wgu_ref, wd_ref, o_ref, h_sc, cw_sc, acc_sc):
    e = pl.program_id(1)

    @pl.when(e == 0)
    def _():
        h = _rms_rows(x_ref[...], g_ref[...])
        h_hi, h_lo = _split2(h)
        logits = _dot(h_hi, wrh_ref[...]) + _dot(h_lo, wrh_ref[...]) + _dot(h_hi, wrl_ref[...]) + rb_ref[...]
        h_sc[...] = h_hi
        cw_sc[...] = _moe_combine(logits)
        acc_sc[...] = jnp.zeros_like(acc_sc)

    ab = _dot(h_sc[...], wgu_ref[...])
    a = ab[:, :MOE_HIDDEN]
    b = ab[:, MOE_HIDDEN:]
    hid = (a * jax.nn.sigmoid(a)) * b * _lane_col(cw_sc[...], e)
    acc_sc[...] += _dot(hid.astype(BF16), wd_ref[...])

    @pl.when(e == pl.num_programs(1) - 1)
    def _():
        o_ref[...] = x_ref[...] + acc_sc[...]


def _moe(x2d, g, wr_hi, wr_lo, rb, wgu, wd, tm=1024):
    t, d = x2d.shape
    ne = wgu.shape[0]
    return pl.pallas_call(
        _moe_kernel,
        grid=(t // tm, ne),
        in_specs=[pl.BlockSpec((tm, d), lambda i, e: (i, 0)),
                  pl.BlockSpec((1, d), lambda i, e: (0, 0)),
                  pl.BlockSpec((d, LANES), lambda i, e: (0, 0)),
                  pl.BlockSpec((d, LANES), lambda i, e: (0, 0)),
                  pl.BlockSpec((1, LANES), lambda i, e: (0, 0)),
                  pl.BlockSpec((None, d, 2 * MOE_HIDDEN), lambda i, e: (e, 0, 0)),
                  pl.BlockSpec((None, MOE_HIDDEN, d), lambda i, e: (e, 0, 0))],
        out_specs=pl.BlockSpec((tm, d), lambda i, e: (i, 0)),
        out_shape=jax.ShapeDtypeStruct((t, d), F32),
        scratch_shapes=[pltpu.VMEM((tm, d), BF16), pltpu.VMEM((tm, LANES), F32), pltpu.VMEM((tm, d), F32)],
        compiler_params=_cparams(("parallel", "arbitrary"), 48),
        name="moe",
    )(x2d, g, wr_hi, wr_lo, rb, wgu, wd)


def _ple_kernel(x_ref, p_ref, wp_ref, wg_ref, g1_ref, g2_ref, o_ref):
    x = x_ref[...]
    e = _dot(p_ref[...].astype(BF16), wp_ref[...])
    gate = jax.nn.sigmoid(_dot(_rms_rows(x, g1_ref[...]).astype(BF16), wg_ref[...]))
    o_ref[...] = x + _rms_rows(e * gate, g2_ref[...])


def _ple(x2d, p2d, wp, wg, g1, g2, tm=512):
    t, d = x2d.shape
    kp = p2d.shape[1]
    return pl.pallas_call(
        _ple_kernel,
        grid=(t // tm,),
        in_specs=[pl.BlockSpec((tm, d), lambda i: (i, 0)),
                  pl.BlockSpec((tm, kp), lambda i: (i, 0)),
                  pl.BlockSpec((kp, d), lambda i: (0, 0)),
                  pl.BlockSpec((d, d), lambda i: (0, 0)),
                  pl.BlockSpec((1, d), lambda i: (0, 0)),
                  pl.BlockSpec((1, d), lambda i: (0, 0))],
        out_specs=pl.BlockSpec((tm, d), lambda i: (i, 0)),
        out_shape=jax.ShapeDtypeStruct((t, d), F32),
        compiler_params=_cparams(("parallel",), 32),
        name="ple",
    )(x2d, p2d, wp, wg, g1, g2)


def _pad_lanes(cols, width=LANES):
    return jnp.pad(cols, ((0, 0), (0, width - cols.shape[-1])))


def _row(v):
    return v.reshape(1, -1).astype(F32)


def _even_mix(x2d, bsz, seq, norm_g, w_in, fox_b_f, fox_qn_g, fox_kn_g, conv_w, conv_b, b_i, b_f, mnorm_g):
    hw = HALF
    o = np.cumsum([0, hw, hw, hw, ATT_HEADS, hw, hw, hw, hw, MLSTM_HEADS, MLSTM_HEADS, hw])
    seg = lambda i: w_in[:, o[i]:o[i + 1]]
    fq, fk, fv, ff, fo, mq, mk, mv, mi, mf, mo = [seg(i) for i in range(11)]
    w = jnp.concatenate([fq, fk, fv, fo, mq, mk, mv, mo, _pad_lanes(jnp.concatenate([ff, mi, mf], axis=1))],
                        axis=1).astype(BF16)
    gain = jnp.concatenate([jnp.tile(fox_qn_g, ATT_HEADS) * ATT_HEAD_DIM ** -0.5, jnp.tile(fox_kn_g, ATT_HEADS)])
    fox_u, ml_u, gates = _in_proj(x2d, _row(norm_g), w, _row(gain), (4 * hw, 4 * hw, LANES), 2 * hw)
    bias = _pad_lanes(_row(jnp.concatenate([fox_b_f, b_i, b_f])))
    gcol, rf, rm = _prep_even(gates.reshape(bsz, seq, LANES), bias)
    out_a = _fox_attention(fox_u.reshape(bsz, seq, 4 * hw), gcol, rf)
    out_b = _mlstm(ml_u.reshape(bsz, seq, 4 * hw), conv_w, _row(conv_b), gcol, rm, _row(mnorm_g))
    return out_a.reshape(-1, hw), out_b.reshape(-1, hw)


def _odd_mix(x2d, bsz, seq, norm_g, w_in, moba_qn_g, moba_kn_g, conv_w, conv_b, dt_bias, a_log, d_skip, snorm_g):
    hw = HALF
    nbc = SSD_GROUPS * SSD_STATE
    o = np.cumsum([0, hw, hw, hw, hw, hw, nbc, nbc, SSD_HEADS])
    w = jnp.concatenate([w_in[:, :o[7]], _pad_lanes(w_in[:, o[7]:o[8]])], axis=1).astype(BF16)
    gain = jnp.concatenate([jnp.tile(moba_qn_g, ATT_HEADS) * ATT_HEAD_DIM ** -0.5, jnp.tile(moba_kn_g, ATT_HEADS)])
    moba_u, zx, bcin, dts = _in_proj(x2d, _row(norm_g), w, _row(gain), (3 * hw, 2 * hw, 2 * nbc, LANES), 2 * hw)
    gcol, rs = _prep_odd(dts.reshape(bsz, seq, LANES), _pad_lanes(_row(dt_bias)), _pad_lanes(_row(a_log)))
    out_c = _moba_attention(moba_u.reshape(bsz, seq, 3 * hw))
    out_d = _ssd(zx.reshape(bsz, seq, 2 * hw), bcin.reshape(bsz, seq, 2 * nbc), conv_w, _row(conv_b), gcol,
                 _ssd_expand_matrices(), rs, _row(jnp.repeat(d_skip, SSD_HEAD_DIM)), _row(snorm_g))
    return out_c.reshape(-1, hw), out_d.reshape(-1, hw)


def _moe_layer(x2d, norm_g, w_group, b_group, w_router, b_router, w_gate, w_up, w_down):
    d = x2d.shape[1]
    wr = _pad_lanes(jnp.concatenate([w_group, w_router], axis=1))
    wr_hi = wr.astype(BF16)
    wr_lo = (wr - wr_hi.astype(F32)).astype(BF16)
    rb = _pad_lanes(_row(jnp.concatenate([b_group, b_router])))
    wgu = jnp.concatenate([w_gate.reshape(MOE_EXPERTS, d, MOE_HIDDEN), w_up.reshape(MOE_EXPERTS, d, MOE_HIDDEN)],
                          axis=-1).astype(BF16)
    wd = w_down.reshape(MOE_EXPERTS, MOE_HIDDEN, d).astype(BF16)
    return _moe(x2d, _row(norm_g), wr_hi, wr_lo, rb, wgu, wd)


def kernel(x, p, norm1_g, norm2_g, ev_w_in, ev_fox_b_f, ev_fox_qn_g, ev_fox_kn_g, ev_mlstm_conv_w, ev_mlstm_conv_b, ev_mlstm_b_i, ev_mlstm_b_f, ev_mlstm_norm_g, ev_w_out, od_w_in, od_moba_qn_g, od_moba_kn_g, od_ssd_conv_w, od_ssd_conv_b, od_ssd_dt_bias, od_ssd_A_log, od_ssd_D, od_ssd_norm_g, od_w_out, moe_w_group, moe_b_group, moe_w_router, moe_b_router, moe_w_gate, moe_w_up, moe_w_down, ple_w_proj, ple_w_gate, ple_gate_norm_g, ple_out_norm_g):
    bsz, seq, d = x.shape
    depth = p.shape[0]
    x2d = x.reshape(bsz * seq, d)
    for i in range(depth):
        j = i // 2
        if i % 2 == 0:
            mix_a, mix_b = _even_mix(x2d, bsz, seq, norm1_g[i], ev_w_in[j], ev_fox_b_f[j], ev_fox_qn_g[j],
                                     ev_fox_kn_g[j], ev_mlstm_conv_w[j], ev_mlstm_conv_b[j], ev_mlstm_b_i[j],
                                     ev_mlstm_b_f[j], ev_mlstm_norm_g[j])
            w_out = ev_w_out[j]
        else:
            mix_a, mix_b = _odd_mix(x2d, bsz, seq, norm1_g[i], od_w_in[j], od_moba_qn_g[j], od_moba_kn_g[j],
                                    od_ssd_conv_w[j], od_ssd_conv_b[j], od_ssd_dt_bias[j], od_ssd_A_log[j],
                                    od_ssd_D[j], od_ssd_norm_g[j])
            w_out = od_w_out[j]
        x2d = _out_proj(mix_a, mix_b, x2d, w_out.astype(BF16))
        x2d = _moe_layer(x2d, norm2_g[i], moe_w_group[i], moe_b_group[i], moe_w_router[i], moe_b_router[i],
                         moe_w_gate[i], moe_w_up[i], moe_w_down[i])
        x2d = _ple(x2d, p[i].reshape(bsz * seq, -1), ple_w_proj[i].astype(BF16), ple_w_gate[i].astype(BF16),
                   _row(ple_gate_norm_g[i]), _row(ple_out_norm_g[i]))
    return x2d.reshape(bsz, seq, d)
```

```python
import functools

import jax
import jax.numpy as jnp
import numpy as np
from jax import lax
from jax.experimental import pallas as pl
from jax.experimental.pallas import tpu as pltpu

F32 = jnp.float32
BF16 = jnp.bfloat16
NEG_INF = float("-inf")

NORM_EPS = 1e-6
D_MODEL = 1024
HALF = D_MODEL // 2
ATT_HEAD_DIM = 64
ATT_HEADS = HALF // ATT_HEAD_DIM
ATT_PAIRS = ATT_HEADS // 2
MLSTM_HEAD_DIM = 128
MLSTM_HEADS = HALF // MLSTM_HEAD_DIM
SSD_HEAD_DIM = 64
SSD_HEADS = HALF // SSD_HEAD_DIM
SSD_GROUPS = 2
SSD_STATE = 128
SSD_GROUP_W = HALF // SSD_GROUPS
SSD_HEADS_PER_GROUP = SSD_HEADS // SSD_GROUPS
CONV_WIDTH = 4
MOBA_BLOCK = 256
MOBA_TOPK = 3
MOE_GROUPS = 4
MOE_EPG = 4
MOE_EXPERTS = MOE_GROUPS * MOE_EPG
MOE_HIDDEN = D_MODEL // 4
PLE_DIM = 256

LANES = 128
MXU_N = 256
SEQ_TILE = 256
ATT_TILE = 512
MIB = 1024 * 1024


def _cparams(sem, vmem_mib):
    return pltpu.CompilerParams(dimension_semantics=sem, vmem_limit_bytes=vmem_mib * MIB)


def _split2(x):
    hi = x.astype(BF16)
    lo = (x - hi.astype(F32)).astype(BF16)
    return hi, lo


def _split3(x):
    hi = x.astype(BF16)
    r = x - hi.astype(F32)
    mid = r.astype(BF16)
    lo = (r - mid.astype(F32)).astype(BF16)
    return hi, mid, lo


def _dot(a, b):
    return jnp.dot(a, b, preferred_element_type=F32)


def _dot_nt(a, b):
    return lax.dot_general(a, b, (((1,), (1,)), ((), ())), preferred_element_type=F32)


def _log_sigmoid(x):
    return jnp.minimum(x, 0.0) - jnp.log1p(jnp.exp(-jnp.abs(x)))


def _softplus(x):
    return jnp.maximum(x, 0.0) + jnp.log1p(jnp.exp(-jnp.abs(x)))


def _rms_rows(xf, g, eps=NORM_EPS):
    ms = jnp.mean(xf * xf, axis=-1, keepdims=True)
    return xf * lax.rsqrt(ms + eps) * g


def _block_index(i, block):
    shift = block.bit_length() - 1
    assert 1 << shift == block
    return lax.shift_right_logical(i, shift)


def _lane_col(x, idx):
    lane = lax.broadcasted_iota(jnp.int32, (1, x.shape[1]), 1)
    return jnp.sum(jnp.where(lane == idx, x, 0.0), axis=1, keepdims=True)


def _in_proj_kernel(x_ref, g_ref, w_ref, gain_ref, *out_refs, widths, n_norm, head_dim):
    h = _rms_rows(x_ref[...], g_ref[...]).astype(BF16)
    r = _block_index(lax.broadcasted_iota(jnp.int32, (MXU_N, MXU_N), 0), head_dim)
    c = _block_index(lax.broadcasted_iota(jnp.int32, (MXU_N, MXU_N), 1), head_dim)
    seg = jnp.where(r == c, 1.0, 0.0).astype(BF16)
    off = 0
    for o_ref, wdt in zip(out_refs, widths):
        for c0 in range(0, wdt, MXU_N):
            n = min(MXU_N, wdt - c0)
            y = _dot(h, w_ref[:, off + c0: off + c0 + n])
            if off + c0 < n_norm:
                hi, lo = _split2(y * y)
                ss = _dot(hi, seg) + _dot(lo, seg)
                y = y * lax.rsqrt(ss * (1.0 / head_dim) + NORM_EPS) * gain_ref[:, off + c0: off + c0 + n]
            o_ref[:, c0:c0 + n] = y
        off += wdt


def _in_proj(x2d, g, w, gain, widths, n_norm, tm=256):
    t, d = x2d.shape
    n = w.shape[1]
    kern = functools.partial(_in_proj_kernel, widths=widths, n_norm=n_norm, head_dim=ATT_HEAD_DIM)
    return pl.pallas_call(
        kern,
        grid=(t // tm,),
        in_specs=[
            pl.BlockSpec((tm, d), lambda i: (i, 0)),
            pl.BlockSpec((1, d), lambda i: (0, 0)),
            pl.BlockSpec((d, n), lambda i: (0, 0)),
            pl.BlockSpec((1, n_norm), lambda i: (0, 0)),
        ],
        out_specs=[pl.BlockSpec((tm, wd), lambda i: (i, 0)) for wd in widths],
        out_shape=[jax.ShapeDtypeStruct((t, wd), F32) for wd in widths],
        compiler_params=_cparams(("parallel",), 48),
        name="in_proj",
    )(x2d, g, w, gain)


def _tri_cumsum(vals, tri):
    hi, mid, lo = _split3(vals)
    return _dot(tri, hi) + _dot(tri, mid) + _dot(tri, lo)


def _lower_tri(n):
    r = lax.broadcasted_iota(jnp.int32, (n, n), 0)
    c = lax.broadcasted_iota(jnp.int32, (n, n), 1)
    return jnp.where(c <= r, 1.0, 0.0).astype(BF16)


def _prep_even_kernel(raw_ref, bias_ref, g_ref, rm_ref):
    s = raw_ref.shape[0]
    ch = SEQ_TILE
    lane = lax.broadcasted_iota(jnp.int32, (1, LANES), 1)
    is_cum = (lane < ATT_HEADS) | ((lane >= ATT_HEADS + MLSTM_HEADS) & (lane < ATT_HEADS + 2 * MLSTM_HEADS))
    tri = _lower_tri(ch)
    carry = jnp.zeros((1, LANES), F32)
    pad = jnp.zeros((6, ch), F32)
    for c in range(s // ch):
        z = raw_ref[c * ch:(c + 1) * ch, :] + bias_ref[...]
        cs = _tri_cumsum(jnp.where(is_cum, _log_sigmoid(z), 0.0), tri) + carry
        carry = cs[ch - 1:ch, :]
        g = jnp.where(is_cum, cs, z)
        g_ref[c * ch:(c + 1) * ch, :] = g
        gt = g.T
        for hh in range(MLSTM_HEADS):
            i_row = gt[ATT_HEADS + hh:ATT_HEADS + hh + 1]
            f_row = gt[ATT_HEADS + MLSTM_HEADS + hh:ATT_HEADS + MLSTM_HEADS + hh + 1]
            rm_ref[hh, :, c * ch:(c + 1) * ch] = jnp.concatenate([i_row - f_row, f_row, pad], axis=0)


def _prep_even(raw, bias):
    b, s, _ = raw.shape
    return pl.pallas_call(
        _prep_even_kernel,
        grid=(b,),
        in_specs=[pl.BlockSpec((None, s, LANES), lambda i: (i, 0, 0)),
                  pl.BlockSpec((1, LANES), lambda i: (0, 0))],
        out_specs=[pl.BlockSpec((None, s, LANES), lambda i: (i, 0, 0)),
                   pl.BlockSpec((None, MLSTM_HEADS, 8, s), lambda i: (i, 0, 0, 0))],
        out_shape=[jax.ShapeDtypeStruct((b, s, LANES), F32),
                   jax.ShapeDtypeStruct((b, MLSTM_HEADS, 8, s), F32)],
        compiler_params=_cparams(("parallel",), 32),
        name="prep_even",
    )(raw, bias)


def _prep_odd_kernel(raw_ref, bias_ref, alog_ref, g_ref, rs_ref):
    s = raw_ref.shape[0]
    ch = SEQ_TILE
    lane = lax.broadcasted_iota(jnp.int32, (1, LANES), 1)
    is_dt = lane < SSD_HEADS
    tri = _lower_tri(ch)
    a_row = jnp.where(is_dt, -jnp.exp(alog_ref[...]), 0.0)
    pad = jnp.zeros((8 - SSD_HEADS_PER_GROUP, ch), F32)
    for c in range(s // ch):
        dt = jnp.where(is_dt, _softplus(raw_ref[c * ch:(c + 1) * ch, :] + bias_ref[...]), 0.0)
        acum = _tri_cumsum(a_row * dt, tri)
        g = jnp.where(is_dt, dt, pltpu.roll(acum, SSD_HEADS, 1))
        g_ref[c * ch:(c + 1) * ch, :] = g
        gt = g.T
        for gi in range(SSD_GROUPS):
            lo = SSD_HEADS + gi * SSD_HEADS_PER_GROUP
            rs_ref[gi, :, c * ch:(c + 1) * ch] = jnp.concatenate([gt[lo:lo + SSD_HEADS_PER_GROUP], pad], axis=0)


def _prep_odd(raw, bias, alog):
    b, s, _ = raw.shape
    return pl.pallas_call(
        _prep_odd_kernel,
        grid=(b,),
        in_specs=[pl.BlockSpec((None, s, LANES), lambda i: (i, 0, 0)),
                  pl.BlockSpec((1, LANES), lambda i: (0, 0)),
                  pl.BlockSpec((1, LANES), lambda i: (0, 0))],
        out_specs=[pl.BlockSpec((None, s, LANES), lambda i: (i, 0, 0)),
                   pl.BlockSpec((None, SSD_GROUPS, 8, s), lambda i: (i, 0, 0, 0))],
        out_shape=[jax.ShapeDtypeStruct((b, s, LANES), F32),
                   jax.ShapeDtypeStruct((b, SSD_GROUPS, 8, s), F32)],
        compiler_params=_cparams(("parallel",), 32),
        name="prep_odd",
    )(raw, bias, alog)


def _softmax_step(s_t, v_t, carry):
    m, l, acc = carry
    m_new = jnp.maximum(m, jnp.max(s_t, axis=0, keepdims=True))
    alpha = jnp.exp(m - m_new)
    p = jnp.exp(s_t - m_new)
    l = alpha * l + jnp.sum(p, axis=0, keepdims=True)
    acc = alpha * acc + _dot(v_t, p.astype(BF16))
    return m_new, l, acc


def _softmax_init(tq):
    return (jnp.full((1, tq), NEG_INF, F32), jnp.zeros((1, tq), F32), jnp.zeros((ATT_HEAD_DIM, tq), F32))


def _stage_kv(k_ref, v_ref, kb, vt):
    kb[...] = k_ref[...].astype(BF16)
    for c in range(k_ref.shape[0] // SEQ_TILE):
        vt[:, c * SEQ_TILE:(c + 1) * SEQ_TILE] = v_ref[c * SEQ_TILE:(c + 1) * SEQ_TILE, :].T.astype(BF16)


def _head_queries(q2):
    lane = lax.broadcasted_iota(jnp.int32, (1, LANES), 1)
    return [jnp.where(lane < ATT_HEAD_DIM, q2, 0.0), jnp.where(lane >= ATT_HEAD_DIM, q2, 0.0)]


def _fox_kernel(q_ref, k_ref, v_ref, fo_ref, g_ref, o_ref, kb, vt, cfb):
    hp = pl.program_id(1)
    qi = pl.program_id(2)
    tq = q_ref.shape[0]
    s_len = k_ref.shape[0]
    hd = ATT_HEAD_DIM

    @pl.when(qi == 0)
    def _():
        _stage_kv(k_ref, v_ref, kb, vt)
        g = g_ref[...]
        for hh in range(2):
            cfb[hh] = jnp.broadcast_to(_lane_col(g, 2 * hp + hh), (s_len, LANES))

    qh = [q.astype(BF16) for q in _head_queries(q_ref[...])]

    def logits(j, hh):
        st = pl.multiple_of(j * tq, tq)
        s_t = _dot_nt(kb[pl.ds(st, tq), :], qh[hh])
        return s_t - jnp.concatenate([cfb[hh, pl.ds(st, tq), :]] * (tq // LANES), axis=1)

    def v_rows(j, hh):
        return vt[hh * hd:(hh + 1) * hd, pl.ds(pl.multiple_of(j * tq, tq), tq)]

    def past(j, carry):
        return tuple(_softmax_step(logits(j, hh), v_rows(j, hh), carry[hh]) for hh in range(2))

    carry = lax.fori_loop(0, qi, past, (_softmax_init(tq), _softmax_init(tq)))
    krow = lax.broadcasted_iota(jnp.int32, (tq, tq), 0)
    qcol = lax.broadcasted_iota(jnp.int32, (tq, tq), 1)
    outs = []
    for hh in range(2):
        s_t = jnp.where(krow <= qcol, logits(qi, hh), NEG_INF)
        _, l, acc = _softmax_step(s_t, v_rows(qi, hh), carry[hh])
        outs.append(acc / l)
    o_ref[...] = jnp.concatenate(outs, axis=0).T * jax.nn.sigmoid(fo_ref[...])


def _fox_attention(qkvo, gcol, tq=ATT_TILE):
    b, s, _ = qkvo.shape
    np_ = ATT_PAIRS
    return pl.pallas_call(
        _fox_kernel,
        grid=(b, np_, s // tq),
        in_specs=[
            pl.BlockSpec((None, tq, LANES), lambda bi, hp, qi: (bi, qi, hp)),
            pl.BlockSpec((None, s, LANES), lambda bi, hp, qi: (bi, 0, np_ + hp)),
            pl.BlockSpec((None, s, LANES), lambda bi, hp, qi: (bi, 0, 2 * np_ + hp)),
            pl.BlockSpec((None, tq, LANES), lambda bi, hp, qi: (bi, qi, 3 * np_ + hp)),
            pl.BlockSpec((None, s, LANES), lambda bi, hp, qi: (bi, 0, 0)),
        ],
        out_specs=pl.BlockSpec((None, tq, LANES), lambda bi, hp, qi: (bi, qi, hp)),
        out_shape=jax.ShapeDtypeStruct((b, s, HALF), F32),
        scratch_shapes=[pltpu.VMEM((s, LANES), BF16), pltpu.VMEM((LANES, s), BF16),
                        pltpu.VMEM((2, s, LANES), F32)],
        compiler_params=_cparams(("parallel", "parallel", "arbitrary"), 32),
        name="fox_attention",
    )(qkvo, qkvo, qkvo, qkvo, gcol)


def _moba_kernel(q_ref, k_ref, v_ref, o_ref, kb, vt, kmh, kml, bias):
    qi = pl.program_id(2)
    tq = q_ref.shape[0]
    s_len = k_ref.shape[0]
    blk = MOBA_BLOCK
    nb = s_len // blk
    nbp = kmh.shape[0]
    hd = ATT_HEAD_DIM
    per_tile = tq // blk

    @pl.when(qi == 0)
    def _():
        _stage_kv(k_ref, v_ref, kb, vt)
        means = [jnp.mean(k_ref[n * blk:(n + 1) * blk, :], axis=0, keepdims=True) for n in range(nb)]
        km = jnp.concatenate(means + [jnp.zeros((nbp - nb, LANES), F32)], axis=0)
        kmh[...], kml[...] = _split2(km)

    qcol1 = lax.broadcasted_iota(jnp.int32, (1, tq), 1)
    q_blk = per_tile * qi + _block_index(qcol1, blk)
    nrow = lax.broadcasted_iota(jnp.int32, (nbp, tq), 0)
    valid = nrow < q_blk
    qh = []
    for hh, qm in enumerate(_head_queries(q_ref[...])):
        q_hi, q_lo = _split2(qm)
        qh.append(q_hi)
        gate = _dot_nt(kmh[...], q_hi) + _dot_nt(kmh[...], q_lo) + _dot_nt(kml[...], q_hi)
        gate = jnp.where(valid, gate, NEG_INF)
        rank = jnp.zeros((nbp, tq), F32)
        for m in range(nb):
            gm = gate[m:m + 1, :]
            rank = rank + jnp.where((gm > gate) | ((gm == gate) & (nrow > m)), 1.0, 0.0)
        selb = jnp.where((rank < float(MOBA_TOPK)) & valid, 0.0, NEG_INF)
        for n in range(nb):
            bias[hh, n] = jnp.broadcast_to(selb[n:n + 1, :], (8, tq))

    def v_rows(st, size, hh):
        return vt[hh * hd:(hh + 1) * hd, pl.ds(st, size)]

    st = pl.multiple_of(qi * tq, tq)
    krow = lax.broadcasted_iota(jnp.int32, (tq, tq), 0)
    qcol = lax.broadcasted_iota(jnp.int32, (tq, tq), 1)
    same_blk = _block_index(krow, blk) == _block_index(qcol, blk)
    carry = []
    for hh in range(2):
        s_t = _dot_nt(kb[pl.ds(st, tq), :], qh[hh])
        sel_rows = jnp.concatenate(
            [jnp.broadcast_to(bias[hh, per_tile * qi + n][0:1, :], (blk, tq)) for n in range(per_tile)], axis=0)
        mask = jnp.where(krow <= qcol, jnp.where(same_blk, 0.0, sel_rows), NEG_INF)
        carry.append(_softmax_step(s_t + mask, v_rows(st, tq, hh), _softmax_init(tq)))

    def past(n, carry):
        stn = pl.multiple_of(n * blk, blk)
        new = []
        for hh in range(2):
            s_t = _dot_nt(kb[pl.ds(stn, blk), :], qh[hh]) + bias[hh, n][0:1, :]
            new.append(_softmax_step(s_t, v_rows(stn, blk, hh), carry[hh]))
        return tuple(new)

    carry = lax.fori_loop(0, per_tile * qi, past, tuple(carry))
    o_ref[...] = jnp.concatenate([acc / l for _, l, acc in carry], axis=0).T


def _moba_attention(qkv, tq=ATT_TILE):
    b, s, _ = qkv.shape
    np_ = ATT_PAIRS
    nb = s // MOBA_BLOCK
    nbp = -(-nb // 16) * 16
    return pl.pallas_call(
        _moba_kernel,
        grid=(b, np_, s // tq),
        in_specs=[
            pl.BlockSpec((None, tq, LANES), lambda bi, hp, qi: (bi, qi, hp)),
            pl.BlockSpec((None, s, LANES), lambda bi, hp, qi: (bi, 0, np_ + hp)),
            pl.BlockSpec((None, s, LANES), lambda bi, hp, qi: (bi, 0, 2 * np_ + hp)),
        ],
        out_specs=pl.BlockSpec((None, tq, LANES), lambda bi, hp, qi: (bi, qi, hp)),
        out_shape=jax.ShapeDtypeStruct((b, s, HALF), F32),
        scratch_shapes=[pltpu.VMEM((s, LANES), BF16), pltpu.VMEM((LANES, s), BF16),
                        pltpu.VMEM((nbp, LANES), BF16), pltpu.VMEM((nbp, LANES), BF16),
                        pltpu.VMEM((2, nb, 8, tq), F32)],
        compiler_params=_cparams(("parallel", "parallel", "arbitrary"), 32),
        name="moba_attention",
    )(qkv, qkv, qkv)


def _conv_silu(x_ref, w_ref, b_ref, o_ref, scale):
    s_len, ch = x_ref.shape
    lc = SEQ_TILE
    w = w_ref[...]
    b = b_ref[...]
    row = lax.broadcasted_iota(jnp.int32, (lc, ch), 0)

    def body(c, _):
        st = pl.multiple_of(c * lc, lc)
        pst = pl.multiple_of(jnp.maximum(c - 1, 0) * lc, lc)
        cur = x_ref[pl.ds(st, lc), :]
        prev = jnp.where(c > 0, x_ref[pl.ds(pst, lc), :], 0.0)
        y = b
        for j in range(CONV_WIDTH - 1):
            sh = CONV_WIDTH - 1 - j
            shifted = jnp.where(row < sh, pltpu.roll(prev, sh, 0), pltpu.roll(cur, sh, 0))
            y = y + shifted * w[j:j + 1]
        y = y + cur * w[CONV_WIDTH - 1:CONV_WIDTH]
        y = y * jax.nn.sigmoid(y)
        o_ref[pl.ds(st, lc), :] = y * scale if scale != 1.0 else y
        return 0

    lax.fori_loop(0, s_len // lc, body, 0)


def _mlstm_kernel(q_ref, k_ref, v_ref, og_ref, cwq_ref, cwk_ref, cbq_ref, cbk_ref, g_ref, r_ref, ng_ref,
                  o_ref, qc, kc, cst):
    hh = pl.program_id(1)
    s_len, d = q_ref.shape
    ln = SEQ_TILE
    _conv_silu(q_ref, cwq_ref, cbq_ref, qc, 1.0)
    _conv_silu(k_ref, cwk_ref, cbk_ref, kc, d ** -0.5)
    cst[...] = jnp.zeros_like(cst)
    row = lax.broadcasted_iota(jnp.int32, (ln, ln), 0)
    col = lax.broadcasted_iota(jnp.int32, (ln, ln), 1)
    tri = col <= row
    lane = lax.broadcasted_iota(jnp.int32, (1, LANES), 1)
    ones_blk = jnp.broadcast_to(jnp.where(lane == 0, 1.0, 0.0), (ln, LANES))
    f_lane = ATT_HEADS + MLSTM_HEADS + hh

    def step(c, m_prev):
        st = pl.multiple_of(c * ln, ln)
        q = qc[pl.ds(st, ln), :]
        k = kc[pl.ds(st, ln), :]
        v = v_ref[pl.ds(st, ln), :]
        f_col = _lane_col(g_ref[pl.ds(st, ln), :], f_lane)
        a_row = r_ref[0:1, pl.ds(st, ln)]
        a_mat = jnp.where(tri, a_row, NEG_INF)
        m_col = jnp.maximum(m_prev, jnp.max(a_mat, axis=1, keepdims=True))
        qb = q.astype(BF16)
        w_qk = _dot_nt(qb, k.astype(BF16)) * jnp.exp(a_mat - m_col)
        v_aug = jnp.concatenate([v, ones_blk], axis=1).astype(BF16)
        c_prev = cst[...]
        inter = jnp.exp(m_prev - m_col)
        q_c = _dot(qb, c_prev.astype(BF16))
        num = _dot(w_qk.astype(BF16), v_aug[:, :d]) + inter * q_c[:, :d]
        den = jnp.sum(w_qk, axis=1, keepdims=True) + inter * q_c[:, d:d + 1]
        h = num / jnp.maximum(jnp.abs(den), jnp.exp(-f_col - m_col))
        m_end = jnp.maximum(m_prev, jnp.max(a_row, axis=1, keepdims=True))
        k_w = (k.T * jnp.exp(a_row - m_end)).astype(BF16)
        cst[...] = jnp.exp(m_prev - m_end) * c_prev + _dot(k_w, v_aug)
        hn = _rms_rows(h, ng_ref[...])
        o_ref[pl.ds(st, ln), :] = hn * jax.nn.sigmoid(og_ref[pl.ds(st, ln), :])
        return m_end

    lax.fori_loop(0, s_len // ln, step, jnp.zeros((1, 1), F32))


def _mlstm(qkvo, conv_w, conv_b, gcol, grow, norm_g):
    b, s, _ = qkvo.shape
    nh = MLSTM_HEADS
    d = MLSTM_HEAD_DIM
    big = lambda off: pl.BlockSpec((None, s, d), lambda bi, h: (bi, 0, off + h))
    return pl.pallas_call(
        _mlstm_kernel,
        grid=(b, nh),
        in_specs=[
            big(0), big(nh), big(2 * nh), big(3 * nh),
            pl.BlockSpec((CONV_WIDTH, d), lambda bi, h: (0, h)),
            pl.BlockSpec((CONV_WIDTH, d), lambda bi, h: (0, nh + h)),
            pl.BlockSpec((1, d), lambda bi, h: (0, h)),
            pl.BlockSpec((1, d), lambda bi, h: (0, nh + h)),
            pl.BlockSpec((None, s, LANES), lambda bi, h: (bi, 0, 0)),
            pl.BlockSpec((None, None, 8, s), lambda bi, h: (bi, h, 0, 0)),
            pl.BlockSpec((1, d), lambda bi, h: (0, h)),
        ],
        out_specs=pl.BlockSpec((None, s, d), lambda bi, h: (bi, 0, h)),
        out_shape=jax.ShapeDtypeStruct((b, s, HALF), F32),
        scratch_shapes=[pltpu.VMEM((s, d), F32), pltpu.VMEM((s, d), F32), pltpu.VMEM((d, 2 * d), F32)],
        compiler_params=_cparams(("parallel", "parallel"), 40),
        name="mlstm",
    )(qkvo, qkvo, qkvo, qkvo, conv_w, conv_w, conv_b, conv_b, gcol, grow, norm_g)


def _ssd_kernel(z_ref, x_ref, b_ref, c_ref, cwx_ref, cwb_ref, cwc_ref, cbx_ref, cbb_ref, cbc_ref,
                g_ref, e_ref, r_ref, dsk_ref, ng_ref, o_ref, xc, bc, cc, sst):
    s_len, gw = x_ref.shape
    ln = SEQ_TILE
    _conv_silu(x_ref, cwx_ref, cbx_ref, xc, 1.0)
    _conv_silu(b_ref, cwb_ref, cbb_ref, bc, 1.0)
    _conv_silu(c_ref, cwc_ref, cbc_ref, cc, 1.0)
    sst[...] = jnp.zeros_like(sst)
    row = lax.broadcasted_iota(jnp.int32, (ln, ln), 0)
    col = lax.broadcasted_iota(jnp.int32, (ln, ln), 1)
    tri = col <= row
    head_of_lane = _block_index(lax.broadcasted_iota(jnp.int32, (1, gw), 1), SSD_HEAD_DIM)

    def step(c, _):
        st = pl.multiple_of(c * ln, ln)
        x = xc[pl.ds(st, ln), :]
        bm = bc[pl.ds(st, ln), :]
        cm = cc[pl.ds(st, ln), :]
        z = z_ref[pl.ds(st, ln), :]
        parts = _split3(g_ref[pl.ds(st, ln), :])
        dt_e = sum(_dot(p, e_ref[0]) for p in parts)
        ac_e = sum(_dot(p, e_ref[1]) for p in parts)
        xdt = x * dt_e
        xdt_b = xdt.astype(BF16)
        cmb = cm.astype(BF16)
        cb = _dot_nt(cmb, bm.astype(BF16))
        ac_rows = r_ref[0:SSD_HEADS_PER_GROUP, pl.ds(st, ln)]
        y = jnp.zeros((ln, gw), F32)
        for hh in range(SSD_HEADS_PER_GROUP):
            ac_col = ac_e[:, hh * SSD_HEAD_DIM:hh * SSD_HEAD_DIM + 1]
            l_mat = jnp.exp(jnp.where(tri, ac_col - ac_rows[hh:hh + 1], NEG_INF))
            y = jnp.where(head_of_lane == hh, _dot((cb * l_mat).astype(BF16), xdt_b), y)
        ac_end = ac_e[ln - 1:ln, :]
        prev = sst[...]
        y = y + _dot(cmb, prev.astype(BF16)) * jnp.exp(ac_e)
        states = _dot(bm.T.astype(BF16), (xdt * jnp.exp(ac_end - ac_e)).astype(BF16))
        sst[...] = prev * jnp.exp(ac_end) + states
        y = y + dsk_ref[...] * x
        y = y * (z * jax.nn.sigmoid(z))
        o_ref[pl.ds(st, ln), :] = _rms_rows(y, ng_ref[...])
        return 0

    lax.fori_loop(0, s_len // ln, step, 0)


def _ssd(zx, bcin, conv_w, conv_b, gcol, expand, grow, d_row, norm_g):
    b, s, _ = zx.shape
    gw = SSD_GROUP_W
    ns = SSD_STATE
    ng = SSD_GROUPS
    xoff = HALF // gw
    boff = HALF // ns
    coff = (HALF + ng * ns) // ns
    return pl.pallas_call(
        _ssd_kernel,
        grid=(b, ng),
        in_specs=[
            pl.BlockSpec((None, s, gw), lambda bi, g: (bi, 0, g)),
            pl.BlockSpec((None, s, gw), lambda bi, g: (bi, 0, xoff + g)),
            pl.BlockSpec((None, s, ns), lambda bi, g: (bi, 0, g)),
            pl.BlockSpec((None, s, ns), lambda bi, g: (bi, 0, ng + g)),
            pl.BlockSpec((CONV_WIDTH, gw), lambda bi, g: (0, g)),
            pl.BlockSpec((CONV_WIDTH, ns), lambda bi, g: (0, boff + g)),
            pl.BlockSpec((CONV_WIDTH, ns), lambda bi, g: (0, coff + g)),
            pl.BlockSpec((1, gw), lambda bi, g: (0, g)),
            pl.BlockSpec((1, ns), lambda bi, g: (0, boff + g)),
            pl.BlockSpec((1, ns), lambda bi, g: (0, coff + g)),
            pl.BlockSpec((None, s, LANES), lambda bi, g: (bi, 0, 0)),
            pl.BlockSpec((None, 2, LANES, gw), lambda bi, g: (g, 0, 0, 0)),
            pl.BlockSpec((None, None, 8, s), lambda bi, g: (bi, g, 0, 0)),
            pl.BlockSpec((1, gw), lambda bi, g: (0, g)),
            pl.BlockSpec((1, gw), lambda bi, g: (0, g)),
        ],
        out_specs=pl.BlockSpec((None, s, gw), lambda bi, g: (bi, 0, g)),
        out_shape=jax.ShapeDtypeStruct((b, s, HALF), F32),
        scratch_shapes=[pltpu.VMEM((s, gw), F32), pltpu.VMEM((s, ns), F32), pltpu.VMEM((s, ns), F32),
                        pltpu.VMEM((ns, gw), F32)],
        compiler_params=_cparams(("parallel", "parallel"), 48),
        name="ssd",
    )(zx, zx, bcin, bcin, conv_w, conv_w, conv_w, conv_b, conv_b, conv_b, gcol, expand, grow, d_row, norm_g)


def _ssd_expand_matrices():
    e = np.zeros((SSD_GROUPS, 2, LANES, SSD_GROUP_W), np.float32)
    for g in range(SSD_GROUPS):
        for h in range(SSD_HEADS_PER_GROUP):
            head = g * SSD_HEADS_PER_GROUP + h
            e[g, 0, head, h * SSD_HEAD_DIM:(h + 1) * SSD_HEAD_DIM] = 1.0
            e[g, 1, SSD_HEADS + head, h * SSD_HEAD_DIM:(h + 1) * SSD_HEAD_DIM] = 1.0
    return jnp.asarray(e, BF16)


def _out_proj_kernel(a_ref, b_ref, x_ref, w_ref, o_ref):
    half = a_ref.shape[1]
    o_ref[...] = (x_ref[...] + _dot(a_ref[...].astype(BF16), w_ref[:half, :])
                  + _dot(b_ref[...].astype(BF16), w_ref[half:, :]))


def _out_proj(mix_a, mix_b, x2d, w, tm=512):
    t, d = x2d.shape
    half = mix_a.shape[1]
    return pl.pallas_call(
        _out_proj_kernel,
        grid=(t // tm,),
        in_specs=[pl.BlockSpec((tm, half), lambda i: (i, 0)),
                  pl.BlockSpec((tm, half), lambda i: (i, 0)),
                  pl.BlockSpec((tm, d), lambda i: (i, 0)),
                  pl.BlockSpec((2 * half, d), lambda i: (0, 0))],
        out_specs=pl.BlockSpec((tm, d), lambda i: (i, 0)),
        out_shape=jax.ShapeDtypeStruct((t, d), F32),
        compiler_params=_cparams(("parallel",), 32),
        name="out_proj",
    )(mix_a, mix_b, x2d, w)


def _moe_combine(logits):
    lane = lax.broadcasted_iota(jnp.int32, (1, LANES), 1)
    gl = [logits[:, g:g + 1] for g in range(MOE_GROUPS)]
    g_max = functools.reduce(jnp.maximum, gl)
    g_den = sum(jnp.exp(x - g_max) for x in gl)
    g_w = 1.0 / g_den
    taken = jnp.zeros_like(g_max) > 1.0
    is_g = []
    for g in range(MOE_GROUPS):
        hit = (gl[g] == g_max) & jnp.logical_not(taken)
        is_g.append(hit)
        taken = taken | hit
    e_in = []
    for j in range(MOE_EPG):
        v = jnp.zeros_like(g_max)
        for g in range(MOE_GROUPS):
            v = jnp.where(is_g[g], logits[:, MOE_GROUPS + g * MOE_EPG + j:MOE_GROUPS + g * MOE_EPG + j + 1], v)
        e_in.append(v)
    rank = []
    for j in range(MOE_EPG):
        r = jnp.zeros_like(g_max)
        for m in range(MOE_EPG):
            if m == j:
                continue
            ahead = (e_in[m] > e_in[j]) | ((e_in[m] == e_in[j]) & (m < j))
            r = r + jnp.where(ahead, 1.0, 0.0)
        rank.append(r)
    v0 = sum(jnp.where(rank[j] == 0.0, e_in[j], 0.0) for j in range(MOE_EPG))
    v1 = sum(jnp.where(rank[j] == 1.0, e_in[j], 0.0) for j in range(MOE_EPG))
    e1 = jnp.exp(v1 - v0)
    w0 = 1.0 / (1.0 + e1)
    w1 = e1 / (1.0 + e1)
    comb = jnp.zeros(logits.shape, F32)
    for g in range(MOE_GROUPS):
        for j in range(MOE_EPG):
            wj = jnp.where(rank[j] == 0.0, w0, jnp.where(rank[j] == 1.0, w1, 0.0))
            cw = jnp.where(is_g[g], g_w * wj, 0.0)
            comb = jnp.where(lane == g * MOE_EPG + j, cw, comb)
    return comb


def _moe_kernel(x_ref, g_ref, wrh_ref, wrl_ref, rb_ref, wgu_ref, wd_ref, o_ref, h_sc, cw_sc, acc_sc):
    e = pl.program_id(1)

    @pl.when(e == 0)
    def _():
        h = _rms_rows(x_ref[...], g_ref[...])
        h_hi, h_lo = _split2(h)
        logits = _dot(h_hi, wrh_ref[...]) + _dot(h_lo, wrh_ref[...]) + _dot(h_hi, wrl_ref[...]) + rb_ref[...]
        h_sc[...] = h_hi
        cw_sc[...] = _moe_combine(logits)
        acc_sc[...] = jnp.zeros_like(acc_sc)

    ab = _dot(h_sc[...], wgu_ref[...])
    a = ab[:, :MOE_HIDDEN]
    b = ab[:, MOE_HIDDEN:]
    hid = (a * jax.nn.sigmoid(a)) * b * _lane_col(cw_sc[...], e)
    acc_sc[...] += _dot(hid.astype(BF16), wd_ref[...])

    @pl.when(e == pl.num_programs(1) - 1)
    def _():
        o_ref[...] = x_ref[...] + acc_sc[...]


def _moe(x2d, g, wr_hi, wr_lo, rb, wgu, wd, tm=1024):
    t, d = x2d.shape
    ne = wgu.shape[0]
    return pl.pallas_call(
        _moe_kernel,
        grid=(t // tm, ne),
        in_specs=[pl.BlockSpec((tm, d), lambda i, e: (i, 0)),
                  pl.BlockSpec((1, d), lambda i, e: (0, 0)),
                  pl.BlockSpec((d, LANES), lambda i, e: (0, 0)),
                  pl.BlockSpec((d, LANES), lambda i, e: (0, 0)),
                  pl.BlockSpec((1, LANES), lambda i, e: (0, 0)),
                  pl.BlockSpec((None, d, 2 * MOE_HIDDEN), lambda i, e: (e, 0, 0)),
                  pl.BlockSpec((None, MOE_HIDDEN, d), lambda i, e: (e, 0, 0))],
        out_specs=pl.BlockSpec((tm, d), lambda i, e: (i, 0)),
        out_shape=jax.ShapeDtypeStruct((t, d), F32),
        scratch_shapes=[pltpu.VMEM((tm, d), BF16), pltpu.VMEM((tm, LANES), F32), pltpu.VMEM((tm, d), F32)],
        compiler_params=_cparams(("parallel", "arbitrary"), 48),
        name="moe",
    )(x2d, g, wr_hi, wr_lo, rb, wgu, wd)


def _ple_kernel(x_ref, p_ref, wp_ref, wg_ref, g1_ref, g2_ref, o_ref):
    x = x_ref[...]
    e = _dot(p_ref[...].astype(BF16), wp_ref[...])
    gate = jax.nn.sigmoid(_dot(_rms_rows(x, g1_ref[...]).astype(BF16), wg_ref[...]))
    o_ref[...] = x + _rms_rows(e * gate, g2_ref[...])


def _ple(x2d, p2d, wp, wg, g1, g2, tm=512):
    t, d = x2d.shape
    kp = p2d.shape[1]
    return pl.pallas_call(
        _ple_kernel,
        grid=(t // tm,),
        in_specs=[pl.BlockSpec((tm, d), lambda i: (i, 0)),
                  pl.BlockSpec((tm, kp), lambda i: (i, 0)),
                  pl.BlockSpec((kp, d), lambda i: (0, 0)),
                  pl.BlockSpec((d, d), lambda i: (0, 0)),
                  pl.BlockSpec((1, d), lambda i: (0, 0)),
                  pl.BlockSpec((1, d), lambda i: (0, 0))],
        out_specs=pl.BlockSpec((tm, d), lambda i: (i, 0)),
        out_shape=jax.ShapeDtypeStruct((t, d), F32),
        compiler_params=_cparams(("parallel",), 32),
        name="ple",
    )(x2d, p2d, wp, wg, g1, g2)


def _pad_lanes(cols, width=LANES):
    return jnp.pad(cols, ((0, 0), (0, width - cols.shape[-1])))


def _row(v):
    return v.reshape(1, -1).astype(F32)


def _even_mix(x2d, bsz, seq, norm_g, w_in, fox_b_f, fox_qn_g, fox_kn_g, conv_w, conv_b, b_i, b_f, mnorm_g):
    hw = HALF
    o = np.cumsum([0, hw, hw, hw, ATT_HEADS, hw, hw, hw, hw, MLSTM_HEADS, MLSTM_HEADS, hw])
    seg = lambda i: w_in[:, o[i]:o[i + 1]]
    fq, fk, fv, ff, fo, mq, mk, mv, mi, mf, mo = [seg(i) for i in range(11)]
    w = jnp.concatenate([fq, fk, fv, fo, mq, mk, mv, mo, _pad_lanes(jnp.concatenate([ff, mi, mf], axis=1))],
                        axis=1).astype(BF16)
    gain = jnp.concatenate([jnp.tile(fox_qn_g, ATT_HEADS) * ATT_HEAD_DIM ** -0.5, jnp.tile(fox_kn_g, ATT_HEADS)])
    fox_u, ml_u, gates = _in_proj(x2d, _row(norm_g), w, _row(gain), (4 * hw, 4 * hw, LANES), 2 * hw)
    bias = _pad_lanes(_row(jnp.concatenate([fox_b_f, b_i, b_f])))
    gcol, rm = _prep_even(gates.reshape(bsz, seq, LANES), bias)
    out_a = _fox_attention(fox_u.reshape(bsz, seq, 4 * hw), gcol)
    out_b = _mlstm(ml_u.reshape(bsz, seq, 4 * hw), conv_w, _row(conv_b), gcol, rm, _row(mnorm_g))
    return out_a.reshape(-1, hw), out_b.reshape(-1, hw)


def _odd_mix(x2d, bsz, seq, norm_g, w_in, moba_qn_g, moba_kn_g, conv_w, conv_b, dt_bias, a_log, d_skip, snorm_g):
    hw = HALF
    nbc = SSD_GROUPS * SSD_STATE
    o = np.cumsum([0, hw, hw, hw, hw, hw, nbc, nbc, SSD_HEADS])
    w = jnp.concatenate([w_in[:, :o[7]], _pad_lanes(w_in[:, o[7]:o[8]])], axis=1).astype(BF16)
    gain = jnp.concatenate([jnp.tile(moba_qn_g, ATT_HEADS) * ATT_HEAD_DIM ** -0.5, jnp.tile(moba_kn_g, ATT_HEADS)])
    moba_u, zx, bcin, dts = _in_proj(x2d, _row(norm_g), w, _row(gain), (3 * hw, 2 * hw, 2 * nbc, LANES), 2 * hw)
    gcol, rs = _prep_odd(dts.reshape(bsz, seq, LANES), _pad_lanes(_row(dt_bias)), _pad_lanes(_row(a_log)))
    out_c = _moba_attention(moba_u.reshape(bsz, seq, 3 * hw))
    out_d = _ssd(zx.reshape(bsz, seq, 2 * hw), bcin.reshape(bsz, seq, 2 * nbc), conv_w, _row(conv_b), gcol,
                 _ssd_expand_matrices(), rs, _row(jnp.repeat(d_skip, SSD_HEAD_DIM)), _row(snorm_g))
    return out_c.reshape(-1, hw), out_d.reshape(-1, hw)


def _moe_layer(x2d, norm_g, w_group, b_group, w_router, b_router, w_gate, w_up, w_down):
    d = x2d.shape[1]
    wr = _pad_lanes(jnp.concatenate([w_group, w_router], axis=1))
    wr_hi = wr.astype(BF16)
    wr_lo = (wr - wr_hi.astype(F32)).astype(BF16)
    rb = _pad_lanes(_row(jnp.concatenate([b_group, b_router])))
    wgu = jnp.concatenate([w_gate.reshape(MOE_EXPERTS, d, MOE_HIDDEN), w_up.reshape(MOE_EXPERTS, d, MOE_HIDDEN)],
                          axis=-1).astype(BF16)
    wd = w_down.reshape(MOE_EXPERTS, MOE_HIDDEN, d).astype(BF16)
    return _moe(x2d, _row(norm_g), wr_hi, wr_lo, rb, wgu, wd)


def kernel(x, p, norm1_g, norm2_g, ev_w_in, ev_fox_b_f, ev_fox_qn_g, ev_fox_kn_g, ev_mlstm_conv_w, ev_mlstm_conv_b, ev_mlstm_b_i, ev_mlstm_b_f, ev_mlstm_norm_g, ev_w_out, od_w_in, od_moba_qn_g, od_moba_kn_g, od_ssd_conv_w, od_ssd_conv_b, od_ssd_dt_bias, od_ssd_A_log, od_ssd_D, od_ssd_norm_g, od_w_out, moe_w_group, moe_b_group, moe_w_router, moe_b_router, moe_w_gate, moe_w_up, moe_w_down, ple_w_proj, ple_w_gate, ple_gate_norm_g, ple_out_norm_g):
    bsz, seq, d = x.shape
    depth = p.shape[0]
    x2d = x.reshape(bsz * seq, d)
    for i in range(depth):
        j = i // 2
        if i % 2 == 0:
            mix_a, mix_b = _even_mix(x2d, bsz, seq, norm1_g[i], ev_w_in[j], ev_fox_b_f[j], ev_fox_qn_g[j],
                                     ev_fox_kn_g[j], ev_mlstm_conv_w[j], ev_mlstm_conv_b[j], ev_mlstm_b_i[j],
                                     ev_mlstm_b_f[j], ev_mlstm_norm_g[j])
            w_out = ev_w_out[j]
        else:
            mix_a, mix_b = _odd_mix(x2d, bsz, seq, norm1_g[i], od_w_in[j], od_moba_qn_g[j], od_moba_kn_g[j],
                                    od_ssd_conv_w[j], od_ssd_conv_b[j], od_ssd_dt_bias[j], od_ssd_A_log[j],
                                    od_ssd_D[j], od_ssd_norm_g[j])
            w_out = od_w_out[j]
        x2d = _out_proj(mix_a, mix_b, x2d, w_out.astype(BF16))
        x2d = _moe_layer(x2d, norm2_g[i], moe_w_group[i], moe_b_group[i], moe_w_router[i], moe_b_router[i],
                         moe_w_gate[i], moe_w_up[i], moe_w_down[i])
        x2d = _ple(x2d, p[i].reshape(bsz * seq, -1), ple_w_proj[i].astype(BF16), ple_w_gate[i].astype(BF16),
                   _row(ple_gate_norm_g[i]), _row(ple_out_norm_g[i]))
    return x2d.reshape(bsz, seq, d)
```

```python
import functools

import jax
import jax.numpy as jnp
import numpy as np
from jax import lax
from jax.experimental import pallas as pl
from jax.experimental.pallas import tpu as pltpu

F32 = jnp.float32
BF16 = jnp.bfloat16
NEG_INF = float("-inf")
LOG2E = 1.4426950408889634

NORM_EPS = 1e-6
D_MODEL = 1024
HALF = D_MODEL // 2
ATT_HEAD_DIM = 64
ATT_HEADS = HALF // ATT_HEAD_DIM
ATT_PAIRS = ATT_HEADS // 2
ATT_Q_SCALE = ATT_HEAD_DIM ** -0.5 * LOG2E
MLSTM_HEAD_DIM = 128
MLSTM_HEADS = HALF // MLSTM_HEAD_DIM
SSD_HEAD_DIM = 64
SSD_HEADS = HALF // SSD_HEAD_DIM
SSD_GROUPS = 2
SSD_STATE = 128
SSD_GROUP_W = HALF // SSD_GROUPS
SSD_HEADS_PER_GROUP = SSD_HEADS // SSD_GROUPS
CONV_WIDTH = 4
MOBA_BLOCK = 256
MOBA_TOPK = 3
MOE_GROUPS = 4
MOE_EPG = 4
MOE_EXPERTS = MOE_GROUPS * MOE_EPG
MOE_HIDDEN = D_MODEL // 4
MOE_ROUTER_ROWS = 32
PLE_DIM = 256

LANES = 128
MXU_N = 256
SEQ_TILE = 256
ATT_TILE = 512
MIB = 1024 * 1024


def _cparams(sem, vmem_mib):
    return pltpu.CompilerParams(dimension_semantics=sem, vmem_limit_bytes=vmem_mib * MIB)


def _split2(x):
    hi = x.astype(BF16)
    lo = (x - hi.astype(F32)).astype(BF16)
    return hi, lo


def _split3(x):
    hi = x.astype(BF16)
    r = x - hi.astype(F32)
    mid = r.astype(BF16)
    lo = (r - mid.astype(F32)).astype(BF16)
    return hi, mid, lo


def _dot(a, b):
    return jnp.dot(a, b, preferred_element_type=F32)


def _dot_nt(a, b):
    return lax.dot_general(a, b, (((1,), (1,)), ((), ())), preferred_element_type=F32)


def _log_sigmoid(x):
    return jnp.minimum(x, 0.0) - jnp.log1p(jnp.exp(-jnp.abs(x)))


def _softplus(x):
    return jnp.maximum(x, 0.0) + jnp.log1p(jnp.exp(-jnp.abs(x)))


def _rms_rows(xf, g, eps=NORM_EPS):
    ms = jnp.mean(xf * xf, axis=-1, keepdims=True)
    return xf * lax.rsqrt(ms + eps) * g


def _block_index(i, block):
    shift = block.bit_length() - 1
    assert 1 << shift == block
    return lax.shift_right_logical(i, shift)


def _lane_col(x, idx):
    lane = lax.broadcasted_iota(jnp.int32, (1, x.shape[1]), 1)
    return jnp.sum(jnp.where(lane == idx, x, 0.0), axis=1, keepdims=True)


def _in_proj_kernel(x_ref, g_ref, w_ref, gain_ref, *out_refs, widths, n_norm, head_dim):
    h = _rms_rows(x_ref[...], g_ref[...]).astype(BF16)
    r = _block_index(lax.broadcasted_iota(jnp.int32, (MXU_N, MXU_N), 0), head_dim)
    c = _block_index(lax.broadcasted_iota(jnp.int32, (MXU_N, MXU_N), 1), head_dim)
    seg = jnp.where(r == c, 1.0, 0.0).astype(BF16)
    off = 0
    for o_ref, wdt in zip(out_refs, widths):
        for c0 in range(0, wdt, MXU_N):
            n = min(MXU_N, wdt - c0)
            y = _dot(h, w_ref[:, off + c0: off + c0 + n])
            if off + c0 < n_norm:
                hi, lo = _split2(y * y)
                ss = _dot(hi, seg) + _dot(lo, seg)
                y = y * lax.rsqrt(ss * (1.0 / head_dim) + NORM_EPS) * gain_ref[:, off + c0: off + c0 + n]
            o_ref[:, c0:c0 + n] = y.astype(o_ref.dtype)
        off += wdt


def _in_proj(x2d, g, w, gain, segments, n_norm, tm=512):
    t, d = x2d.shape
    n = w.shape[1]
    widths = tuple(wd for wd, _ in segments)
    kern = functools.partial(_in_proj_kernel, widths=widths, n_norm=n_norm, head_dim=ATT_HEAD_DIM)
    return pl.pallas_call(
        kern,
        grid=(t // tm,),
        in_specs=[
            pl.BlockSpec((tm, d), lambda i: (i, 0)),
            pl.BlockSpec((1, d), lambda i: (0, 0)),
            pl.BlockSpec((d, n), lambda i: (0, 0)),
            pl.BlockSpec((1, n_norm), lambda i: (0, 0)),
        ],
        out_specs=[pl.BlockSpec((tm, wd), lambda i: (i, 0)) for wd in widths],
        out_shape=[jax.ShapeDtypeStruct((t, wd), dt) for wd, dt in segments],
        compiler_params=_cparams(("parallel",), 52),
        name="in_proj",
    )(x2d, g, w, gain)


def _tri_cumsum(vals, tri):
    hi, mid, lo = _split3(vals)
    return _dot(tri, hi) + _dot(tri, mid) + _dot(tri, lo)


def _lower_tri(n):
    r = lax.broadcasted_iota(jnp.int32, (n, n), 0)
    c = lax.broadcasted_iota(jnp.int32, (n, n), 1)
    return jnp.where(c <= r, 1.0, 0.0).astype(BF16)


def _prep_even_kernel(raw_ref, bias_ref, g_ref, rm_ref):
    s = raw_ref.shape[0]
    ch = SEQ_TILE
    lane = lax.broadcasted_iota(jnp.int32, (1, LANES), 1)
    is_cum = (lane < ATT_HEADS) | ((lane >= ATT_HEADS + MLSTM_HEADS) & (lane < ATT_HEADS + 2 * MLSTM_HEADS))
    tri = _lower_tri(ch)
    carry = jnp.zeros((1, LANES), F32)
    pad = jnp.zeros((6, ch), F32)
    for c in range(s // ch):
        z = raw_ref[c * ch:(c + 1) * ch, :] + bias_ref[...]
        cs = _tri_cumsum(jnp.where(is_cum, _log_sigmoid(z), 0.0), tri) + carry
        carry = cs[ch - 1:ch, :]
        g = jnp.where(is_cum, cs, z)
        g_ref[c * ch:(c + 1) * ch, :] = g
        gt = g.T
        for hh in range(MLSTM_HEADS):
            i_row = gt[ATT_HEADS + hh:ATT_HEADS + hh + 1]
            f_row = gt[ATT_HEADS + MLSTM_HEADS + hh:ATT_HEADS + MLSTM_HEADS + hh + 1]
            rm_ref[hh, :, c * ch:(c + 1) * ch] = jnp.concatenate([i_row - f_row, f_row, pad], axis=0)


def _prep_even(raw, bias):
    b, s, _ = raw.shape
    return pl.pallas_call(
        _prep_even_kernel,
        grid=(b,),
        in_specs=[pl.BlockSpec((None, s, LANES), lambda i: (i, 0, 0)),
                  pl.BlockSpec((1, LANES), lambda i: (0, 0))],
        out_specs=[pl.BlockSpec((None, s, LANES), lambda i: (i, 0, 0)),
                   pl.BlockSpec((None, MLSTM_HEADS, 8, s), lambda i: (i, 0, 0, 0))],
        out_shape=[jax.ShapeDtypeStruct((b, s, LANES), F32),
                   jax.ShapeDtypeStruct((b, MLSTM_HEADS, 8, s), F32)],
        compiler_params=_cparams(("parallel",), 32),
        name="prep_even",
    )(raw, bias)


def _prep_odd_kernel(raw_ref, bias_ref, alog_ref, g_ref, rs_ref):
    s = raw_ref.shape[0]
    ch = SEQ_TILE
    lane = lax.broadcasted_iota(jnp.int32, (1, LANES), 1)
    is_dt = lane < SSD_HEADS
    tri = _lower_tri(ch)
    a_row = jnp.where(is_dt, -jnp.exp(alog_ref[...]), 0.0)
    pad = jnp.zeros((8 - SSD_HEADS_PER_GROUP, ch), F32)
    for c in range(s // ch):
        dt = jnp.where(is_dt, _softplus(raw_ref[c * ch:(c + 1) * ch, :] + bias_ref[...]), 0.0)
        acum = _tri_cumsum(a_row * dt, tri)
        g = jnp.where(is_dt, dt, pltpu.roll(acum, SSD_HEADS, 1))
        g_ref[c * ch:(c + 1) * ch, :] = g
        gt = g.T
        for gi in range(SSD_GROUPS):
            lo = SSD_HEADS + gi * SSD_HEADS_PER_GROUP
            rs_ref[gi, :, c * ch:(c + 1) * ch] = jnp.concatenate([gt[lo:lo + SSD_HEADS_PER_GROUP], pad], axis=0)


def _prep_odd(raw, bias, alog):
    b, s, _ = raw.shape
    return pl.pallas_call(
        _prep_odd_kernel,
        grid=(b,),
        in_specs=[pl.BlockSpec((None, s, LANES), lambda i: (i, 0, 0)),
                  pl.BlockSpec((1, LANES), lambda i: (0, 0)),
                  pl.BlockSpec((1, LANES), lambda i: (0, 0))],
        out_specs=[pl.BlockSpec((None, s, LANES), lambda i: (i, 0, 0)),
                   pl.BlockSpec((None, SSD_GROUPS, 8, s), lambda i: (i, 0, 0, 0))],
        out_shape=[jax.ShapeDtypeStruct((b, s, LANES), F32),
                   jax.ShapeDtypeStruct((b, SSD_GROUPS, 8, s), F32)],
        compiler_params=_cparams(("parallel",), 32),
        name="prep_odd",
    )(raw, bias, alog)


def _two_pass_softmax_pv(n_past, past_span, past_logits, own_span, own_logits, v_rows, s_scr, tq):
    heads = range(2)
    neg = jnp.full((1, tq), NEG_INF, F32)

    def store_max(span, s_t, hh, m):
        s_scr[hh, pl.ds(*span), :] = s_t
        return jnp.maximum(m, jnp.max(s_t, axis=0, keepdims=True))

    def accumulate(span, hh, m, carry):
        l, acc = carry
        p = jnp.exp2(s_scr[hh, pl.ds(*span), :] - m)
        return l + jnp.sum(p, axis=0, keepdims=True), acc + _dot(v_rows(*span, hh), p.astype(BF16))

    ms = lax.fori_loop(
        0, n_past, lambda j, ms: tuple(store_max(past_span(j), past_logits(j, hh), hh, ms[hh]) for hh in heads),
        (neg, neg))
    ms = [store_max(own_span, own_logits(hh), hh, ms[hh]) for hh in heads]
    zero = (jnp.zeros((1, tq), F32), jnp.zeros((ATT_HEAD_DIM, tq), F32))
    carry = tuple(accumulate(own_span, hh, ms[hh], zero) for hh in heads)
    carry = lax.fori_loop(
        0, n_past, lambda j, c: tuple(accumulate(past_span(j), hh, ms[hh], c[hh]) for hh in heads), carry)
    return jnp.concatenate([acc / l for l, acc in carry], axis=0).T


def _stage_v(v_ref, vt):
    for c in range(v_ref.shape[0] // SEQ_TILE):
        vt[:, c * SEQ_TILE:(c + 1) * SEQ_TILE] = v_ref[c * SEQ_TILE:(c + 1) * SEQ_TILE, :].astype(F32).T.astype(BF16)


def _head_queries(q2):
    lane = lax.broadcasted_iota(jnp.int32, (1, LANES), 1)
    return [jnp.where(lane < ATT_HEAD_DIM, q2, 0.0), jnp.where(lane >= ATT_HEAD_DIM, q2, 0.0)]


def _fox_kernel(q_ref, k_ref, v_ref, fo_ref, g_ref, o_ref, vt, cfb, s_scr):
    hp = pl.program_id(1)
    qi = pl.program_id(2)
    tq = q_ref.shape[0]
    s_len = k_ref.shape[0]
    hd = ATT_HEAD_DIM

    @pl.when(qi == 0)
    def _():
        _stage_v(v_ref, vt)
        g = g_ref[...]
        for hh in range(2):
            cfb[hh] = jnp.broadcast_to(_lane_col(g, 2 * hp + hh) * LOG2E, (s_len, LANES))

    qh = _head_queries(q_ref[...])

    def logits(j, hh):
        st = pl.multiple_of(j * tq, tq)
        s_t = _dot_nt(k_ref[pl.ds(st, tq), :], qh[hh])
        return s_t - jnp.concatenate([cfb[hh, pl.ds(st, tq), :]] * (tq // LANES), axis=1)

    def span(j):
        return pl.multiple_of(j * tq, tq), tq

    krow = lax.broadcasted_iota(jnp.int32, (tq, tq), 0)
    qcol = lax.broadcasted_iota(jnp.int32, (tq, tq), 1)
    out = _two_pass_softmax_pv(
        qi, span, logits, span(qi), lambda hh: jnp.where(krow <= qcol, logits(qi, hh), NEG_INF),
        lambda st, size, hh: vt[hh * hd:(hh + 1) * hd, pl.ds(st, size)], s_scr, tq)
    o_ref[...] = out * jax.nn.sigmoid(fo_ref[...])


def _fox_attention(qkv, ogate, gcol, tq=ATT_TILE):
    b, s, _ = qkv.shape
    np_ = ATT_PAIRS
    return pl.pallas_call(
        _fox_kernel,
        grid=(b, np_, s // tq),
        in_specs=[
            pl.BlockSpec((None, tq, LANES), lambda bi, hp, qi: (bi, qi, hp)),
            pl.BlockSpec((None, s, LANES), lambda bi, hp, qi: (bi, 0, np_ + hp)),
            pl.BlockSpec((None, s, LANES), lambda bi, hp, qi: (bi, 0, 2 * np_ + hp)),
            pl.BlockSpec((None, tq, LANES), lambda bi, hp, qi: (bi, qi, hp)),
            pl.BlockSpec((None, s, LANES), lambda bi, hp, qi: (bi, 0, 0)),
        ],
        out_specs=pl.BlockSpec((None, tq, LANES), lambda bi, hp, qi: (bi, qi, hp)),
        out_shape=jax.ShapeDtypeStruct((b, s, HALF), F32),
        scratch_shapes=[pltpu.VMEM((LANES, s), BF16), pltpu.VMEM((2, s, LANES), F32),
                        pltpu.VMEM((2, s, tq), F32)],
        compiler_params=_cparams(("parallel", "parallel", "arbitrary"), 40),
        name="fox_attention",
    )(qkv, qkv, qkv, ogate, gcol)


def _moba_kernel(q_ref, k_ref, v_ref, o_ref, vt, kmh, kml, bias, s_scr):
    qi = pl.program_id(2)
    tq = q_ref.shape[0]
    s_len = k_ref.shape[0]
    blk = MOBA_BLOCK
    nb = s_len // blk
    nbp = kmh.shape[0]
    hd = ATT_HEAD_DIM
    per_tile = tq // blk

    @pl.when(qi == 0)
    def _():
        _stage_v(v_ref, vt)
        means = [jnp.mean(k_ref[n * blk:(n + 1) * blk, :].astype(F32), axis=0, keepdims=True) for n in range(nb)]
        km = jnp.concatenate(means + [jnp.zeros((nbp - nb, LANES), F32)], axis=0)
        kmh[...], kml[...] = _split2(km)

    qcol1 = lax.broadcasted_iota(jnp.int32, (1, tq), 1)
    q_blk = per_tile * qi + _block_index(qcol1, blk)
    nrow = lax.broadcasted_iota(jnp.int32, (nbp, tq), 0)
    valid = nrow < q_blk
    qh = _head_queries(q_ref[...])
    for hh in range(2):
        gate = _dot_nt(kmh[...], qh[hh]) + _dot_nt(kml[...], qh[hh])
        gate = jnp.where(valid, gate, NEG_INF)
        rank = jnp.zeros((nbp, tq), F32)
        for m in range(nb):
            gm = gate[m:m + 1, :]
            rank = rank + jnp.where((gm > gate) | ((gm == gate) & (nrow > m)), 1.0, 0.0)
        selb = jnp.where((rank < float(MOBA_TOPK)) & valid, 0.0, NEG_INF)
        for n in range(nb):
            bias[hh, n] = jnp.broadcast_to(selb[n:n + 1, :], (8, tq))

    def v_rows(st, size, hh):
        return vt[hh * hd:(hh + 1) * hd, pl.ds(st, size)]

    st = pl.multiple_of(qi * tq, tq)
    krow = lax.broadcasted_iota(jnp.int32, (tq, tq), 0)
    qcol = lax.broadcasted_iota(jnp.int32, (tq, tq), 1)
    same_blk = _block_index(krow, blk) == _block_index(qcol, blk)

    def own_logits(hh):
        s_t = _dot_nt(k_ref[pl.ds(st, tq), :], qh[hh])
        sel_rows = jnp.concatenate(
            [jnp.broadcast_to(bias[hh, per_tile * qi + n][0:1, :], (blk, tq)) for n in range(per_tile)], axis=0)
        return s_t + jnp.where(krow <= qcol, jnp.where(same_blk, 0.0, sel_rows), NEG_INF)

    def past_logits(n, hh):
        return _dot_nt(k_ref[pl.ds(pl.multiple_of(n * blk, blk), blk), :], qh[hh]) + bias[hh, n][0:1, :]

    o_ref[...] = _two_pass_softmax_pv(
        per_tile * qi, lambda n: (pl.multiple_of(n * blk, blk), blk), past_logits, (st, tq), own_logits,
        v_rows, s_scr, tq)


def _moba_attention(qkv, tq=ATT_TILE):
    b, s, _ = qkv.shape
    np_ = ATT_PAIRS
    nb = s // MOBA_BLOCK
    nbp = -(-nb // 16) * 16
    return pl.pallas_call(
        _moba_kernel,
        grid=(b, np_, s // tq),
        in_specs=[
            pl.BlockSpec((None, tq, LANES), lambda bi, hp, qi: (bi, qi, hp)),
            pl.BlockSpec((None, s, LANES), lambda bi, hp, qi: (bi, 0, np_ + hp)),
            pl.BlockSpec((None, s, LANES), lambda bi, hp, qi: (bi, 0, 2 * np_ + hp)),
        ],
        out_specs=pl.BlockSpec((None, tq, LANES), lambda bi, hp, qi: (bi, qi, hp)),
        out_shape=jax.ShapeDtypeStruct((b, s, HALF), F32),
        scratch_shapes=[pltpu.VMEM((LANES, s), BF16),
                        pltpu.VMEM((nbp, LANES), BF16), pltpu.VMEM((nbp, LANES), BF16),
                        pltpu.VMEM((2, nb, 8, tq), F32), pltpu.VMEM((2, s, tq), F32)],
        compiler_params=_cparams(("parallel", "parallel", "arbitrary"), 40),
        name="moba_attention",
    )(qkv, qkv, qkv)


def _conv_silu(x_ref, w_ref, b_ref, o_ref, scale):
    s_len, ch = x_ref.shape
    lc = SEQ_TILE
    w = w_ref[...]
    b = b_ref[...]
    row = lax.broadcasted_iota(jnp.int32, (lc, ch), 0)

    def body(c, _):
        st = pl.multiple_of(c * lc, lc)
        pst = pl.multiple_of(jnp.maximum(c - 1, 0) * lc, lc)
        cur = x_ref[pl.ds(st, lc), :]
        prev = jnp.where(c > 0, x_ref[pl.ds(pst, lc), :], 0.0)
        y = b
        for j in range(CONV_WIDTH - 1):
            sh = CONV_WIDTH - 1 - j
            shifted = jnp.where(row < sh, pltpu.roll(prev, sh, 0), pltpu.roll(cur, sh, 0))
            y = y + shifted * w[j:j + 1]
        y = y + cur * w[CONV_WIDTH - 1:CONV_WIDTH]
        y = y * jax.nn.sigmoid(y)
        o_ref[pl.ds(st, lc), :] = y * scale if scale != 1.0 else y
        return 0

    lax.fori_loop(0, s_len // lc, body, 0)


def _mlstm_kernel(q_ref, k_ref, v_ref, og_ref, cwq_ref, cwk_ref, cbq_ref, cbk_ref, g_ref, r_ref, ng_ref,
                  o_ref, qc, kc, cst):
    hh = pl.program_id(1)
    s_len, d = q_ref.shape
    ln = SEQ_TILE
    _conv_silu(q_ref, cwq_ref, cbq_ref, qc, 1.0)
    _conv_silu(k_ref, cwk_ref, cbk_ref, kc, d ** -0.5)
    cst[...] = jnp.zeros_like(cst)
    row = lax.broadcasted_iota(jnp.int32, (ln, ln), 0)
    col = lax.broadcasted_iota(jnp.int32, (ln, ln), 1)
    tri = col <= row
    lane = lax.broadcasted_iota(jnp.int32, (1, LANES), 1)
    ones_blk = jnp.broadcast_to(jnp.where(lane == 0, 1.0, 0.0), (ln, LANES)).astype(BF16)
    f_lane = ATT_HEADS + MLSTM_HEADS + hh

    def step(c, m_prev):
        st = pl.multiple_of(c * ln, ln)
        q = qc[pl.ds(st, ln), :]
        k = kc[pl.ds(st, ln), :]
        v = v_ref[pl.ds(st, ln), :]
        f_col = _lane_col(g_ref[pl.ds(st, ln), :], f_lane)
        a_row = r_ref[0:1, pl.ds(st, ln)]
        a_mat = jnp.where(tri, a_row, NEG_INF)
        m_col = jnp.maximum(m_prev, jnp.max(a_mat, axis=1, keepdims=True))
        qb = q.astype(BF16)
        w_qk = _dot_nt(qb, k.astype(BF16)) * jnp.exp(a_mat - m_col)
        v_aug = jnp.concatenate([v, ones_blk], axis=1)
        c_prev = cst[...]
        inter = jnp.exp(m_prev - m_col)
        q_c = _dot(qb, c_prev.astype(BF16))
        num = _dot(w_qk.astype(BF16), v_aug[:, :d]) + inter * q_c[:, :d]
        den = jnp.sum(w_qk, axis=1, keepdims=True) + inter * q_c[:, d:d + 1]
        h = num / jnp.maximum(jnp.abs(den), jnp.exp(-f_col - m_col))
        m_end = jnp.maximum(m_prev, jnp.max(a_row, axis=1, keepdims=True))
        k_w = (k.T * jnp.exp(a_row - m_end)).astype(BF16)
        cst[...] = jnp.exp(m_prev - m_end) * c_prev + _dot(k_w, v_aug)
        hn = _rms_rows(h, ng_ref[...])
        o_ref[pl.ds(st, ln), :] = hn * jax.nn.sigmoid(og_ref[pl.ds(st, ln), :])
        return m_end

    lax.fori_loop(0, s_len // ln, step, jnp.zeros((1, 1), F32))


def _mlstm(att, rest, conv_w, conv_b, gcol, grow, norm_g):
    b, s, _ = att.shape
    nh = MLSTM_HEADS
    d = MLSTM_HEAD_DIM
    big = lambda off: pl.BlockSpec((None, s, d), lambda bi, h: (bi, 0, off + h))
    return pl.pallas_call(
        _mlstm_kernel,
        grid=(b, nh),
        in_specs=[
            big(nh), big(2 * nh), big(3 * nh), big(3 * nh),
            pl.BlockSpec((CONV_WIDTH, d), lambda bi, h: (0, h)),
            pl.BlockSpec((CONV_WIDTH, d), lambda bi, h: (0, nh + h)),
            pl.BlockSpec((1, d), lambda bi, h: (0, h)),
            pl.BlockSpec((1, d), lambda bi, h: (0, nh + h)),
            pl.BlockSpec((None, s, LANES), lambda bi, h: (bi, 0, 0)),
            pl.BlockSpec((None, None, 8, s), lambda bi, h: (bi, h, 0, 0)),
            pl.BlockSpec((1, d), lambda bi, h: (0, h)),
        ],
        out_specs=pl.BlockSpec((None, s, d), lambda bi, h: (bi, 0, h)),
        out_shape=jax.ShapeDtypeStruct((b, s, HALF), F32),
        scratch_shapes=[pltpu.VMEM((s, d), F32), pltpu.VMEM((s, d), F32), pltpu.VMEM((d, 2 * d), F32)],
        compiler_params=_cparams(("parallel", "parallel"), 40),
        name="mlstm",
    )(rest, rest, att, rest, conv_w, conv_w, conv_b, conv_b, gcol, grow, norm_g)


def _ssd_kernel(z_ref, x_ref, b_ref, c_ref, cwx_ref, cwb_ref, cwc_ref, cbx_ref, cbb_ref, cbc_ref,
                g_ref, e_ref, r_ref, dsk_ref, ng_ref, o_ref, xc, bc, cc, sst):
    s_len, gw = x_ref.shape
    ln = SEQ_TILE
    _conv_silu(x_ref, cwx_ref, cbx_ref, xc, 1.0)
    _conv_silu(b_ref, cwb_ref, cbb_ref, bc, 1.0)
    _conv_silu(c_ref, cwc_ref, cbc_ref, cc, 1.0)
    sst[...] = jnp.zeros_like(sst)
    row = lax.broadcasted_iota(jnp.int32, (ln, ln), 0)
    col = lax.broadcasted_iota(jnp.int32, (ln, ln), 1)
    tri = col <= row
    head_of_lane = _block_index(lax.broadcasted_iota(jnp.int32, (1, gw), 1), SSD_HEAD_DIM)

    def step(c, _):
        st = pl.multiple_of(c * ln, ln)
        x = xc[pl.ds(st, ln), :]
        bm = bc[pl.ds(st, ln), :]
        cm = cc[pl.ds(st, ln), :]
        z = z_ref[pl.ds(st, ln), :]
        parts = _split3(g_ref[pl.ds(st, ln), :])
        dt_e = sum(_dot(p, e_ref[0]) for p in parts)
        ac_e = sum(_dot(p, e_ref[1]) for p in parts)
        xdt = x * dt_e
        xdt_b = xdt.astype(BF16)
        cmb = cm.astype(BF16)
        cb = _dot_nt(cmb, bm.astype(BF16))
        ac_rows = r_ref[0:SSD_HEADS_PER_GROUP, pl.ds(st, ln)]
        y = jnp.zeros((ln, gw), F32)
        for hh in range(SSD_HEADS_PER_GROUP):
            ac_col = ac_e[:, hh * SSD_HEAD_DIM:hh * SSD_HEAD_DIM + 1]
            l_mat = jnp.exp(jnp.where(tri, ac_col - ac_rows[hh:hh + 1], NEG_INF))
            y = jnp.where(head_of_lane == hh, _dot((cb * l_mat).astype(BF16), xdt_b), y)
        ac_end = ac_e[ln - 1:ln, :]
        prev = sst[...]
        y = y + _dot(cmb, prev.astype(BF16)) * jnp.exp(ac_e)
        states = _dot(bm.T.astype(BF16), (xdt * jnp.exp(ac_end - ac_e)).astype(BF16))
        sst[...] = prev * jnp.exp(ac_end) + states
        y = y + dsk_ref[...] * x
        y = y * (z * jax.nn.sigmoid(z))
        o_ref[pl.ds(st, ln), :] = _rms_rows(y, ng_ref[...])
        return 0

    lax.fori_loop(0, s_len // ln, step, 0)


def _ssd(zx, bcin, conv_w, conv_b, gcol, expand, grow, d_row, norm_g):
    b, s, _ = zx.shape
    gw = SSD_GROUP_W
    ns = SSD_STATE
    ng = SSD_GROUPS
    xoff = HALF // gw
    boff = HALF // ns
    coff = (HALF + ng * ns) // ns
    return pl.pallas_call(
        _ssd_kernel,
        grid=(b, ng),
        in_specs=[
            pl.BlockSpec((None, s, gw), lambda bi, g: (bi, 0, g)),
            pl.BlockSpec((None, s, gw), lambda bi, g: (bi, 0, xoff + g)),
            pl.BlockSpec((None, s, ns), lambda bi, g: (bi, 0, g)),
            pl.BlockSpec((None, s, ns), lambda bi, g: (bi, 0, ng + g)),
            pl.BlockSpec((CONV_WIDTH, gw), lambda bi, g: (0, g)),
            pl.BlockSpec((CONV_WIDTH, ns), lambda bi, g: (0, boff + g)),
            pl.BlockSpec((CONV_WIDTH, ns), lambda bi, g: (0, coff + g)),
            pl.BlockSpec((1, gw), lambda bi, g: (0, g)),
            pl.BlockSpec((1, ns), lambda bi, g: (0, boff + g)),
            pl.BlockSpec((1, ns), lambda bi, g: (0, coff + g)),
            pl.BlockSpec((None, s, LANES), lambda bi, g: (bi, 0, 0)),
            pl.BlockSpec((None, 2, LANES, gw), lambda bi, g: (g, 0, 0, 0)),
            pl.BlockSpec((None, None, 8, s), lambda bi, g: (bi, g, 0, 0)),
            pl.BlockSpec((1, gw), lambda bi, g: (0, g)),
            pl.BlockSpec((1, gw), lambda bi, g: (0, g)),
        ],
        out_specs=pl.BlockSpec((None, s, gw), lambda bi, g: (bi, 0, g)),
        out_shape=jax.ShapeDtypeStruct((b, s, HALF), F32),
        scratch_shapes=[pltpu.VMEM((s, gw), F32), pltpu.VMEM((s, ns), F32), pltpu.VMEM((s, ns), F32),
                        pltpu.VMEM((ns, gw), F32)],
        compiler_params=_cparams(("parallel", "parallel"), 48),
        name="ssd",
    )(zx, zx, bcin, bcin, conv_w, conv_w, conv_w, conv_b, conv_b, conv_b, gcol, expand, grow, d_row, norm_g)


def _ssd_expand_matrices():
    e = np.zeros((SSD_GROUPS, 2, LANES, SSD_GROUP_W), np.float32)
    for g in range(SSD_GROUPS):
        for h in range(SSD_HEADS_PER_GROUP):
            head = g * SSD_HEADS_PER_GROUP + h
            e[g, 0, head, h * SSD_HEAD_DIM:(h + 1) * SSD_HEAD_DIM] = 1.0
            e[g, 1, SSD_HEADS + head, h * SSD_HEAD_DIM:(h + 1) * SSD_HEAD_DIM] = 1.0
    return jnp.asarray(e, BF16)


def _out_proj_kernel(a_ref, b_ref, x_ref, w_ref, o_ref):
    half = a_ref.shape[1]
    o_ref[...] = (x_ref[...] + _dot(a_ref[...].astype(BF16), w_ref[:half, :])
                  + _dot(b_ref[...].astype(BF16), w_ref[half:, :]))


def _out_proj(mix_a, mix_b, x2d, w, tm=512):
    t, d = x2d.shape
    half = mix_a.shape[1]
    return pl.pallas_call(
        _out_proj_kernel,
        grid=(t // tm,),
        in_specs=[pl.BlockSpec((tm, half), lambda i: (i, 0)),
                  pl.BlockSpec((tm, half), lambda i: (i, 0)),
                  pl.BlockSpec((tm, d), lambda i: (i, 0)),
                  pl.BlockSpec((2 * half, d), lambda i: (0, 0))],
        out_specs=pl.BlockSpec((tm, d), lambda i: (i, 0)),
        out_shape=jax.ShapeDtypeStruct((t, d), F32),
        compiler_params=_cparams(("parallel",), 32),
        name="out_proj",
    )(mix_a, mix_b, x2d, w)


def _moe_combine(logits_t):
    gl = [logits_t[g:g + 1, :] for g in range(MOE_GROUPS)]
    g_max = functools.reduce(jnp.maximum, gl)
    g_den = sum(jnp.exp(x - g_max) for x in gl)
    g_w = 1.0 / g_den
    taken = jnp.zeros_like(g_max) > 1.0
    is_g = []
    for g in range(MOE_GROUPS):
        hit = (gl[g] == g_max) & jnp.logical_not(taken)
        is_g.append(hit)
        taken = taken | hit
    e_in = []
    for j in range(MOE_EPG):
        v = jnp.zeros_like(g_max)
        for g in range(MOE_GROUPS):
            row = MOE_GROUPS + g * MOE_EPG + j
            v = jnp.where(is_g[g], logits_t[row:row + 1, :], v)
        e_in.append(v)
    rank = []
    for j in range(MOE_EPG):
        r = jnp.zeros_like(g_max)
        for m in range(MOE_EPG):
            if m == j:
                continue
            ahead = (e_in[m] > e_in[j]) | ((e_in[m] == e_in[j]) & (m < j))
            r = r + jnp.where(ahead, 1.0, 0.0)
        rank.append(r)
    v0 = sum(jnp.where(rank[j] == 0.0, e_in[j], 0.0) for j in range(MOE_EPG))
    v1 = sum(jnp.where(rank[j] == 1.0, e_in[j], 0.0) for j in range(MOE_EPG))
    e1 = jnp.exp(v1 - v0)
    w0 = 1.0 / (1.0 + e1)
    w1 = e1 / (1.0 + e1)
    comb = []
    for g in range(MOE_GROUPS):
        for j in range(MOE_EPG):
            wj = jnp.where(rank[j] == 0.0, w0, jnp.where(rank[j] == 1.0, w1, 0.0))
            comb.append(jnp.where(is_g[g], g_w * wj, 0.0))
    return comb


def _moe_kernel(x_ref, g_ref, wrh_ref, wrl_ref, rb_ref, wgu_ref, wd_ref, o_ref, h_sc, cw_sc, acc_sc):
    e = pl.program_id(1)
    tm = x_ref.shape[0]

    @pl.when(e == 0)
    def _():
        h = _rms_rows(x_ref[...], g_ref[...])
        h_hi, h_lo = _split2(h)
        logits_t = (_dot_nt(wrh_ref[...], h_hi) + _dot_nt(wrh_ref[...], h_lo) + _dot_nt(wrl_ref[...], h_hi)
                    + rb_ref[...])
        comb_t = jnp.concatenate(_moe_combine(logits_t) + [jnp.zeros((LANES - MOE_EXPERTS, tm), F32)], axis=0)
        h_sc[...] = h_hi
        cw_sc[...] = comb_t.T
        acc_sc[...] = jnp.zeros_like(acc_sc)

    ab = _dot(h_sc[...], wgu_ref[...])
    a = ab[:, :MOE_HIDDEN]
    b = ab[:, MOE_HIDDEN:]
    hid = (a * jax.nn.sigmoid(a)) * b * _lane_col(cw_sc[...], e)
    acc_sc[...] += _dot(hid.astype(BF16), wd_ref[...])

    @pl.when(e == pl.num_programs(1) - 1)
    def _():
        o_ref[...] = x_ref[...] + acc_sc[...]


def _moe(x2d, g, wr_hi, wr_lo, rb, wgu, wd, tm=1024):
    t, d = x2d.shape
    ne = wgu.shape[0]
    return pl.pallas_call(
        _moe_kernel,
        grid=(t // tm, ne),
        in_specs=[pl.BlockSpec((tm, d), lambda i, e: (i, 0)),
                  pl.BlockSpec((1, d), lambda i, e: (0, 0)),
                  pl.BlockSpec((MOE_ROUTER_ROWS, d), lambda i, e: (0, 0)),
                  pl.BlockSpec((MOE_ROUTER_ROWS, d), lambda i, e: (0, 0)),
                  pl.BlockSpec((MOE_ROUTER_ROWS, 1), lambda i, e: (0, 0)),
                  pl.BlockSpec((None, d, 2 * MOE_HIDDEN), lambda i, e: (e, 0, 0)),
                  pl.BlockSpec((None, MOE_HIDDEN, d), lambda i, e: (e, 0, 0))],
        out_specs=pl.BlockSpec((tm, d), lambda i, e: (i, 0)),
        out_shape=jax.ShapeDtypeStruct((t, d), F32),
        scratch_shapes=[pltpu.VMEM((tm, d), BF16), pltpu.VMEM((tm, LANES), F32), pltpu.VMEM((tm, d), F32)],
        compiler_params=_cparams(("parallel", "arbitrary"), 48),
        name="moe",
    )(x2d, g, wr_hi, wr_lo, rb, wgu, wd)


def _ple_kernel(x_ref, p_ref, wp_ref, wg_ref, g1_ref, g2_ref, o_ref):
    x = x_ref[...]
    e = _dot(p_ref[...].astype(BF16), wp_ref[...])
    gate = jax.nn.sigmoid(_dot(_rms_rows(x, g1_ref[...]).astype(BF16), wg_ref[...]))
    o_ref[...] = x + _rms_rows(e * gate, g2_ref[...])


def _ple(x2d, p2d, wp, wg, g1, g2, tm=512):
    t, d = x2d.shape
    kp = p2d.shape[1]
    return pl.pallas_call(
        _ple_kernel,
        grid=(t // tm,),
        in_specs=[pl.BlockSpec((tm, d), lambda i: (i, 0)),
                  pl.BlockSpec((tm, kp), lambda i: (i, 0)),
                  pl.BlockSpec((kp, d), lambda i: (0, 0)),
                  pl.BlockSpec((d, d), lambda i: (0, 0)),
                  pl.BlockSpec((1, d), lambda i: (0, 0)),
                  pl.BlockSpec((1, d), lambda i: (0, 0))],
        out_specs=pl.BlockSpec((tm, d), lambda i: (i, 0)),
        out_shape=jax.ShapeDtypeStruct((t, d), F32),
        compiler_params=_cparams(("parallel",), 32),
        name="ple",
    )(x2d, p2d, wp, wg, g1, g2)


def _pad_lanes(cols, width=LANES):
    return jnp.pad(cols, ((0, 0), (0, width - cols.shape[-1])))


def _row(v):
    return v.reshape(1, -1).astype(F32)


def _even_mix(x2d, bsz, seq, norm_g, w_in, fox_b_f, fox_qn_g, fox_kn_g, conv_w, conv_b, b_i, b_f, mnorm_g):
    hw = HALF
    o = np.cumsum([0, hw, hw, hw, ATT_HEADS, hw, hw, hw, hw, MLSTM_HEADS, MLSTM_HEADS, hw])
    seg = lambda i: w_in[:, o[i]:o[i + 1]]
    fq, fk, fv, ff, fo, mq, mk, mv, mi, mf, mo = [seg(i) for i in range(11)]
    w = jnp.concatenate([fq, fk, fv, mv, fo, mq, mk, mo, _pad_lanes(jnp.concatenate([ff, mi, mf], axis=1))],
                        axis=1).astype(BF16)
    gain = jnp.concatenate([jnp.tile(fox_qn_g, ATT_HEADS) * ATT_Q_SCALE, jnp.tile(fox_kn_g, ATT_HEADS)])
    att, rest, gates = _in_proj(x2d, _row(norm_g), w, _row(gain), ((4 * hw, BF16), (4 * hw, F32), (LANES, F32)),
                                2 * hw)
    att = att.reshape(bsz, seq, 4 * hw)
    rest = rest.reshape(bsz, seq, 4 * hw)
    bias = _pad_lanes(_row(jnp.concatenate([fox_b_f, b_i, b_f])))
    gcol, rm = _prep_even(gates.reshape(bsz, seq, LANES), bias)
    out_a = _fox_attention(att, rest, gcol)
    out_b = _mlstm(att, rest, conv_w, _row(conv_b), gcol, rm, _row(mnorm_g))
    return out_a.reshape(-1, hw), out_b.reshape(-1, hw)


def _odd_mix(x2d, bsz, seq, norm_g, w_in, moba_qn_g, moba_kn_g, conv_w, conv_b, dt_bias, a_log, d_skip, snorm_g):
    hw = HALF
    nbc = SSD_GROUPS * SSD_STATE
    o = np.cumsum([0, hw, hw, hw, hw, hw, nbc, nbc, SSD_HEADS])
    w = jnp.concatenate([w_in[:, :o[7]], _pad_lanes(w_in[:, o[7]:o[8]])], axis=1).astype(BF16)
    gain = jnp.concatenate([jnp.tile(moba_qn_g, ATT_HEADS) * ATT_Q_SCALE, jnp.tile(moba_kn_g, ATT_HEADS)])
    moba_u, zx, bcin, dts = _in_proj(x2d, _row(norm_g), w, _row(gain),
                                     ((3 * hw, BF16), (2 * hw, F32), (2 * nbc, F32), (LANES, F32)), 2 * hw)
    gcol, rs = _prep_odd(dts.reshape(bsz, seq, LANES), _pad_lanes(_row(dt_bias)), _pad_lanes(_row(a_log)))
    out_c = _moba_attention(moba_u.reshape(bsz, seq, 3 * hw))
    out_d = _ssd(zx.reshape(bsz, seq, 2 * hw), bcin.reshape(bsz, seq, 2 * nbc), conv_w, _row(conv_b), gcol,
                 _ssd_expand_matrices(), rs, _row(jnp.repeat(d_skip, SSD_HEAD_DIM)), _row(snorm_g))
    return out_c.reshape(-1, hw), out_d.reshape(-1, hw)


def _moe_layer(x2d, norm_g, w_group, b_group, w_router, b_router, w_gate, w_up, w_down):
    d = x2d.shape[1]
    wr = _pad_lanes(jnp.concatenate([w_group, w_router], axis=1), MOE_ROUTER_ROWS).T
    wr_hi = wr.astype(BF16)
    wr_lo = (wr - wr_hi.astype(F32)).astype(BF16)
    rb = _pad_lanes(_row(jnp.concatenate([b_group, b_router])), MOE_ROUTER_ROWS).T
    wgu = jnp.concatenate([w_gate.reshape(MOE_EXPERTS, d, MOE_HIDDEN), w_up.reshape(MOE_EXPERTS, d, MOE_HIDDEN)],
                          axis=-1).astype(BF16)
    wd = w_down.reshape(MOE_EXPERTS, MOE_HIDDEN, d).astype(BF16)
    return _moe(x2d, _row(norm_g), wr_hi, wr_lo, rb, wgu, wd)


def kernel(x, p, norm1_g, norm2_g, ev_w_in, ev_fox_b_f, ev_fox_qn_g, ev_fox_kn_g, ev_mlstm_conv_w, ev_mlstm_conv_b, ev_mlstm_b_i, ev_mlstm_b_f, ev_mlstm_norm_g, ev_w_out, od_w_in, od_moba_qn_g, od_moba_kn_g, od_ssd_conv_w, od_ssd_conv_b, od_ssd_dt_bias, od_ssd_A_log, od_ssd_D, od_ssd_norm_g, od_w_out, moe_w_group, moe_b_group, moe_w_router, moe_b_router, moe_w_gate, moe_w_up, moe_w_down, ple_w_proj, ple_w_gate, ple_gate_norm_g, ple_out_norm_g):
    bsz, seq, d = x.shape
    depth = p.shape[0]
    x2d = x.reshape(bsz * seq, d)
    for i in range(depth):
        j = i // 2
        if i % 2 == 0:
            mix_a, mix_b = _even_mix(x2d, bsz, seq, norm1_g[i], ev_w_in[j], ev_fox_b_f[j], ev_fox_qn_g[j],
                                     ev_fox_kn_g[j], ev_mlstm_conv_w[j], ev_mlstm_conv_b[j], ev_mlstm_b_i[j],
                                     ev_mlstm_b_f[j], ev_mlstm_norm_g[j])
            w_out = ev_w_out[j]
        else:
            mix_a, mix_b = _odd_mix(x2d, bsz, seq, norm1_g[i], od_w_in[j], od_moba_qn_g[j], od_moba_kn_g[j],
                                    od_ssd_conv_w[j], od_ssd_conv_b[j], od_ssd_dt_bias[j], od_ssd_A_log[j],
                                    od_ssd_D[j], od_ssd_norm_g[j])
            w_out = od_w_out[j]
        x2d = _out_proj(mix_a, mix_b, x2d, w_out.astype(BF16))
        x2d = _moe_layer(x2d, norm2_g[i], moe_w_group[i], moe_b_group[i], moe_w_router[i], moe_b_router[i],
                         moe_w_gate[i], moe_w_up[i], moe_w_down[i])
        x2d = _ple(x2d, p[i].reshape(bsz * seq, -1), ple_w_proj[i].astype(BF16), ple_w_gate[i].astype(BF16),
                   _row(ple_gate_norm_g[i]), _row(ple_out_norm_g[i]))
    return x2d.reshape(bsz, seq, d)
```

```python
import functools

import jax
import jax.numpy as jnp
import numpy as np
from jax import lax
from jax.experimental import pallas as pl
from jax.experimental.pallas import tpu as pltpu

F32 = jnp.float32
BF16 = jnp.bfloat16
NEG_INF = float("-inf")
LOG2E = 1.4426950408889634

NORM_EPS = 1e-6
D_MODEL = 1024
HALF = D_MODEL // 2
ATT_HEAD_DIM = 64
ATT_HEADS = HALF // ATT_HEAD_DIM
ATT_PAIRS = ATT_HEADS // 2
ATT_Q_SCALE = ATT_HEAD_DIM ** -0.5 * LOG2E
MLSTM_HEAD_DIM = 128
MLSTM_HEADS = HALF // MLSTM_HEAD_DIM
SSD_HEAD_DIM = 64
SSD_HEADS = HALF // SSD_HEAD_DIM
SSD_GROUPS = 2
SSD_STATE = 128
SSD_GROUP_W = HALF // SSD_GROUPS
SSD_HEADS_PER_GROUP = SSD_HEADS // SSD_GROUPS
CONV_WIDTH = 4
MOBA_BLOCK = 256
MOBA_TOPK = 3
MOE_GROUPS = 4
MOE_EPG = 4
MOE_EXPERTS = MOE_GROUPS * MOE_EPG
MOE_HIDDEN = D_MODEL // 4
MOE_ROUTER_ROWS = 32
MOE_CHUNK = 256
PLE_DIM = 256

LANES = 128
MXU_N = 256
SEQ_TILE = 256
ATT_TILE = 512
MIB = 1024 * 1024


def _cparams(sem, vmem_mib):
    return pltpu.CompilerParams(dimension_semantics=sem, vmem_limit_bytes=vmem_mib * MIB)


def _split2(x):
    hi = x.astype(BF16)
    lo = (x - hi.astype(F32)).astype(BF16)
    return hi, lo


def _split3(x):
    hi = x.astype(BF16)
    r = x - hi.astype(F32)
    mid = r.astype(BF16)
    lo = (r - mid.astype(F32)).astype(BF16)
    return hi, mid, lo


def _dot(a, b):
    return jnp.dot(a, b, preferred_element_type=F32)


def _dot_nt(a, b):
    return lax.dot_general(a, b, (((1,), (1,)), ((), ())), preferred_element_type=F32)


def _log_sigmoid(x):
    return jnp.minimum(x, 0.0) - jnp.log1p(jnp.exp(-jnp.abs(x)))


def _softplus(x):
    return jnp.maximum(x, 0.0) + jnp.log1p(jnp.exp(-jnp.abs(x)))


def _rms_rows(xf, g, eps=NORM_EPS):
    ms = jnp.mean(xf * xf, axis=-1, keepdims=True)
    return xf * lax.rsqrt(ms + eps) * g


def _block_index(i, block):
    shift = block.bit_length() - 1
    assert 1 << shift == block
    return lax.shift_right_logical(i, shift)


def _lane_col(x, idx):
    lane = lax.broadcasted_iota(jnp.int32, (1, x.shape[1]), 1)
    return jnp.sum(jnp.where(lane == idx, x, 0.0), axis=1, keepdims=True)


def _in_proj_kernel(x_ref, g_ref, w_ref, gain_ref, *out_refs, widths, n_norm, head_dim):
    h = _rms_rows(x_ref[...], g_ref[...]).astype(BF16)
    r = _block_index(lax.broadcasted_iota(jnp.int32, (MXU_N, MXU_N), 0), head_dim)
    c = _block_index(lax.broadcasted_iota(jnp.int32, (MXU_N, MXU_N), 1), head_dim)
    seg = jnp.where(r == c, 1.0, 0.0).astype(BF16)
    off = 0
    for o_ref, wdt in zip(out_refs, widths):
        for c0 in range(0, wdt, MXU_N):
            n = min(MXU_N, wdt - c0)
            y = _dot(h, w_ref[:, off + c0: off + c0 + n])
            if off + c0 < n_norm:
                hi, lo = _split2(y * y)
                ss = _dot(hi, seg) + _dot(lo, seg)
                y = y * lax.rsqrt(ss * (1.0 / head_dim) + NORM_EPS) * gain_ref[:, off + c0: off + c0 + n]
            o_ref[:, c0:c0 + n] = y.astype(o_ref.dtype)
        off += wdt


def _in_proj(x2d, g, w, gain, segments, n_norm, tm=512):
    t, d = x2d.shape
    n = w.shape[1]
    widths = tuple(wd for wd, _ in segments)
    kern = functools.partial(_in_proj_kernel, widths=widths, n_norm=n_norm, head_dim=ATT_HEAD_DIM)
    return pl.pallas_call(
        kern,
        grid=(t // tm,),
        in_specs=[
            pl.BlockSpec((tm, d), lambda i: (i, 0)),
            pl.BlockSpec((1, d), lambda i: (0, 0)),
            pl.BlockSpec((d, n), lambda i: (0, 0)),
            pl.BlockSpec((1, n_norm), lambda i: (0, 0)),
        ],
        out_specs=[pl.BlockSpec((tm, wd), lambda i: (i, 0)) for wd in widths],
        out_shape=[jax.ShapeDtypeStruct((t, wd), dt) for wd, dt in segments],
        compiler_params=_cparams(("parallel",), 52),
        name="in_proj",
    )(x2d, g, w, gain)


def _tri_cumsum(vals, tri):
    hi, mid, lo = _split3(vals)
    return _dot(tri, hi) + _dot(tri, mid) + _dot(tri, lo)


def _lower_tri(n):
    r = lax.broadcasted_iota(jnp.int32, (n, n), 0)
    c = lax.broadcasted_iota(jnp.int32, (n, n), 1)
    return jnp.where(c <= r, 1.0, 0.0).astype(BF16)


def _prep_even_kernel(raw_ref, bias_ref, g_ref, rm_ref):
    s = raw_ref.shape[0]
    ch = SEQ_TILE
    lane = lax.broadcasted_iota(jnp.int32, (1, LANES), 1)
    is_cum = (lane < ATT_HEADS) | ((lane >= ATT_HEADS + MLSTM_HEADS) & (lane < ATT_HEADS + 2 * MLSTM_HEADS))
    tri = _lower_tri(ch)
    carry = jnp.zeros((1, LANES), F32)
    pad = jnp.zeros((6, ch), F32)
    for c in range(s // ch):
        z = raw_ref[c * ch:(c + 1) * ch, :] + bias_ref[...]
        cs = _tri_cumsum(jnp.where(is_cum, _log_sigmoid(z), 0.0), tri) + carry
        carry = cs[ch - 1:ch, :]
        g = jnp.where(is_cum, cs, z)
        g_ref[c * ch:(c + 1) * ch, :] = g
        gt = g.T
        for hh in range(MLSTM_HEADS):
            i_row = gt[ATT_HEADS + hh:ATT_HEADS + hh + 1]
            f_row = gt[ATT_HEADS + MLSTM_HEADS + hh:ATT_HEADS + MLSTM_HEADS + hh + 1]
            rm_ref[hh, :, c * ch:(c + 1) * ch] = jnp.concatenate([i_row - f_row, f_row, pad], axis=0)


def _prep_even(raw, bias):
    b, s, _ = raw.shape
    return pl.pallas_call(
        _prep_even_kernel,
        grid=(b,),
        in_specs=[pl.BlockSpec((None, s, LANES), lambda i: (i, 0, 0)),
                  pl.BlockSpec((1, LANES), lambda i: (0, 0))],
        out_specs=[pl.BlockSpec((None, s, LANES), lambda i: (i, 0, 0)),
                   pl.BlockSpec((None, MLSTM_HEADS, 8, s), lambda i: (i, 0, 0, 0))],
        out_shape=[jax.ShapeDtypeStruct((b, s, LANES), F32),
                   jax.ShapeDtypeStruct((b, MLSTM_HEADS, 8, s), F32)],
        compiler_params=_cparams(("parallel",), 32),
        name="prep_even",
    )(raw, bias)


def _prep_odd_kernel(raw_ref, bias_ref, alog_ref, g_ref, rs_ref):
    s = raw_ref.shape[0]
    ch = SEQ_TILE
    lane = lax.broadcasted_iota(jnp.int32, (1, LANES), 1)
    is_dt = lane < SSD_HEADS
    tri = _lower_tri(ch)
    a_row = jnp.where(is_dt, -jnp.exp(alog_ref[...]), 0.0)
    pad = jnp.zeros((8 - SSD_HEADS_PER_GROUP, ch), F32)
    for c in range(s // ch):
        dt = jnp.where(is_dt, _softplus(raw_ref[c * ch:(c + 1) * ch, :] + bias_ref[...]), 0.0)
        acum = _tri_cumsum(a_row * dt, tri)
        g = jnp.where(is_dt, dt, pltpu.roll(acum, SSD_HEADS, 1))
        g_ref[c * ch:(c + 1) * ch, :] = g
        gt = g.T
        for gi in range(SSD_GROUPS):
            lo = SSD_HEADS + gi * SSD_HEADS_PER_GROUP
            rs_ref[gi, :, c * ch:(c + 1) * ch] = jnp.concatenate([gt[lo:lo + SSD_HEADS_PER_GROUP], pad], axis=0)


def _prep_odd(raw, bias, alog):
    b, s, _ = raw.shape
    return pl.pallas_call(
        _prep_odd_kernel,
        grid=(b,),
        in_specs=[pl.BlockSpec((None, s, LANES), lambda i: (i, 0, 0)),
                  pl.BlockSpec((1, LANES), lambda i: (0, 0)),
                  pl.BlockSpec((1, LANES), lambda i: (0, 0))],
        out_specs=[pl.BlockSpec((None, s, LANES), lambda i: (i, 0, 0)),
                   pl.BlockSpec((None, SSD_GROUPS, 8, s), lambda i: (i, 0, 0, 0))],
        out_shape=[jax.ShapeDtypeStruct((b, s, LANES), F32),
                   jax.ShapeDtypeStruct((b, SSD_GROUPS, 8, s), F32)],
        compiler_params=_cparams(("parallel",), 32),
        name="prep_odd",
    )(raw, bias, alog)


def _two_pass_softmax_pv(n_past, past_span, past_logits, own_span, own_logits, v_rows, s_scr, tq):
    heads = range(2)
    neg = jnp.full((1, tq), NEG_INF, F32)

    def store_max(span, s_t, hh, m):
        s_scr[hh, pl.ds(*span), :] = s_t
        return jnp.maximum(m, jnp.max(s_t, axis=0, keepdims=True))

    def accumulate(span, hh, m, carry):
        l, acc = carry
        p = jnp.exp2(s_scr[hh, pl.ds(*span), :] - m)
        return l + jnp.sum(p, axis=0, keepdims=True), acc + _dot(v_rows(*span, hh), p.astype(BF16))

    ms = lax.fori_loop(
        0, n_past, lambda j, ms: tuple(store_max(past_span(j), past_logits(j, hh), hh, ms[hh]) for hh in heads),
        (neg, neg))
    ms = [store_max(own_span, own_logits(hh), hh, ms[hh]) for hh in heads]
    zero = (jnp.zeros((1, tq), F32), jnp.zeros((ATT_HEAD_DIM, tq), F32))
    carry = tuple(accumulate(own_span, hh, ms[hh], zero) for hh in heads)
    carry = lax.fori_loop(
        0, n_past, lambda j, c: tuple(accumulate(past_span(j), hh, ms[hh], c[hh]) for hh in heads), carry)
    return jnp.concatenate([acc / l for l, acc in carry], axis=0).T


def _stage_v(v_ref, vt):
    for c in range(v_ref.shape[0] // SEQ_TILE):
        vt[:, c * SEQ_TILE:(c + 1) * SEQ_TILE] = v_ref[c * SEQ_TILE:(c + 1) * SEQ_TILE, :].astype(F32).T.astype(BF16)


def _head_queries(q2):
    lane = lax.broadcasted_iota(jnp.int32, (1, LANES), 1)
    return [jnp.where(lane < ATT_HEAD_DIM, q2, 0.0), jnp.where(lane >= ATT_HEAD_DIM, q2, 0.0)]


def _fox_kernel(q_ref, k_ref, v_ref, fo_ref, g_ref, o_ref, vt, cfb, s_scr):
    hp = pl.program_id(1)
    qi = pl.program_id(2)
    tq = q_ref.shape[0]
    s_len = k_ref.shape[0]
    hd = ATT_HEAD_DIM

    @pl.when(qi == 0)
    def _():
        _stage_v(v_ref, vt)
        g = g_ref[...]
        for hh in range(2):
            cfb[hh] = jnp.broadcast_to(_lane_col(g, 2 * hp + hh) * LOG2E, (s_len, LANES))

    qh = _head_queries(q_ref[...])

    def logits(j, hh):
        st = pl.multiple_of(j * tq, tq)
        s_t = _dot_nt(k_ref[pl.ds(st, tq), :], qh[hh])
        return s_t - jnp.concatenate([cfb[hh, pl.ds(st, tq), :]] * (tq // LANES), axis=1)

    def span(j):
        return pl.multiple_of(j * tq, tq), tq

    krow = lax.broadcasted_iota(jnp.int32, (tq, tq), 0)
    qcol = lax.broadcasted_iota(jnp.int32, (tq, tq), 1)
    out = _two_pass_softmax_pv(
        qi, span, logits, span(qi), lambda hh: jnp.where(krow <= qcol, logits(qi, hh), NEG_INF),
        lambda st, size, hh: vt[hh * hd:(hh + 1) * hd, pl.ds(st, size)], s_scr, tq)
    o_ref[...] = out * jax.nn.sigmoid(fo_ref[...])


def _fox_attention(qkv, ogate, gcol, tq=ATT_TILE):
    b, s, _ = qkv.shape
    np_ = ATT_PAIRS
    return pl.pallas_call(
        _fox_kernel,
        grid=(b, np_, s // tq),
        in_specs=[
            pl.BlockSpec((None, tq, LANES), lambda bi, hp, qi: (bi, qi, hp)),
            pl.BlockSpec((None, s, LANES), lambda bi, hp, qi: (bi, 0, np_ + hp)),
            pl.BlockSpec((None, s, LANES), lambda bi, hp, qi: (bi, 0, 2 * np_ + hp)),
            pl.BlockSpec((None, tq, LANES), lambda bi, hp, qi: (bi, qi, hp)),
            pl.BlockSpec((None, s, LANES), lambda bi, hp, qi: (bi, 0, 0)),
        ],
        out_specs=pl.BlockSpec((None, tq, LANES), lambda bi, hp, qi: (bi, qi, hp)),
        out_shape=jax.ShapeDtypeStruct((b, s, HALF), F32),
        scratch_shapes=[pltpu.VMEM((LANES, s), BF16), pltpu.VMEM((2, s, LANES), F32),
                        pltpu.VMEM((2, s, tq), F32)],
        compiler_params=_cparams(("parallel", "parallel", "arbitrary"), 40),
        name="fox_attention",
    )(qkv, qkv, qkv, ogate, gcol)


def _moba_kernel(q_ref, k_ref, v_ref, o_ref, vt, kmh, kml, bias, s_scr):
    qi = pl.program_id(2)
    tq = q_ref.shape[0]
    s_len = k_ref.shape[0]
    blk = MOBA_BLOCK
    nb = s_len // blk
    nbp = kmh.shape[0]
    hd = ATT_HEAD_DIM
    per_tile = tq // blk

    @pl.when(qi == 0)
    def _():
        _stage_v(v_ref, vt)
        means = [jnp.mean(k_ref[n * blk:(n + 1) * blk, :].astype(F32), axis=0, keepdims=True) for n in range(nb)]
        km = jnp.concatenate(means + [jnp.zeros((nbp - nb, LANES), F32)], axis=0)
        kmh[...], kml[...] = _split2(km)

    qcol1 = lax.broadcasted_iota(jnp.int32, (1, tq), 1)
    q_blk = per_tile * qi + _block_index(qcol1, blk)
    nrow = lax.broadcasted_iota(jnp.int32, (nbp, tq), 0)
    valid = nrow < q_blk
    qh = _head_queries(q_ref[...])
    for hh in range(2):
        gate = _dot_nt(kmh[...], qh[hh]) + _dot_nt(kml[...], qh[hh])
        gate = jnp.where(valid, gate, NEG_INF)
        rank = jnp.zeros((nbp, tq), F32)
        for m in range(nb):
            gm = gate[m:m + 1, :]
            rank = rank + jnp.where((gm > gate) | ((gm == gate) & (nrow > m)), 1.0, 0.0)
        selb = jnp.where((rank < float(MOBA_TOPK)) & valid, 0.0, NEG_INF)
        for n in range(nb):
            bias[hh, n] = jnp.broadcast_to(selb[n:n + 1, :], (8, tq))

    def v_rows(st, size, hh):
        return vt[hh * hd:(hh + 1) * hd, pl.ds(st, size)]

    st = pl.multiple_of(qi * tq, tq)
    krow = lax.broadcasted_iota(jnp.int32, (tq, tq), 0)
    qcol = lax.broadcasted_iota(jnp.int32, (tq, tq), 1)
    same_blk = _block_index(krow, blk) == _block_index(qcol, blk)

    def sel_rows(j, hh):
        return jnp.concatenate(
            [jnp.broadcast_to(bias[hh, per_tile * j + n][0:1, :], (blk, tq)) for n in range(per_tile)], axis=0)

    def own_logits(hh):
        s_t = _dot_nt(k_ref[pl.ds(st, tq), :], qh[hh])
        return s_t + jnp.where(krow <= qcol, jnp.where(same_blk, 0.0, sel_rows(qi, hh)), NEG_INF)

    def span(j):
        return pl.multiple_of(j * tq, tq), tq

    def past_logits(j, hh):
        return _dot_nt(k_ref[pl.ds(*span(j)), :], qh[hh]) + sel_rows(j, hh)

    o_ref[...] = _two_pass_softmax_pv(qi, span, past_logits, (st, tq), own_logits, v_rows, s_scr, tq)


def _moba_attention(qkv, tq=ATT_TILE):
    b, s, _ = qkv.shape
    np_ = ATT_PAIRS
    nb = s // MOBA_BLOCK
    nbp = -(-nb // 16) * 16
    return pl.pallas_call(
        _moba_kernel,
        grid=(b, np_, s // tq),
        in_specs=[
            pl.BlockSpec((None, tq, LANES), lambda bi, hp, qi: (bi, qi, hp)),
            pl.BlockSpec((None, s, LANES), lambda bi, hp, qi: (bi, 0, np_ + hp)),
            pl.BlockSpec((None, s, LANES), lambda bi, hp, qi: (bi, 0, 2 * np_ + hp)),
        ],
        out_specs=pl.BlockSpec((None, tq, LANES), lambda bi, hp, qi: (bi, qi, hp)),
        out_shape=jax.ShapeDtypeStruct((b, s, HALF), F32),
        scratch_shapes=[pltpu.VMEM((LANES, s), BF16),
                        pltpu.VMEM((nbp, LANES), BF16), pltpu.VMEM((nbp, LANES), BF16),
                        pltpu.VMEM((2, nb, 8, tq), F32), pltpu.VMEM((2, s, tq), F32)],
        compiler_params=_cparams(("parallel", "parallel", "arbitrary"), 40),
        name="moba_attention",
    )(qkv, qkv, qkv)


def _conv_silu(x_ref, w_ref, b_ref, o_ref, scale):
    s_len, ch = x_ref.shape
    lc = SEQ_TILE
    w = w_ref[...]
    b = b_ref[...]
    row = lax.broadcasted_iota(jnp.int32, (lc, ch), 0)

    def body(c, _):
        st = pl.multiple_of(c * lc, lc)
        pst = pl.multiple_of(jnp.maximum(c - 1, 0) * lc, lc)
        cur = x_ref[pl.ds(st, lc), :]
        prev = jnp.where(c > 0, x_ref[pl.ds(pst, lc), :], 0.0)
        y = b
        for j in range(CONV_WIDTH - 1):
            sh = CONV_WIDTH - 1 - j
            shifted = jnp.where(row < sh, pltpu.roll(prev, sh, 0), pltpu.roll(cur, sh, 0))
            y = y + shifted * w[j:j + 1]
        y = y + cur * w[CONV_WIDTH - 1:CONV_WIDTH]
        y = y * jax.nn.sigmoid(y)
        o_ref[pl.ds(st, lc), :] = y * scale if scale != 1.0 else y
        return 0

    lax.fori_loop(0, s_len // lc, body, 0)


def _mlstm_kernel(q_ref, k_ref, v_ref, og_ref, cwq_ref, cwk_ref, cbq_ref, cbk_ref, g_ref, r_ref, ng_ref,
                  o_ref, qc, kc, cst):
    hh = pl.program_id(1)
    s_len, d = q_ref.shape
    ln = SEQ_TILE
    _conv_silu(q_ref, cwq_ref, cbq_ref, qc, 1.0)
    _conv_silu(k_ref, cwk_ref, cbk_ref, kc, d ** -0.5)
    cst[...] = jnp.zeros_like(cst)
    row = lax.broadcasted_iota(jnp.int32, (ln, ln), 0)
    col = lax.broadcasted_iota(jnp.int32, (ln, ln), 1)
    tri = col <= row
    lane = lax.broadcasted_iota(jnp.int32, (1, LANES), 1)
    ones_blk = jnp.broadcast_to(jnp.where(lane == 0, 1.0, 0.0), (ln, LANES)).astype(BF16)
    f_lane = ATT_HEADS + MLSTM_HEADS + hh

    def step(c, m_prev):
        st = pl.multiple_of(c * ln, ln)
        q = qc[pl.ds(st, ln), :]
        k = kc[pl.ds(st, ln), :]
        v = v_ref[pl.ds(st, ln), :]
        f_col = _lane_col(g_ref[pl.ds(st, ln), :], f_lane)
        a_row = r_ref[0:1, pl.ds(st, ln)]
        a_mat = jnp.where(tri, a_row, NEG_INF)
        m_col = jnp.maximum(m_prev, jnp.max(a_mat, axis=1, keepdims=True))
        qb = q.astype(BF16)
        w_qk = _dot_nt(qb, k.astype(BF16)) * jnp.exp(a_mat - m_col)
        v_aug = jnp.concatenate([v, ones_blk], axis=1)
        c_prev = cst[...]
        inter = jnp.exp(m_prev - m_col)
        q_c = _dot(qb, c_prev.astype(BF16))
        num = _dot(w_qk.astype(BF16), v_aug[:, :d]) + inter * q_c[:, :d]
        den = jnp.sum(w_qk, axis=1, keepdims=True) + inter * q_c[:, d:d + 1]
        h = num / jnp.maximum(jnp.abs(den), jnp.exp(-f_col - m_col))
        m_end = jnp.maximum(m_prev, jnp.max(a_row, axis=1, keepdims=True))
        k_w = (k.T * jnp.exp(a_row - m_end)).astype(BF16)
        cst[...] = jnp.exp(m_prev - m_end) * c_prev + _dot(k_w, v_aug)
        hn = _rms_rows(h, ng_ref[...])
        o_ref[pl.ds(st, ln), :] = hn * jax.nn.sigmoid(og_ref[pl.ds(st, ln), :])
        return m_end

    lax.fori_loop(0, s_len // ln, step, jnp.zeros((1, 1), F32))


def _mlstm(att, rest, conv_w, conv_b, gcol, grow, norm_g):
    b, s, _ = att.shape
    nh = MLSTM_HEADS
    d = MLSTM_HEAD_DIM
    big = lambda off: pl.BlockSpec((None, s, d), lambda bi, h: (bi, 0, off + h))
    return pl.pallas_call(
        _mlstm_kernel,
        grid=(b, nh),
        in_specs=[
            big(nh), big(2 * nh), big(3 * nh), big(3 * nh),
            pl.BlockSpec((CONV_WIDTH, d), lambda bi, h: (0, h)),
            pl.BlockSpec((CONV_WIDTH, d), lambda bi, h: (0, nh + h)),
            pl.BlockSpec((1, d), lambda bi, h: (0, h)),
            pl.BlockSpec((1, d), lambda bi, h: (0, nh + h)),
            pl.BlockSpec((None, s, LANES), lambda bi, h: (bi, 0, 0)),
            pl.BlockSpec((None, None, 8, s), lambda bi, h: (bi, h, 0, 0)),
            pl.BlockSpec((1, d), lambda bi, h: (0, h)),
        ],
        out_specs=pl.BlockSpec((None, s, d), lambda bi, h: (bi, 0, h)),
        out_shape=jax.ShapeDtypeStruct((b, s, HALF), F32),
        scratch_shapes=[pltpu.VMEM((s, d), F32), pltpu.VMEM((s, d), F32), pltpu.VMEM((d, 2 * d), F32)],
        compiler_params=_cparams(("parallel", "parallel"), 40),
        name="mlstm",
    )(rest, rest, att, rest, conv_w, conv_w, conv_b, conv_b, gcol, grow, norm_g)


def _ssd_kernel(z_ref, x_ref, b_ref, c_ref, cwx_ref, cwb_ref, cwc_ref, cbx_ref, cbb_ref, cbc_ref,
                g_ref, e_ref, r_ref, dsk_ref, ng_ref, o_ref, xc, bc, cc, sst):
    s_len, gw = x_ref.shape
    ln = SEQ_TILE
    _conv_silu(x_ref, cwx_ref, cbx_ref, xc, 1.0)
    _conv_silu(b_ref, cwb_ref, cbb_ref, bc, 1.0)
    _conv_silu(c_ref, cwc_ref, cbc_ref, cc, 1.0)
    sst[...] = jnp.zeros_like(sst)
    row = lax.broadcasted_iota(jnp.int32, (ln, ln), 0)
    col = lax.broadcasted_iota(jnp.int32, (ln, ln), 1)
    tri = col <= row
    head_of_lane = _block_index(lax.broadcasted_iota(jnp.int32, (1, gw), 1), SSD_HEAD_DIM)

    def step(c, _):
        st = pl.multiple_of(c * ln, ln)
        x = xc[pl.ds(st, ln), :]
        bm = bc[pl.ds(st, ln), :]
        cm = cc[pl.ds(st, ln), :]
        z = z_ref[pl.ds(st, ln), :]
        parts = _split3(g_ref[pl.ds(st, ln), :])
        dt_e = sum(_dot(p, e_ref[0]) for p in parts)
        ac_e = sum(_dot(p, e_ref[1]) for p in parts)
        xdt = x * dt_e
        xdt_b = xdt.astype(BF16)
        cmb = cm.astype(BF16)
        cb = _dot_nt(cmb, bm.astype(BF16))
        ac_rows = r_ref[0:SSD_HEADS_PER_GROUP, pl.ds(st, ln)]
        y = jnp.zeros((ln, gw), F32)
        for hh in range(SSD_HEADS_PER_GROUP):
            ac_col = ac_e[:, hh * SSD_HEAD_DIM:hh * SSD_HEAD_DIM + 1]
            l_mat = jnp.exp(jnp.where(tri, ac_col - ac_rows[hh:hh + 1], NEG_INF))
            y = jnp.where(head_of_lane == hh, _dot((cb * l_mat).astype(BF16), xdt_b), y)
        ac_end = ac_e[ln - 1:ln, :]
        prev = sst[...]
        y = y + _dot(cmb, prev.astype(BF16)) * jnp.exp(ac_e)
        states = _dot(bm.T.astype(BF16), (xdt * jnp.exp(ac_end - ac_e)).astype(BF16))
        sst[...] = prev * jnp.exp(ac_end) + states
        y = y + dsk_ref[...] * x
        y = y * (z * jax.nn.sigmoid(z))
        o_ref[pl.ds(st, ln), :] = _rms_rows(y, ng_ref[...])
        return 0

    lax.fori_loop(0, s_len // ln, step, 0)


def _ssd(zx, bcin, conv_w, conv_b, gcol, expand, grow, d_row, norm_g):
    b, s, _ = zx.shape
    gw = SSD_GROUP_W
    ns = SSD_STATE
    ng = SSD_GROUPS
    xoff = HALF // gw
    boff = HALF // ns
    coff = (HALF + ng * ns) // ns
    return pl.pallas_call(
        _ssd_kernel,
        grid=(b, ng),
        in_specs=[
            pl.BlockSpec((None, s, gw), lambda bi, g: (bi, 0, g)),
            pl.BlockSpec((None, s, gw), lambda bi, g: (bi, 0, xoff + g)),
            pl.BlockSpec((None, s, ns), lambda bi, g: (bi, 0, g)),
            pl.BlockSpec((None, s, ns), lambda bi, g: (bi, 0, ng + g)),
            pl.BlockSpec((CONV_WIDTH, gw), lambda bi, g: (0, g)),
            pl.BlockSpec((CONV_WIDTH, ns), lambda bi, g: (0, boff + g)),
            pl.BlockSpec((CONV_WIDTH, ns), lambda bi, g: (0, coff + g)),
            pl.BlockSpec((1, gw), lambda bi, g: (0, g)),
            pl.BlockSpec((1, ns), lambda bi, g: (0, boff + g)),
            pl.BlockSpec((1, ns), lambda bi, g: (0, coff + g)),
            pl.BlockSpec((None, s, LANES), lambda bi, g: (bi, 0, 0)),
            pl.BlockSpec((None, 2, LANES, gw), lambda bi, g: (g, 0, 0, 0)),
            pl.BlockSpec((None, None, 8, s), lambda bi, g: (bi, g, 0, 0)),
            pl.BlockSpec((1, gw), lambda bi, g: (0, g)),
            pl.BlockSpec((1, gw), lambda bi, g: (0, g)),
        ],
        out_specs=pl.BlockSpec((None, s, gw), lambda bi, g: (bi, 0, g)),
        out_shape=jax.ShapeDtypeStruct((b, s, HALF), F32),
        scratch_shapes=[pltpu.VMEM((s, gw), F32), pltpu.VMEM((s, ns), F32), pltpu.VMEM((s, ns), F32),
                        pltpu.VMEM((ns, gw), F32)],
        compiler_params=_cparams(("parallel", "parallel"), 48),
        name="ssd",
    )(zx, zx, bcin, bcin, conv_w, conv_w, conv_w, conv_b, conv_b, conv_b, gcol, expand, grow, d_row, norm_g)


def _ssd_expand_matrices():
    e = np.zeros((SSD_GROUPS, 2, LANES, SSD_GROUP_W), np.float32)
    for g in range(SSD_GROUPS):
        for h in range(SSD_HEADS_PER_GROUP):
            head = g * SSD_HEADS_PER_GROUP + h
            e[g, 0, head, h * SSD_HEAD_DIM:(h + 1) * SSD_HEAD_DIM] = 1.0
            e[g, 1, SSD_HEADS + head, h * SSD_HEAD_DIM:(h + 1) * SSD_HEAD_DIM] = 1.0
    return jnp.asarray(e, BF16)


def _out_proj_kernel(a_ref, b_ref, x_ref, w_ref, o_ref):
    half = a_ref.shape[1]
    o_ref[...] = (x_ref[...] + _dot(a_ref[...].astype(BF16), w_ref[:half, :])
                  + _dot(b_ref[...].astype(BF16), w_ref[half:, :]))


def _out_proj(mix_a, mix_b, x2d, w, tm=512):
    t, d = x2d.shape
    half = mix_a.shape[1]
    return pl.pallas_call(
        _out_proj_kernel,
        grid=(t // tm,),
        in_specs=[pl.BlockSpec((tm, half), lambda i: (i, 0)),
                  pl.BlockSpec((tm, half), lambda i: (i, 0)),
                  pl.BlockSpec((tm, d), lambda i: (i, 0)),
                  pl.BlockSpec((2 * half, d), lambda i: (0, 0))],
        out_specs=pl.BlockSpec((tm, d), lambda i: (i, 0)),
        out_shape=jax.ShapeDtypeStruct((t, d), F32),
        compiler_params=_cparams(("parallel",), 32),
        name="out_proj",
    )(mix_a, mix_b, x2d, w)


def _moe_route(logits_t):
    gl = [logits_t[g:g + 1, :] for g in range(MOE_GROUPS)]
    g_max = functools.reduce(jnp.maximum, gl)
    g_den = sum(jnp.exp(x - g_max) for x in gl)
    g_w = 1.0 / g_den
    taken = jnp.zeros_like(g_max) > 1.0
    is_g = []
    for g in range(MOE_GROUPS):
        hit = (gl[g] == g_max) & jnp.logical_not(taken)
        is_g.append(hit)
        taken = taken | hit
    e_in = []
    for j in range(MOE_EPG):
        v = jnp.zeros_like(g_max)
        for g in range(MOE_GROUPS):
            row = MOE_GROUPS + g * MOE_EPG + j
            v = jnp.where(is_g[g], logits_t[row:row + 1, :], v)
        e_in.append(v)
    rank = []
    for j in range(MOE_EPG):
        r = jnp.zeros_like(g_max)
        for m in range(MOE_EPG):
            if m == j:
                continue
            ahead = (e_in[m] > e_in[j]) | ((e_in[m] == e_in[j]) & (m < j))
            r = r + jnp.where(ahead, 1.0, 0.0)
        rank.append(r)
    v0 = sum(jnp.where(rank[j] == 0.0, e_in[j], 0.0) for j in range(MOE_EPG))
    v1 = sum(jnp.where(rank[j] == 1.0, e_in[j], 0.0) for j in range(MOE_EPG))
    e1 = jnp.exp(v1 - v0)
    w0 = 1.0 / (1.0 + e1)
    w1 = e1 / (1.0 + e1)
    comb = []
    for g in range(MOE_GROUPS):
        for j in range(MOE_EPG):
            wj = jnp.where(rank[j] == 0.0, w0, jnp.where(rank[j] == 1.0, w1, 0.0))
            comb.append(jnp.where(is_g[g], g_w * wj, 0.0))
    return comb, is_g


ROUTE_GROUP, ROUTE_CHUNK, ROUTE_SLOT, ROUTE_COMB = 0, 1, 2, 8


def _moe_kernel(x_ref, g_ref, wrh_ref, wrl_ref, rb_ref, tri_ref, wgu_ref, wd_ref, o_ref,
                h_sc, rows_sc, cols_sc, cnt_sc):
    grp = pl.program_id(1)
    tm, d = x_ref.shape
    ch = MOE_CHUNK

    @pl.when(grp == 0)
    def _():
        x = x_ref[...]
        h_hi, h_lo = _split2(_rms_rows(x, g_ref[...]))
        h_sc[...] = h_hi
        logits_t = (_dot_nt(wrh_ref[...], h_hi) + _dot_nt(wrh_ref[...], h_lo) + _dot_nt(wrl_ref[...], h_hi)
                    + rb_ref[...])
        comb, is_g = _moe_route(logits_t)
        member = jnp.concatenate([jnp.where(m, 1.0, 0.0) for m in is_g]
                                 + [jnp.zeros((16 - MOE_GROUPS, tm), F32)], axis=0)
        incl = _dot(member.astype(BF16), tri_ref[...])
        pos = sum(jnp.where(is_g[g], incl[g:g + 1, :] - 1.0, 0.0) for g in range(MOE_GROUPS))
        gid = sum(jnp.where(is_g[g], float(g), 0.0) for g in range(MOE_GROUPS))
        chunk = jnp.floor(pos * (1.0 / ch))
        rows = jnp.concatenate([gid, chunk, pos - ch * chunk, jnp.zeros((ROUTE_COMB - 3, tm), F32)] + comb
                               + [jnp.zeros((LANES - ROUTE_COMB - MOE_EXPERTS, tm), F32)], axis=0)
        rows_sc[...] = rows
        cols_sc[...] = rows.T
        cnt_sc[...] = jnp.broadcast_to(incl[:, tm - 1:tm], cnt_sc.shape)
        o_ref[...] = x

    lane = lax.broadcasted_iota(jnp.int32, (1, LANES), 1)
    grp_row = lax.broadcasted_iota(jnp.int32, cnt_sc.shape, 0) == grp
    n_grp = jnp.sum(jnp.where(grp_row, cnt_sc[...], 0.0), axis=0, keepdims=True)[0, 0].astype(jnp.int32)
    grp_f = grp.astype(F32)
    for c in range(tm // ch):
        @pl.when(n_grp > c * ch)
        def _():
            rows = rows_sc[...]
            cols = cols_sc[...]
            in_chunk = (rows[ROUTE_GROUP:ROUTE_GROUP + 1, :] == grp_f) & (rows[ROUTE_CHUNK:ROUTE_CHUNK + 1, :] == c)
            slot = lax.broadcasted_iota(jnp.int32, (ch, tm), 0).astype(F32)
            gather = jnp.where(in_chunk & (rows[ROUTE_SLOT:ROUTE_SLOT + 1, :] == slot), 1.0, 0.0).astype(BF16)
            in_chunk_t = (cols[:, ROUTE_GROUP:ROUTE_GROUP + 1] == grp_f) & (cols[:, ROUTE_CHUNK:ROUTE_CHUNK + 1] == c)
            slot_t = lax.broadcasted_iota(jnp.int32, (tm, ch), 1).astype(F32)
            scatter = jnp.where(in_chunk_t & (cols[:, ROUTE_SLOT:ROUTE_SLOT + 1] == slot_t), 1.0, 0.0).astype(BF16)
            hs = _dot(gather, h_sc[...]).astype(BF16)
            table = sum(_dot(gather, part) for part in _split3(cols))
            y = jnp.zeros((ch, d), F32)
            for j in range(MOE_EPG):
                cw = jnp.sum(jnp.where(lane == ROUTE_COMB + MOE_EPG * grp + j, table, 0.0), axis=1, keepdims=True)
                ab = _dot(hs, wgu_ref[j])
                a = ab[:, :MOE_HIDDEN]
                hid = (a * jax.nn.sigmoid(a)) * ab[:, MOE_HIDDEN:] * cw
                y = y + _dot(hid.astype(BF16), wd_ref[j])
            y_hi, y_lo = _split2(y)
            o_ref[...] += _dot(scatter, y_hi) + _dot(scatter, y_lo)


def _moe(x2d, g, wr_hi, wr_lo, rb, wgu, wd, tm=1024):
    t, d = x2d.shape
    tri = jnp.triu(jnp.ones((tm, tm), BF16))
    const = lambda shape: pl.BlockSpec(shape, lambda i, gi: (0,) * len(shape), pipeline_mode=pl.Buffered(1))
    return pl.pallas_call(
        _moe_kernel,
        grid=(t // tm, MOE_GROUPS),
        in_specs=[pl.BlockSpec((tm, d), lambda i, gi: (i, 0)),
                  const((1, d)), const((MOE_ROUTER_ROWS, d)), const((MOE_ROUTER_ROWS, d)),
                  const((MOE_ROUTER_ROWS, 1)), const((tm, tm)),
                  pl.BlockSpec((MOE_EPG, d, 2 * MOE_HIDDEN), lambda i, gi: (gi, 0, 0)),
                  pl.BlockSpec((MOE_EPG, MOE_HIDDEN, d), lambda i, gi: (gi, 0, 0))],
        out_specs=pl.BlockSpec((tm, d), lambda i, gi: (i, 0)),
        out_shape=jax.ShapeDtypeStruct((t, d), F32),
        scratch_shapes=[pltpu.VMEM((tm, d), BF16), pltpu.VMEM((LANES, tm), F32), pltpu.VMEM((tm, LANES), F32),
                        pltpu.VMEM((16, LANES), F32)],
        compiler_params=_cparams(("parallel", "arbitrary"), 52),
        name="moe",
    )(x2d, g, wr_hi, wr_lo, rb, tri, wgu, wd)


def _ple_kernel(x_ref, p_ref, wp_ref, wg_ref, g1_ref, g2_ref, o_ref):
    x = x_ref[...]
    e = _dot(p_ref[...].astype(BF16), wp_ref[...])
    gate = jax.nn.sigmoid(_dot(_rms_rows(x, g1_ref[...]).astype(BF16), wg_ref[...]))
    o_ref[...] = x + _rms_rows(e * gate, g2_ref[...])


def _ple(x2d, p2d, wp, wg, g1, g2, tm=512):
    t, d = x2d.shape
    kp = p2d.shape[1]
    return pl.pallas_call(
        _ple_kernel,
        grid=(t // tm,),
        in_specs=[pl.BlockSpec((tm, d), lambda i: (i, 0)),
                  pl.BlockSpec((tm, kp), lambda i: (i, 0)),
                  pl.BlockSpec((kp, d), lambda i: (0, 0)),
                  pl.BlockSpec((d, d), lambda i: (0, 0)),
                  pl.BlockSpec((1, d), lambda i: (0, 0)),
                  pl.BlockSpec((1, d), lambda i: (0, 0))],
        out_specs=pl.BlockSpec((tm, d), lambda i: (i, 0)),
        out_shape=jax.ShapeDtypeStruct((t, d), F32),
        compiler_params=_cparams(("parallel",), 32),
        name="ple",
    )(x2d, p2d, wp, wg, g1, g2)


def _pad_lanes(cols, width=LANES):
    return jnp.pad(cols, ((0, 0), (0, width - cols.shape[-1])))


def _row(v):
    return v.reshape(1, -1).astype(F32)


def _even_mix(x2d, bsz, seq, norm_g, w_in, fox_b_f, fox_qn_g, fox_kn_g, conv_w, conv_b, b_i, b_f, mnorm_g):
    hw = HALF
    o = np.cumsum([0, hw, hw, hw, ATT_HEADS, hw, hw, hw, hw, MLSTM_HEADS, MLSTM_HEADS, hw])
    seg = lambda i: w_in[:, o[i]:o[i + 1]]
    fq, fk, fv, ff, fo, mq, mk, mv, mi, mf, mo = [seg(i) for i in range(11)]
    w = jnp.concatenate([fq, fk, fv, mv, fo, mq, mk, mo, _pad_lanes(jnp.concatenate([ff, mi, mf], axis=1))],
                        axis=1).astype(BF16)
    gain = jnp.concatenate([jnp.tile(fox_qn_g, ATT_HEADS) * ATT_Q_SCALE, jnp.tile(fox_kn_g, ATT_HEADS)])
    att, rest, gates = _in_proj(x2d, _row(norm_g), w, _row(gain), ((4 * hw, BF16), (4 * hw, F32), (LANES, F32)),
                                2 * hw)
    att = att.reshape(bsz, seq, 4 * hw)
    rest = rest.reshape(bsz, seq, 4 * hw)
    bias = _pad_lanes(_row(jnp.concatenate([fox_b_f, b_i, b_f])))
    gcol, rm = _prep_even(gates.reshape(bsz, seq, LANES), bias)
    out_a = _fox_attention(att, rest, gcol)
    out_b = _mlstm(att, rest, conv_w, _row(conv_b), gcol, rm, _row(mnorm_g))
    return out_a.reshape(-1, hw), out_b.reshape(-1, hw)


def _odd_mix(x2d, bsz, seq, norm_g, w_in, moba_qn_g, moba_kn_g, conv_w, conv_b, dt_bias, a_log, d_skip, snorm_g):
    hw = HALF
    nbc = SSD_GROUPS * SSD_STATE
    o = np.cumsum([0, hw, hw, hw, hw, hw, nbc, nbc, SSD_HEADS])
    w = jnp.concatenate([w_in[:, :o[7]], _pad_lanes(w_in[:, o[7]:o[8]])], axis=1).astype(BF16)
    gain = jnp.concatenate([jnp.tile(moba_qn_g, ATT_HEADS) * ATT_Q_SCALE, jnp.tile(moba_kn_g, ATT_HEADS)])
    moba_u, zx, bcin, dts = _in_proj(x2d, _row(norm_g), w, _row(gain),
                                     ((3 * hw, BF16), (2 * hw, F32), (2 * nbc, F32), (LANES, F32)), 2 * hw)
    gcol, rs = _prep_odd(dts.reshape(bsz, seq, LANES), _pad_lanes(_row(dt_bias)), _pad_lanes(_row(a_log)))
    out_c = _moba_attention(moba_u.reshape(bsz, seq, 3 * hw))
    out_d = _ssd(zx.reshape(bsz, seq, 2 * hw), bcin.reshape(bsz, seq, 2 * nbc), conv_w, _row(conv_b), gcol,
                 _ssd_expand_matrices(), rs, _row(jnp.repeat(d_skip, SSD_HEAD_DIM)), _row(snorm_g))
    return out_c.reshape(-1, hw), out_d.reshape(-1, hw)


def _moe_layer(x2d, norm_g, w_group, b_group, w_router, b_router, w_gate, w_up, w_down):
    d = x2d.shape[1]
    wr = _pad_lanes(jnp.concatenate([w_group, w_router], axis=1), MOE_ROUTER_ROWS).T
    wr_hi = wr.astype(BF16)
    wr_lo = (wr - wr_hi.astype(F32)).astype(BF16)
    rb = _pad_lanes(_row(jnp.concatenate([b_group, b_router])), MOE_ROUTER_ROWS).T
    wgu = jnp.concatenate([w_gate.reshape(MOE_EXPERTS, d, MOE_HIDDEN), w_up.reshape(MOE_EXPERTS, d, MOE_HIDDEN)],
                          axis=-1).astype(BF16)
    wd = w_down.reshape(MOE_EXPERTS, MOE_HIDDEN, d).astype(BF16)
    return _moe(x2d, _row(norm_g), wr_hi, wr_lo, rb, wgu, wd)


def kernel(x, p, norm1_g, norm2_g, ev_w_in, ev_fox_b_f, ev_fox_qn_g, ev_fox_kn_g, ev_mlstm_conv_w, ev_mlstm_conv_b, ev_mlstm_b_i, ev_mlstm_b_f, ev_mlstm_norm_g, ev_w_out, od_w_in, od_moba_qn_g, od_moba_kn_g, od_ssd_conv_w, od_ssd_conv_b, od_ssd_dt_bias, od_ssd_A_log, od_ssd_D, od_ssd_norm_g, od_w_out, moe_w_group, moe_b_group, moe_w_router, moe_b_router, moe_w_gate, moe_w_up, moe_w_down, ple_w_proj, ple_w_gate, ple_gate_norm_g, ple_out_norm_g):
    bsz, seq, d = x.shape
    depth = p.shape[0]
    x2d = x.reshape(bsz * seq, d)
    for i in range(depth):
        j = i // 2
        if i % 2 == 0:
            mix_a, mix_b = _even_mix(x2d, bsz, seq, norm1_g[i], ev_w_in[j], ev_fox_b_f[j], ev_fox_qn_g[j],
                                     ev_fox_kn_g[j], ev_mlstm_conv_w[j], ev_mlstm_conv_b[j], ev_mlstm_b_i[j],
                                     ev_mlstm_b_f[j], ev_mlstm_norm_g[j])
            w_out = ev_w_out[j]
        else:
            mix_a, mix_b = _odd_mix(x2d, bsz, seq, norm1_g[i], od_w_in[j], od_moba_qn_g[j], od_moba_kn_g[j],
                                    od_ssd_conv_w[j], od_ssd_conv_b[j], od_ssd_dt_bias[j], od_ssd_A_log[j],
                                    od_ssd_D[j], od_ssd_norm_g[j])
            w_out = od_w_out[j]
        x2d = _out_proj(mix_a, mix_b, x2d, w_out.astype(BF16))
        x2d = _moe_layer(x2d, norm2_g[i], moe_w_group[i], moe_b_group[i], moe_w_router[i], moe_b_router[i],
                         moe_w_gate[i], moe_w_up[i], moe_w_down[i])
        x2d = _ple(x2d, p[i].reshape(bsz * seq, -1), ple_w_proj[i].astype(BF16), ple_w_gate[i].astype(BF16),
                   _row(ple_gate_norm_g[i]), _row(ple_out_norm_g[i]))
    return x2d.reshape(bsz, seq, d)
```

```python
import functools

import jax
import jax.numpy as jnp
import numpy as np
from jax import lax
from jax.experimental import pallas as pl
from jax.experimental.pallas import tpu as pltpu

F32 = jnp.float32
BF16 = jnp.bfloat16
MIX_DTYPE = BF16
NEG_INF = float("-inf")
LOG2E = 1.4426950408889634

NORM_EPS = 1e-6
D_MODEL = 1024
HALF = D_MODEL // 2
ATT_HEAD_DIM = 64
ATT_HEADS = HALF // ATT_HEAD_DIM
ATT_PAIRS = ATT_HEADS // 2
ATT_Q_SCALE = ATT_HEAD_DIM ** -0.5 * LOG2E
MLSTM_HEAD_DIM = 128
MLSTM_HEADS = HALF // MLSTM_HEAD_DIM
SSD_HEAD_DIM = 64
SSD_HEADS = HALF // SSD_HEAD_DIM
SSD_GROUPS = 2
SSD_STATE = 128
SSD_GROUP_W = HALF // SSD_GROUPS
SSD_HEADS_PER_GROUP = SSD_HEADS // SSD_GROUPS
CONV_WIDTH = 4
MOBA_BLOCK = 256
MOBA_TOPK = 3
MOE_GROUPS = 4
MOE_EPG = 4
MOE_EXPERTS = MOE_GROUPS * MOE_EPG
MOE_HIDDEN = D_MODEL // 4
MOE_ROUTER_ROWS = 32
MOE_CHUNK = 256
MOE_SMALL_CHUNK = 64
PLE_DIM = 256

LANES = 128
MXU_N = 256
SEQ_TILE = 256
ATT_TILE = 512
MIB = 1024 * 1024


def _cparams(sem, vmem_mib):
    return pltpu.CompilerParams(dimension_semantics=sem, vmem_limit_bytes=vmem_mib * MIB)


def _split2(x):
    hi = x.astype(BF16)
    lo = (x - hi.astype(F32)).astype(BF16)
    return hi, lo


def _split3(x):
    hi = x.astype(BF16)
    r = x - hi.astype(F32)
    mid = r.astype(BF16)
    lo = (r - mid.astype(F32)).astype(BF16)
    return hi, mid, lo


def _dot(a, b):
    return jnp.dot(a, b, preferred_element_type=F32)


def _dot_nt(a, b):
    return lax.dot_general(a, b, (((1,), (1,)), ((), ())), preferred_element_type=F32)


def _log_sigmoid(x):
    return jnp.minimum(x, 0.0) - jnp.log1p(jnp.exp(-jnp.abs(x)))


def _softplus(x):
    return jnp.maximum(x, 0.0) + jnp.log1p(jnp.exp(-jnp.abs(x)))


def _rms_rows(xf, g, eps=NORM_EPS):
    ms = jnp.mean(xf * xf, axis=-1, keepdims=True)
    return xf * lax.rsqrt(ms + eps) * g


def _block_index(i, block):
    shift = block.bit_length() - 1
    assert 1 << shift == block
    return lax.shift_right_logical(i, shift)


def _lane_col(x, idx):
    lane = lax.broadcasted_iota(jnp.int32, (1, x.shape[1]), 1)
    return jnp.sum(jnp.where(lane == idx, x, 0.0), axis=1, keepdims=True)


def _in_proj_kernel(x_ref, g_ref, w_ref, gain_ref, *out_refs, widths, n_norm, head_dim):
    h = _rms_rows(x_ref[...], g_ref[...]).astype(BF16)
    r = _block_index(lax.broadcasted_iota(jnp.int32, (MXU_N, MXU_N), 0), head_dim)
    c = _block_index(lax.broadcasted_iota(jnp.int32, (MXU_N, MXU_N), 1), head_dim)
    seg = jnp.where(r == c, 1.0, 0.0).astype(BF16)
    off = 0
    for o_ref, wdt in zip(out_refs, widths):
        for c0 in range(0, wdt, MXU_N):
            n = min(MXU_N, wdt - c0)
            y = _dot(h, w_ref[:, off + c0: off + c0 + n])
            if off + c0 < n_norm:
                hi, lo = _split2(y * y)
                ss = _dot(hi, seg) + _dot(lo, seg)
                y = y * lax.rsqrt(ss * (1.0 / head_dim) + NORM_EPS) * gain_ref[:, off + c0: off + c0 + n]
            o_ref[:, c0:c0 + n] = y.astype(o_ref.dtype)
        off += wdt


def _in_proj(x2d, g, w, gain, segments, n_norm, tm=512):
    t, d = x2d.shape
    n = w.shape[1]
    widths = tuple(wd for wd, _ in segments)
    kern = functools.partial(_in_proj_kernel, widths=widths, n_norm=n_norm, head_dim=ATT_HEAD_DIM)
    return pl.pallas_call(
        kern,
        grid=(t // tm,),
        in_specs=[
            pl.BlockSpec((tm, d), lambda i: (i, 0)),
            pl.BlockSpec((1, d), lambda i: (0, 0)),
            pl.BlockSpec((d, n), lambda i: (0, 0)),
            pl.BlockSpec((1, n_norm), lambda i: (0, 0)),
        ],
        out_specs=[pl.BlockSpec((tm, wd), lambda i: (i, 0)) for wd in widths],
        out_shape=[jax.ShapeDtypeStruct((t, wd), dt) for wd, dt in segments],
        compiler_params=_cparams(("parallel",), 52),
        name="in_proj",
    )(x2d, g, w, gain)


def _tri_cumsum(vals, tri):
    hi, mid, lo = _split3(vals)
    return _dot(tri, hi) + _dot(tri, mid) + _dot(tri, lo)


def _lower_tri(n):
    r = lax.broadcasted_iota(jnp.int32, (n, n), 0)
    c = lax.broadcasted_iota(jnp.int32, (n, n), 1)
    return jnp.where(c <= r, 1.0, 0.0).astype(BF16)


def _prep_even_kernel(raw_ref, bias_ref, g_ref, rm_ref):
    s = raw_ref.shape[0]
    ch = SEQ_TILE
    lane = lax.broadcasted_iota(jnp.int32, (1, LANES), 1)
    is_cum = (lane < ATT_HEADS) | ((lane >= ATT_HEADS + MLSTM_HEADS) & (lane < ATT_HEADS + 2 * MLSTM_HEADS))
    tri = _lower_tri(ch)
    carry = jnp.zeros((1, LANES), F32)
    pad = jnp.zeros((6, ch), F32)
    for c in range(s // ch):
        z = raw_ref[c * ch:(c + 1) * ch, :] + bias_ref[...]
        cs = _tri_cumsum(jnp.where(is_cum, _log_sigmoid(z), 0.0), tri) + carry
        carry = cs[ch - 1:ch, :]
        g = jnp.where(is_cum, cs, z)
        g_ref[c * ch:(c + 1) * ch, :] = g
        gt = g.T
        for hh in range(MLSTM_HEADS):
            i_row = gt[ATT_HEADS + hh:ATT_HEADS + hh + 1]
            f_row = gt[ATT_HEADS + MLSTM_HEADS + hh:ATT_HEADS + MLSTM_HEADS + hh + 1]
            rm_ref[hh, :, c * ch:(c + 1) * ch] = jnp.concatenate([i_row - f_row, f_row, pad], axis=0)


def _prep_even(raw, bias):
    b, s, _ = raw.shape
    return pl.pallas_call(
        _prep_even_kernel,
        grid=(b,),
        in_specs=[pl.BlockSpec((None, s, LANES), lambda i: (i, 0, 0)),
                  pl.BlockSpec((1, LANES), lambda i: (0, 0))],
        out_specs=[pl.BlockSpec((None, s, LANES), lambda i: (i, 0, 0)),
                   pl.BlockSpec((None, MLSTM_HEADS, 8, s), lambda i: (i, 0, 0, 0))],
        out_shape=[jax.ShapeDtypeStruct((b, s, LANES), F32),
                   jax.ShapeDtypeStruct((b, MLSTM_HEADS, 8, s), F32)],
        compiler_params=_cparams(("parallel",), 32),
        name="prep_even",
    )(raw, bias)


def _prep_odd_kernel(raw_ref, bias_ref, alog_ref, g_ref, rs_ref):
    s = raw_ref.shape[0]
    ch = SEQ_TILE
    lane = lax.broadcasted_iota(jnp.int32, (1, LANES), 1)
    is_dt = lane < SSD_HEADS
    tri = _lower_tri(ch)
    a_row = jnp.where(is_dt, -jnp.exp(alog_ref[...]), 0.0)
    pad = jnp.zeros((8 - SSD_HEADS_PER_GROUP, ch), F32)
    for c in range(s // ch):
        dt = jnp.where(is_dt, _softplus(raw_ref[c * ch:(c + 1) * ch, :] + bias_ref[...]), 0.0)
        acum = _tri_cumsum(a_row * dt, tri)
        g = jnp.where(is_dt, dt, pltpu.roll(acum, SSD_HEADS, 1))
        g_ref[c * ch:(c + 1) * ch, :] = g
        gt = g.T
        for gi in range(SSD_GROUPS):
            lo = SSD_HEADS + gi * SSD_HEADS_PER_GROUP
            rs_ref[gi, :, c * ch:(c + 1) * ch] = jnp.concatenate([gt[lo:lo + SSD_HEADS_PER_GROUP], pad], axis=0)


def _prep_odd(raw, bias, alog):
    b, s, _ = raw.shape
    return pl.pallas_call(
        _prep_odd_kernel,
        grid=(b,),
        in_specs=[pl.BlockSpec((None, s, LANES), lambda i: (i, 0, 0)),
                  pl.BlockSpec((1, LANES), lambda i: (0, 0)),
                  pl.BlockSpec((1, LANES), lambda i: (0, 0))],
        out_specs=[pl.BlockSpec((None, s, LANES), lambda i: (i, 0, 0)),
                   pl.BlockSpec((None, SSD_GROUPS, 8, s), lambda i: (i, 0, 0, 0))],
        out_shape=[jax.ShapeDtypeStruct((b, s, LANES), F32),
                   jax.ShapeDtypeStruct((b, SSD_GROUPS, 8, s), F32)],
        compiler_params=_cparams(("parallel",), 32),
        name="prep_odd",
    )(raw, bias, alog)


def _two_pass_softmax_pv(n_past, past_span, past_logits, own_span, own_logits, v_rows, s_scr, tq):
    heads = range(2)
    neg = jnp.full((1, tq), NEG_INF, F32)

    def store_max(span, s_t, hh, m):
        s_scr[hh, pl.ds(*span), :] = s_t
        return jnp.maximum(m, jnp.max(s_t, axis=0, keepdims=True))

    def accumulate(span, hh, m, carry):
        l, acc = carry
        p = jnp.exp2(s_scr[hh, pl.ds(*span), :] - m)
        return l + jnp.sum(p, axis=0, keepdims=True), acc + _dot(v_rows(*span, hh), p.astype(BF16))

    ms = lax.fori_loop(
        0, n_past, lambda j, ms: tuple(store_max(past_span(j), past_logits(j, hh), hh, ms[hh]) for hh in heads),
        (neg, neg))
    ms = [store_max(own_span, own_logits(hh), hh, ms[hh]) for hh in heads]
    zero = (jnp.zeros((1, tq), F32), jnp.zeros((ATT_HEAD_DIM, tq), F32))
    carry = tuple(accumulate(own_span, hh, ms[hh], zero) for hh in heads)
    carry = lax.fori_loop(
        0, n_past, lambda j, c: tuple(accumulate(past_span(j), hh, ms[hh], c[hh]) for hh in heads), carry)
    return jnp.concatenate([acc / l for l, acc in carry], axis=0).T


def _stage_v(v_ref, vt):
    for c in range(v_ref.shape[0] // SEQ_TILE):
        vt[:, c * SEQ_TILE:(c + 1) * SEQ_TILE] = v_ref[c * SEQ_TILE:(c + 1) * SEQ_TILE, :].astype(F32).T.astype(BF16)


def _head_queries(q2):
    lane = lax.broadcasted_iota(jnp.int32, (1, LANES), 1)
    return [jnp.where(lane < ATT_HEAD_DIM, q2, 0.0), jnp.where(lane >= ATT_HEAD_DIM, q2, 0.0)]


def _fox_kernel(q_ref, k_ref, v_ref, fo_ref, g_ref, o_ref, vt, cfb, s_scr):
    hp = pl.program_id(1)
    qi = pl.program_id(2)
    tq = q_ref.shape[0]
    s_len = k_ref.shape[0]
    hd = ATT_HEAD_DIM

    @pl.when(qi == 0)
    def _():
        _stage_v(v_ref, vt)
        g = g_ref[...]
        for hh in range(2):
            cfb[hh] = jnp.broadcast_to(_lane_col(g, 2 * hp + hh) * LOG2E, (s_len, LANES))

    qh = _head_queries(q_ref[...])

    def logits(j, hh):
        st = pl.multiple_of(j * tq, tq)
        s_t = _dot_nt(k_ref[pl.ds(st, tq), :], qh[hh])
        return s_t - jnp.concatenate([cfb[hh, pl.ds(st, tq), :]] * (tq // LANES), axis=1)

    def span(j):
        return pl.multiple_of(j * tq, tq), tq

    krow = lax.broadcasted_iota(jnp.int32, (tq, tq), 0)
    qcol = lax.broadcasted_iota(jnp.int32, (tq, tq), 1)
    out = _two_pass_softmax_pv(
        qi, span, logits, span(qi), lambda hh: jnp.where(krow <= qcol, logits(qi, hh), NEG_INF),
        lambda st, size, hh: vt[hh * hd:(hh + 1) * hd, pl.ds(st, size)], s_scr, tq)
    o_ref[...] = (out * jax.nn.sigmoid(fo_ref[...])).astype(o_ref.dtype)


def _fox_attention(qkv, ogate, gcol, tq=ATT_TILE):
    b, s, _ = qkv.shape
    np_ = ATT_PAIRS
    return pl.pallas_call(
        _fox_kernel,
        grid=(b, np_, s // tq),
        in_specs=[
            pl.BlockSpec((None, tq, LANES), lambda bi, hp, qi: (bi, qi, hp)),
            pl.BlockSpec((None, s, LANES), lambda bi, hp, qi: (bi, 0, np_ + hp)),
            pl.BlockSpec((None, s, LANES), lambda bi, hp, qi: (bi, 0, 2 * np_ + hp)),
            pl.BlockSpec((None, tq, LANES), lambda bi, hp, qi: (bi, qi, hp)),
            pl.BlockSpec((None, s, LANES), lambda bi, hp, qi: (bi, 0, 0)),
        ],
        out_specs=pl.BlockSpec((None, tq, LANES), lambda bi, hp, qi: (bi, qi, hp)),
        out_shape=jax.ShapeDtypeStruct((b, s, HALF), MIX_DTYPE),
        scratch_shapes=[pltpu.VMEM((LANES, s), BF16), pltpu.VMEM((2, s, LANES), F32),
                        pltpu.VMEM((2, s, tq), F32)],
        compiler_params=_cparams(("parallel", "parallel", "arbitrary"), 40),
        name="fox_attention",
    )(qkv, qkv, qkv, ogate, gcol)


def _moba_kernel(q_ref, k_ref, v_ref, o_ref, vt, kmh, kml, bias, s_scr):
    qi = pl.program_id(2)
    tq = q_ref.shape[0]
    s_len = k_ref.shape[0]
    blk = MOBA_BLOCK
    nb = s_len // blk
    nbp = kmh.shape[0]
    hd = ATT_HEAD_DIM
    per_tile = tq // blk

    @pl.when(qi == 0)
    def _():
        _stage_v(v_ref, vt)
        means = [jnp.mean(k_ref[n * blk:(n + 1) * blk, :].astype(F32), axis=0, keepdims=True) for n in range(nb)]
        km = jnp.concatenate(means + [jnp.zeros((nbp - nb, LANES), F32)], axis=0)
        kmh[...], kml[...] = _split2(km)

    qcol1 = lax.broadcasted_iota(jnp.int32, (1, tq), 1)
    q_blk = per_tile * qi + _block_index(qcol1, blk)
    nrow = lax.broadcasted_iota(jnp.int32, (nbp, tq), 0)
    valid = nrow < q_blk
    qh = _head_queries(q_ref[...])
    for hh in range(2):
        gate = _dot_nt(kmh[...], qh[hh]) + _dot_nt(kml[...], qh[hh])
        gate = jnp.where(valid, gate, NEG_INF)
        rank = jnp.zeros((nbp, tq), F32)
        for m in range(nb):
            gm = gate[m:m + 1, :]
            rank = rank + jnp.where((gm > gate) | ((gm == gate) & (nrow > m)), 1.0, 0.0)
        selb = jnp.where((rank < float(MOBA_TOPK)) & valid, 0.0, NEG_INF)
        for n in range(nb):
            bias[hh, n] = jnp.broadcast_to(selb[n:n + 1, :], (8, tq))

    def v_rows(st, size, hh):
        return vt[hh * hd:(hh + 1) * hd, pl.ds(st, size)]

    st = pl.multiple_of(qi * tq, tq)
    krow = lax.broadcasted_iota(jnp.int32, (tq, tq), 0)
    qcol = lax.broadcasted_iota(jnp.int32, (tq, tq), 1)
    same_blk = _block_index(krow, blk) == _block_index(qcol, blk)

    def sel_rows(j, hh):
        return jnp.concatenate(
            [jnp.broadcast_to(bias[hh, per_tile * j + n][0:1, :], (blk, tq)) for n in range(per_tile)], axis=0)

    def own_logits(hh):
        s_t = _dot_nt(k_ref[pl.ds(st, tq), :], qh[hh])
        return s_t + jnp.where(krow <= qcol, jnp.where(same_blk, 0.0, sel_rows(qi, hh)), NEG_INF)

    def span(j):
        return pl.multiple_of(j * tq, tq), tq

    def past_logits(j, hh):
        return _dot_nt(k_ref[pl.ds(*span(j)), :], qh[hh]) + sel_rows(j, hh)

    out = _two_pass_softmax_pv(qi, span, past_logits, (st, tq), own_logits, v_rows, s_scr, tq)
    o_ref[...] = out.astype(o_ref.dtype)


def _moba_attention(qkv, tq=ATT_TILE):
    b, s, _ = qkv.shape
    np_ = ATT_PAIRS
    nb = s // MOBA_BLOCK
    nbp = -(-nb // 16) * 16
    return pl.pallas_call(
        _moba_kernel,
        grid=(b, np_, s // tq),
        in_specs=[
            pl.BlockSpec((None, tq, LANES), lambda bi, hp, qi: (bi, qi, hp)),
            pl.BlockSpec((None, s, LANES), lambda bi, hp, qi: (bi, 0, np_ + hp)),
            pl.BlockSpec((None, s, LANES), lambda bi, hp, qi: (bi, 0, 2 * np_ + hp)),
        ],
        out_specs=pl.BlockSpec((None, tq, LANES), lambda bi, hp, qi: (bi, qi, hp)),
        out_shape=jax.ShapeDtypeStruct((b, s, HALF), MIX_DTYPE),
        scratch_shapes=[pltpu.VMEM((LANES, s), BF16),
                        pltpu.VMEM((nbp, LANES), BF16), pltpu.VMEM((nbp, LANES), BF16),
                        pltpu.VMEM((2, nb, 8, tq), F32), pltpu.VMEM((2, s, tq), F32)],
        compiler_params=_cparams(("parallel", "parallel", "arbitrary"), 40),
        name="moba_attention",
    )(qkv, qkv, qkv)


def _conv_silu(x_ref, w_ref, b_ref, o_ref, scale):
    s_len, ch = x_ref.shape
    lc = SEQ_TILE
    w = w_ref[...]
    b = b_ref[...]
    row = lax.broadcasted_iota(jnp.int32, (lc, ch), 0)

    def body(c, _):
        st = pl.multiple_of(c * lc, lc)
        pst = pl.multiple_of(jnp.maximum(c - 1, 0) * lc, lc)
        cur = x_ref[pl.ds(st, lc), :]
        prev = jnp.where(c > 0, x_ref[pl.ds(pst, lc), :], 0.0)
        y = b
        for j in range(CONV_WIDTH - 1):
            sh = CONV_WIDTH - 1 - j
            shifted = jnp.where(row < sh, pltpu.roll(prev, sh, 0), pltpu.roll(cur, sh, 0))
            y = y + shifted * w[j:j + 1]
        y = y + cur * w[CONV_WIDTH - 1:CONV_WIDTH]
        y = y * jax.nn.sigmoid(y)
        o_ref[pl.ds(st, lc), :] = y * scale if scale != 1.0 else y
        return 0

    lax.fori_loop(0, s_len // lc, body, 0)


def _mlstm_kernel(q_ref, k_ref, v_ref, og_ref, cwq_ref, cwk_ref, cbq_ref, cbk_ref, g_ref, r_ref, ng_ref,
                  o_ref, qc, kc, cst):
    hh = pl.program_id(1)
    s_len, d = q_ref.shape
    ln = SEQ_TILE
    _conv_silu(q_ref, cwq_ref, cbq_ref, qc, 1.0)
    _conv_silu(k_ref, cwk_ref, cbk_ref, kc, d ** -0.5)
    cst[...] = jnp.zeros_like(cst)
    row = lax.broadcasted_iota(jnp.int32, (ln, ln), 0)
    col = lax.broadcasted_iota(jnp.int32, (ln, ln), 1)
    tri = col <= row
    lane = lax.broadcasted_iota(jnp.int32, (1, LANES), 1)
    ones_blk = jnp.broadcast_to(jnp.where(lane == 0, 1.0, 0.0), (ln, LANES)).astype(BF16)
    f_lane = ATT_HEADS + MLSTM_HEADS + hh

    def step(c, m_prev):
        st = pl.multiple_of(c * ln, ln)
        q = qc[pl.ds(st, ln), :]
        k = kc[pl.ds(st, ln), :]
        v = v_ref[pl.ds(st, ln), :]
        f_col = _lane_col(g_ref[pl.ds(st, ln), :], f_lane)
        a_row = r_ref[0:1, pl.ds(st, ln)]
        a_mat = jnp.where(tri, a_row, NEG_INF)
        m_col = jnp.maximum(m_prev, jnp.max(a_mat, axis=1, keepdims=True))
        qb = q.astype(BF16)
        w_qk = _dot_nt(qb, k.astype(BF16)) * jnp.exp(a_mat - m_col)
        v_aug = jnp.concatenate([v, ones_blk], axis=1)
        c_prev = cst[...]
        inter = jnp.exp(m_prev - m_col)
        q_c = _dot(qb, c_prev.astype(BF16))
        num = _dot(w_qk.astype(BF16), v_aug[:, :d]) + inter * q_c[:, :d]
        den = jnp.sum(w_qk, axis=1, keepdims=True) + inter * q_c[:, d:d + 1]
        h = num / jnp.maximum(jnp.abs(den), jnp.exp(-f_col - m_col))
        m_end = jnp.maximum(m_prev, jnp.max(a_row, axis=1, keepdims=True))
        k_w = (k.T * jnp.exp(a_row - m_end)).astype(BF16)
        cst[...] = jnp.exp(m_prev - m_end) * c_prev + _dot(k_w, v_aug)
        hn = _rms_rows(h, ng_ref[...])
        o_ref[pl.ds(st, ln), :] = (hn * jax.nn.sigmoid(og_ref[pl.ds(st, ln), :])).astype(o_ref.dtype)
        return m_end

    lax.fori_loop(0, s_len // ln, step, jnp.zeros((1, 1), F32))


def _mlstm(att, rest, conv_w, conv_b, gcol, grow, norm_g):
    b, s, _ = att.shape
    nh = MLSTM_HEADS
    d = MLSTM_HEAD_DIM
    big = lambda off: pl.BlockSpec((None, s, d), lambda bi, h: (bi, 0, off + h))
    return pl.pallas_call(
        _mlstm_kernel,
        grid=(b, nh),
        in_specs=[
            big(nh), big(2 * nh), big(3 * nh), big(3 * nh),
            pl.BlockSpec((CONV_WIDTH, d), lambda bi, h: (0, h)),
            pl.BlockSpec((CONV_WIDTH, d), lambda bi, h: (0, nh + h)),
            pl.BlockSpec((1, d), lambda bi, h: (0, h)),
            pl.BlockSpec((1, d), lambda bi, h: (0, nh + h)),
            pl.BlockSpec((None, s, LANES), lambda bi, h: (bi, 0, 0)),
            pl.BlockSpec((None, None, 8, s), lambda bi, h: (bi, h, 0, 0)),
            pl.BlockSpec((1, d), lambda bi, h: (0, h)),
        ],
        out_specs=pl.BlockSpec((None, s, d), lambda bi, h: (bi, 0, h)),
        out_shape=jax.ShapeDtypeStruct((b, s, HALF), MIX_DTYPE),
        scratch_shapes=[pltpu.VMEM((s, d), F32), pltpu.VMEM((s, d), F32), pltpu.VMEM((d, 2 * d), F32)],
        compiler_params=_cparams(("parallel", "parallel"), 40),
        name="mlstm",
    )(rest, rest, att, rest, conv_w, conv_w, conv_b, conv_b, gcol, grow, norm_g)


def _ssd_kernel(z_ref, x_ref, b_ref, c_ref, cwx_ref, cwb_ref, cwc_ref, cbx_ref, cbb_ref, cbc_ref,
                g_ref, e_ref, r_ref, dsk_ref, ng_ref, o_ref, xc, bc, cc, sst):
    s_len, gw = x_ref.shape
    ln = SEQ_TILE
    _conv_silu(x_ref, cwx_ref, cbx_ref, xc, 1.0)
    _conv_silu(b_ref, cwb_ref, cbb_ref, bc, 1.0)
    _conv_silu(c_ref, cwc_ref, cbc_ref, cc, 1.0)
    sst[...] = jnp.zeros_like(sst)
    row = lax.broadcasted_iota(jnp.int32, (ln, ln), 0)
    col = lax.broadcasted_iota(jnp.int32, (ln, ln), 1)
    tri = col <= row
    head_of_lane = _block_index(lax.broadcasted_iota(jnp.int32, (1, gw), 1), SSD_HEAD_DIM)

    def step(c, _):
        st = pl.multiple_of(c * ln, ln)
        x = xc[pl.ds(st, ln), :]
        bm = bc[pl.ds(st, ln), :]
        cm = cc[pl.ds(st, ln), :]
        z = z_ref[pl.ds(st, ln), :]
        parts = _split3(g_ref[pl.ds(st, ln), :])
        dt_e = sum(_dot(p, e_ref[0]) for p in parts)
        ac_e = sum(_dot(p, e_ref[1]) for p in parts)
        xdt = x * dt_e
        xdt_b = xdt.astype(BF16)
        cmb = cm.astype(BF16)
        cb = _dot_nt(cmb, bm.astype(BF16))
        ac_rows = r_ref[0:SSD_HEADS_PER_GROUP, pl.ds(st, ln)]
        y = jnp.zeros((ln, gw), F32)
        for hh in range(SSD_HEADS_PER_GROUP):
            ac_col = ac_e[:, hh * SSD_HEAD_DIM:hh * SSD_HEAD_DIM + 1]
            l_mat = jnp.exp(jnp.where(tri, ac_col - ac_rows[hh:hh + 1], NEG_INF))
            y = jnp.where(head_of_lane == hh, _dot((cb * l_mat).astype(BF16), xdt_b), y)
        ac_end = ac_e[ln - 1:ln, :]
        prev = sst[...]
        y = y + _dot(cmb, prev.astype(BF16)) * jnp.exp(ac_e)
        states = _dot(bm.T.astype(BF16), (xdt * jnp.exp(ac_end - ac_e)).astype(BF16))
        sst[...] = prev * jnp.exp(ac_end) + states
        y = y + dsk_ref[...] * x
        y = y * (z * jax.nn.sigmoid(z))
        o_ref[pl.ds(st, ln), :] = _rms_rows(y, ng_ref[...]).astype(o_ref.dtype)
        return 0

    lax.fori_loop(0, s_len // ln, step, 0)


def _ssd(zx, bcin, conv_w, conv_b, gcol, expand, grow, d_row, norm_g):
    b, s, _ = zx.shape
    gw = SSD_GROUP_W
    ns = SSD_STATE
    ng = SSD_GROUPS
    xoff = HALF // gw
    boff = HALF // ns
    coff = (HALF + ng * ns) // ns
    return pl.pallas_call(
        _ssd_kernel,
        grid=(b, ng),
        in_specs=[
            pl.BlockSpec((None, s, gw), lambda bi, g: (bi, 0, g)),
            pl.BlockSpec((None, s, gw), lambda bi, g: (bi, 0, xoff + g)),
            pl.BlockSpec((None, s, ns), lambda bi, g: (bi, 0, g)),
            pl.BlockSpec((None, s, ns), lambda bi, g: (bi, 0, ng + g)),
            pl.BlockSpec((CONV_WIDTH, gw), lambda bi, g: (0, g)),
            pl.BlockSpec((CONV_WIDTH, ns), lambda bi, g: (0, boff + g)),
            pl.BlockSpec((CONV_WIDTH, ns), lambda bi, g: (0, coff + g)),
            pl.BlockSpec((1, gw), lambda bi, g: (0, g)),
            pl.BlockSpec((1, ns), lambda bi, g: (0, boff + g)),
            pl.BlockSpec((1, ns), lambda bi, g: (0, coff + g)),
            pl.BlockSpec((None, s, LANES), lambda bi, g: (bi, 0, 0)),
            pl.BlockSpec((None, 2, LANES, gw), lambda bi, g: (g, 0, 0, 0)),
            pl.BlockSpec((None, None, 8, s), lambda bi, g: (bi, g, 0, 0)),
            pl.BlockSpec((1, gw), lambda bi, g: (0, g)),
            pl.BlockSpec((1, gw), lambda bi, g: (0, g)),
        ],
        out_specs=pl.BlockSpec((None, s, gw), lambda bi, g: (bi, 0, g)),
        out_shape=jax.ShapeDtypeStruct((b, s, HALF), MIX_DTYPE),
        scratch_shapes=[pltpu.VMEM((s, gw), F32), pltpu.VMEM((s, ns), F32), pltpu.VMEM((s, ns), F32),
                        pltpu.VMEM((ns, gw), F32)],
        compiler_params=_cparams(("parallel", "parallel"), 48),
        name="ssd",
    )(zx, zx, bcin, bcin, conv_w, conv_w, conv_w, conv_b, conv_b, conv_b, gcol, expand, grow, d_row, norm_g)


def _ssd_expand_matrices():
    e = np.zeros((SSD_GROUPS, 2, LANES, SSD_GROUP_W), np.float32)
    for g in range(SSD_GROUPS):
        for h in range(SSD_HEADS_PER_GROUP):
            head = g * SSD_HEADS_PER_GROUP + h
            e[g, 0, head, h * SSD_HEAD_DIM:(h + 1) * SSD_HEAD_DIM] = 1.0
            e[g, 1, SSD_HEADS + head, h * SSD_HEAD_DIM:(h + 1) * SSD_HEAD_DIM] = 1.0
    return jnp.asarray(e, BF16)


def _moe_route(logits_t):
    gl = [logits_t[g:g + 1, :] for g in range(MOE_GROUPS)]
    g_max = functools.reduce(jnp.maximum, gl)
    g_den = sum(jnp.exp(x - g_max) for x in gl)
    g_w = 1.0 / g_den
    taken = jnp.zeros_like(g_max) > 1.0
    is_g = []
    for g in range(MOE_GROUPS):
        hit = (gl[g] == g_max) & jnp.logical_not(taken)
        is_g.append(hit)
        taken = taken | hit
    e_in = []
    for j in range(MOE_EPG):
        v = jnp.zeros_like(g_max)
        for g in range(MOE_GROUPS):
            row = MOE_GROUPS + g * MOE_EPG + j
            v = jnp.where(is_g[g], logits_t[row:row + 1, :], v)
        e_in.append(v)
    rank = []
    for j in range(MOE_EPG):
        r = jnp.zeros_like(g_max)
        for m in range(MOE_EPG):
            if m == j:
                continue
            ahead = (e_in[m] > e_in[j]) | ((e_in[m] == e_in[j]) & (m < j))
            r = r + jnp.where(ahead, 1.0, 0.0)
        rank.append(r)
    v0 = sum(jnp.where(rank[j] == 0.0, e_in[j], 0.0) for j in range(MOE_EPG))
    v1 = sum(jnp.where(rank[j] == 1.0, e_in[j], 0.0) for j in range(MOE_EPG))
    e1 = jnp.exp(v1 - v0)
    w0 = 1.0 / (1.0 + e1)
    w1 = e1 / (1.0 + e1)
    comb = []
    for g in range(MOE_GROUPS):
        for j in range(MOE_EPG):
            wj = jnp.where(rank[j] == 0.0, w0, jnp.where(rank[j] == 1.0, w1, 0.0))
            comb.append(jnp.where(is_g[g], g_w * wj, 0.0))
    return comb, is_g


ROUTE_GROUP, ROUTE_CHUNK, ROUTE_SLOT, ROUTE_COMB = 0, 1, 2, 8


def _moe_kernel(x_ref, ma_ref, mb_ref, wo_ref, g_ref, wrh_ref, wrl_ref, rb_ref, tri_ref, wgu_ref, wd_ref, o_ref,
                h_sc, rows_sc, cols_sc, cnt_sc):
    grp = pl.program_id(1)
    tm, d = x_ref.shape
    ch = MOE_CHUNK

    @pl.when(grp == 0)
    def _():
        half = ma_ref.shape[1]
        x = x_ref[...] + _dot(ma_ref[...], wo_ref[:half, :]) + _dot(mb_ref[...], wo_ref[half:, :])
        h_hi, h_lo = _split2(_rms_rows(x, g_ref[...]))
        h_sc[...] = h_hi
        logits_t = (_dot_nt(wrh_ref[...], h_hi) + _dot_nt(wrh_ref[...], h_lo) + _dot_nt(wrl_ref[...], h_hi)
                    + rb_ref[...])
        comb, is_g = _moe_route(logits_t)
        member = jnp.concatenate([jnp.where(m, 1.0, 0.0) for m in is_g]
                                 + [jnp.zeros((16 - MOE_GROUPS, tm), F32)], axis=0)
        incl = _dot(member.astype(BF16), tri_ref[...])
        pos = sum(jnp.where(is_g[g], incl[g:g + 1, :] - 1.0, 0.0) for g in range(MOE_GROUPS))
        gid = sum(jnp.where(is_g[g], float(g), 0.0) for g in range(MOE_GROUPS))
        chunk = jnp.floor(pos * (1.0 / ch))
        rows = jnp.concatenate([gid, chunk, pos - ch * chunk, jnp.zeros((ROUTE_COMB - 3, tm), F32)] + comb
                               + [jnp.zeros((LANES - ROUTE_COMB - MOE_EXPERTS, tm), F32)], axis=0)
        rows_sc[...] = rows
        cols_sc[...] = rows.T
        cnt_sc[...] = jnp.broadcast_to(incl[:, tm - 1:tm], cnt_sc.shape)
        o_ref[...] = x

    lane = lax.broadcasted_iota(jnp.int32, (1, LANES), 1)
    grp_row = lax.broadcasted_iota(jnp.int32, cnt_sc.shape, 0) == grp
    n_grp = jnp.sum(jnp.where(grp_row, cnt_sc[...], 0.0), axis=0, keepdims=True)[0, 0].astype(jnp.int32)
    grp_f = grp.astype(F32)

    def run_chunk(c, size):
        rows = rows_sc[...]
        cols = cols_sc[...]
        in_chunk = (rows[ROUTE_GROUP:ROUTE_GROUP + 1, :] == grp_f) & (rows[ROUTE_CHUNK:ROUTE_CHUNK + 1, :] == c)
        slot = lax.broadcasted_iota(jnp.int32, (size, tm), 0).astype(F32)
        gather = jnp.where(in_chunk & (rows[ROUTE_SLOT:ROUTE_SLOT + 1, :] == slot), 1.0, 0.0).astype(BF16)
        in_chunk_t = (cols[:, ROUTE_GROUP:ROUTE_GROUP + 1] == grp_f) & (cols[:, ROUTE_CHUNK:ROUTE_CHUNK + 1] == c)
        slot_t = lax.broadcasted_iota(jnp.int32, (tm, size), 1).astype(F32)
        scatter = jnp.where(in_chunk_t & (cols[:, ROUTE_SLOT:ROUTE_SLOT + 1] == slot_t), 1.0, 0.0).astype(BF16)
        hs = _dot(gather, h_sc[...]).astype(BF16)
        table = sum(_dot(gather, part) for part in _split3(cols))
        y = jnp.zeros((size, d), F32)
        for j in range(MOE_EPG):
            cw = jnp.sum(jnp.where(lane == ROUTE_COMB + MOE_EPG * grp + j, table, 0.0), axis=1, keepdims=True)
            ab = _dot(hs, wgu_ref[j])
            a = ab[:, :MOE_HIDDEN]
            hid = (a * jax.nn.sigmoid(a)) * ab[:, MOE_HIDDEN:] * cw
            y = y + _dot(hid.astype(BF16), wd_ref[j])
        o_ref[...] += _dot(scatter, y.astype(BF16))

    for c in range(tm // ch):
        left = n_grp - c * ch
        pl.when(left > MOE_SMALL_CHUNK)(functools.partial(run_chunk, c, ch))
        pl.when((left > 0) & (left <= MOE_SMALL_CHUNK))(functools.partial(run_chunk, c, MOE_SMALL_CHUNK))


def _moe(x2d, mix_a, mix_b, w_out, g, wr_hi, wr_lo, rb, wgu, wd, tm=1024):
    t, d = x2d.shape
    half = mix_a.shape[1]
    tri = jnp.triu(jnp.ones((tm, tm), BF16))
    const = lambda shape: pl.BlockSpec(shape, lambda i, gi: (0,) * len(shape), pipeline_mode=pl.Buffered(1))
    return pl.pallas_call(
        _moe_kernel,
        grid=(t // tm, MOE_GROUPS),
        in_specs=[pl.BlockSpec((tm, d), lambda i, gi: (i, 0)),
                  pl.BlockSpec((tm, half), lambda i, gi: (i, 0)),
                  pl.BlockSpec((tm, half), lambda i, gi: (i, 0)),
                  const((2 * half, d)),
                  const((1, d)), const((MOE_ROUTER_ROWS, d)), const((MOE_ROUTER_ROWS, d)),
                  const((MOE_ROUTER_ROWS, 1)), const((tm, tm)),
                  pl.BlockSpec((MOE_EPG, d, 2 * MOE_HIDDEN), lambda i, gi: (gi, 0, 0)),
                  pl.BlockSpec((MOE_EPG, MOE_HIDDEN, d), lambda i, gi: (gi, 0, 0))],
        out_specs=pl.BlockSpec((tm, d), lambda i, gi: (i, 0)),
        out_shape=jax.ShapeDtypeStruct((t, d), F32),
        scratch_shapes=[pltpu.VMEM((tm, d), BF16), pltpu.VMEM((LANES, tm), F32), pltpu.VMEM((tm, LANES), F32),
                        pltpu.VMEM((16, LANES), F32)],
        compiler_params=_cparams(("parallel", "arbitrary"), 52),
        name="moe",
    )(x2d, mix_a, mix_b, w_out, g, wr_hi, wr_lo, rb, tri, wgu, wd)


def _ple_kernel(x_ref, p_ref, wp_ref, wg_ref, g1_ref, g2_ref, o_ref):
    x = x_ref[...]
    e = _dot(p_ref[...].astype(BF16), wp_ref[...])
    gate = jax.nn.sigmoid(_dot(_rms_rows(x, g1_ref[...]).astype(BF16), wg_ref[...]))
    o_ref[...] = x + _rms_rows(e * gate, g2_ref[...])


def _ple(x2d, p2d, wp, wg, g1, g2, tm=512):
    t, d = x2d.shape
    kp = p2d.shape[1]
    return pl.pallas_call(
        _ple_kernel,
        grid=(t // tm,),
        in_specs=[pl.BlockSpec((tm, d), lambda i: (i, 0)),
                  pl.BlockSpec((tm, kp), lambda i: (i, 0)),
                  pl.BlockSpec((kp, d), lambda i: (0, 0)),
                  pl.BlockSpec((d, d), lambda i: (0, 0)),
                  pl.BlockSpec((1, d), lambda i: (0, 0)),
                  pl.BlockSpec((1, d), lambda i: (0, 0))],
        out_specs=pl.BlockSpec((tm, d), lambda i: (i, 0)),
        out_shape=jax.ShapeDtypeStruct((t, d), F32),
        compiler_params=_cparams(("parallel",), 32),
        name="ple",
    )(x2d, p2d, wp, wg, g1, g2)


def _pad_lanes(cols, width=LANES):
    return jnp.pad(cols, ((0, 0), (0, width - cols.shape[-1])))


def _row(v):
    return v.reshape(1, -1).astype(F32)


def _even_mix(x2d, bsz, seq, norm_g, w_in, fox_b_f, fox_qn_g, fox_kn_g, conv_w, conv_b, b_i, b_f, mnorm_g):
    hw = HALF
    o = np.cumsum([0, hw, hw, hw, ATT_HEADS, hw, hw, hw, hw, MLSTM_HEADS, MLSTM_HEADS, hw])
    seg = lambda i: w_in[:, o[i]:o[i + 1]]
    fq, fk, fv, ff, fo, mq, mk, mv, mi, mf, mo = [seg(i) for i in range(11)]
    w = jnp.concatenate([fq, fk, fv, mv, fo, mq, mk, mo, _pad_lanes(jnp.concatenate([ff, mi, mf], axis=1))],
                        axis=1).astype(BF16)
    gain = jnp.concatenate([jnp.tile(fox_qn_g, ATT_HEADS) * ATT_Q_SCALE, jnp.tile(fox_kn_g, ATT_HEADS)])
    att, rest, gates = _in_proj(x2d, _row(norm_g), w, _row(gain), ((4 * hw, BF16), (4 * hw, F32), (LANES, F32)),
                                2 * hw)
    att = att.reshape(bsz, seq, 4 * hw)
    rest = rest.reshape(bsz, seq, 4 * hw)
    bias = _pad_lanes(_row(jnp.concatenate([fox_b_f, b_i, b_f])))
    gcol, rm = _prep_even(gates.reshape(bsz, seq, LANES), bias)
    out_a = _fox_attention(att, rest, gcol)
    out_b = _mlstm(att, rest, conv_w, _row(conv_b), gcol, rm, _row(mnorm_g))
    return out_a.reshape(-1, hw), out_b.reshape(-1, hw)


def _odd_mix(x2d, bsz, seq, norm_g, w_in, moba_qn_g, moba_kn_g, conv_w, conv_b, dt_bias, a_log, d_skip, snorm_g):
    hw = HALF
    nbc = SSD_GROUPS * SSD_STATE
    o = np.cumsum([0, hw, hw, hw, hw, hw, nbc, nbc, SSD_HEADS])
    w = jnp.concatenate([w_in[:, :o[7]], _pad_lanes(w_in[:, o[7]:o[8]])], axis=1).astype(BF16)
    gain = jnp.concatenate([jnp.tile(moba_qn_g, ATT_HEADS) * ATT_Q_SCALE, jnp.tile(moba_kn_g, ATT_HEADS)])
    moba_u, zx, bcin, dts = _in_proj(x2d, _row(norm_g), w, _row(gain),
                                     ((3 * hw, BF16), (2 * hw, F32), (2 * nbc, F32), (LANES, F32)), 2 * hw)
    gcol, rs = _prep_odd(dts.reshape(bsz, seq, LANES), _pad_lanes(_row(dt_bias)), _pad_lanes(_row(a_log)))
    out_c = _moba_attention(moba_u.reshape(bsz, seq, 3 * hw))
    out_d = _ssd(zx.reshape(bsz, seq, 2 * hw), bcin.reshape(bsz, seq, 2 * nbc), conv_w, _row(conv_b), gcol,
                 _ssd_expand_matrices(), rs, _row(jnp.repeat(d_skip, SSD_HEAD_DIM)), _row(snorm_g))
    return out_c.reshape(-1, hw), out_d.reshape(-1, hw)


def _moe_layer(x2d, mix_a, mix_b, w_out, norm_g, w_group, b_group, w_router, b_router, w_gate, w_up, w_down):
    d = x2d.shape[1]
    wr = _pad_lanes(jnp.concatenate([w_group, w_router], axis=1), MOE_ROUTER_ROWS).T
    wr_hi = wr.astype(BF16)
    wr_lo = (wr - wr_hi.astype(F32)).astype(BF16)
    rb = _pad_lanes(_row(jnp.concatenate([b_group, b_router])), MOE_ROUTER_ROWS).T
    wgu = jnp.concatenate([w_gate.reshape(MOE_EXPERTS, d, MOE_HIDDEN), w_up.reshape(MOE_EXPERTS, d, MOE_HIDDEN)],
                          axis=-1).astype(BF16)
    wd = w_down.reshape(MOE_EXPERTS, MOE_HIDDEN, d).astype(BF16)
    return _moe(x2d, mix_a, mix_b, w_out.astype(BF16), _row(norm_g), wr_hi, wr_lo, rb, wgu, wd)


def kernel(x, p, norm1_g, norm2_g, ev_w_in, ev_fox_b_f, ev_fox_qn_g, ev_fox_kn_g, ev_mlstm_conv_w, ev_mlstm_conv_b, ev_mlstm_b_i, ev_mlstm_b_f, ev_mlstm_norm_g, ev_w_out, od_w_in, od_moba_qn_g, od_moba_kn_g, od_ssd_conv_w, od_ssd_conv_b, od_ssd_dt_bias, od_ssd_A_log, od_ssd_D, od_ssd_norm_g, od_w_out, moe_w_group, moe_b_group, moe_w_router, moe_b_router, moe_w_gate, moe_w_up, moe_w_down, ple_w_proj, ple_w_gate, ple_gate_norm_g, ple_out_norm_g):
    bsz, seq, d = x.shape
    depth = p.shape[0]
    x2d = x.reshape(bsz * seq, d)
    for i in range(depth):
        j = i // 2
        if i % 2 == 0:
            mix_a, mix_b = _even_mix(x2d, bsz, seq, norm1_g[i], ev_w_in[j], ev_fox_b_f[j], ev_fox_qn_g[j],
                                     ev_fox_kn_g[j], ev_mlstm_conv_w[j], ev_mlstm_conv_b[j], ev_mlstm_b_i[j],
                                     ev_mlstm_b_f[j], ev_mlstm_norm_g[j])
            w_out = ev_w_out[j]
        else:
            mix_a, mix_b = _odd_mix(x2d, bsz, seq, norm1_g[i], od_w_in[j], od_moba_qn_g[j], od_moba_kn_g[j],
                                    od_ssd_conv_w[j], od_ssd_conv_b[j], od_ssd_dt_bias[j], od_ssd_A_log[j],
                                    od_ssd_D[j], od_ssd_norm_g[j])
            w_out = od_w_out[j]
        x2d = _moe_layer(x2d, mix_a, mix_b, w_out, norm2_g[i], moe_w_group[i], moe_b_group[i], moe_w_router[i],
                         moe_b_router[i], moe_w_gate[i], moe_w_up[i], moe_w_down[i])
        x2d = _ple(x2d, p[i].reshape(bsz * seq, -1), ple_w_proj[i].astype(BF16), ple_w_gate[i].astype(BF16),
                   _row(ple_gate_norm_g[i]), _row(ple_out_norm_g[i]))
    return x2d.reshape(bsz, seq, d)
```

```python
import functools

import jax
import jax.numpy as jnp
import numpy as np
from jax import lax
from jax.experimental import pallas as pl
from jax.experimental.pallas import tpu as pltpu

F32 = jnp.float32
BF16 = jnp.bfloat16
MIX_DTYPE = BF16
NEG_INF = float("-inf")
LOG2E = 1.4426950408889634

NORM_EPS = 1e-6
D_MODEL = 1024
HALF = D_MODEL // 2
ATT_HEAD_DIM = 64
ATT_HEADS = HALF // ATT_HEAD_DIM
ATT_PAIRS = ATT_HEADS // 2
ATT_Q_SCALE = ATT_HEAD_DIM ** -0.5 * LOG2E
MLSTM_HEAD_DIM = 128
MLSTM_HEADS = HALF // MLSTM_HEAD_DIM
SSD_HEAD_DIM = 64
SSD_HEADS = HALF // SSD_HEAD_DIM
SSD_GROUPS = 2
SSD_STATE = 128
SSD_GROUP_W = HALF // SSD_GROUPS
SSD_HEADS_PER_GROUP = SSD_HEADS // SSD_GROUPS
CONV_WIDTH = 4
MOBA_BLOCK = 256
MOBA_TOPK = 3
MOE_GROUPS = 4
MOE_EPG = 4
MOE_EXPERTS = MOE_GROUPS * MOE_EPG
MOE_HIDDEN = D_MODEL // 4
MOE_ROUTER_ROWS = 32
MOE_CHUNK = 256
MOE_SMALL_CHUNK = 64
PLE_DIM = 256

LANES = 128
MXU_N = 256
SEQ_TILE = 256
ATT_TILE = 512
MIB = 1024 * 1024


def _cparams(sem, vmem_mib):
    return pltpu.CompilerParams(dimension_semantics=sem, vmem_limit_bytes=vmem_mib * MIB)


def _split2(x):
    hi = x.astype(BF16)
    lo = (x - hi.astype(F32)).astype(BF16)
    return hi, lo


def _split3(x):
    hi = x.astype(BF16)
    r = x - hi.astype(F32)
    mid = r.astype(BF16)
    lo = (r - mid.astype(F32)).astype(BF16)
    return hi, mid, lo


def _dot(a, b):
    return jnp.dot(a, b, preferred_element_type=F32)


def _dot_nt(a, b):
    return lax.dot_general(a, b, (((1,), (1,)), ((), ())), preferred_element_type=F32)


def _log_sigmoid(x):
    return jnp.minimum(x, 0.0) - jnp.log1p(jnp.exp(-jnp.abs(x)))


def _softplus(x):
    return jnp.maximum(x, 0.0) + jnp.log1p(jnp.exp(-jnp.abs(x)))


def _rms_rows(xf, g, eps=NORM_EPS):
    ms = jnp.mean(xf * xf, axis=-1, keepdims=True)
    return xf * lax.rsqrt(ms + eps) * g


def _block_index(i, block):
    shift = block.bit_length() - 1
    assert 1 << shift == block
    return lax.shift_right_logical(i, shift)


def _lane_col(x, idx):
    lane = lax.broadcasted_iota(jnp.int32, (1, x.shape[1]), 1)
    return jnp.sum(jnp.where(lane == idx, x, 0.0), axis=1, keepdims=True)


def _in_proj_kernel(x_ref, g_ref, w_ref, gain_ref, *refs, widths, n_norm, head_dim):
    out_refs, raw_sc = refs[:-1], refs[-1]
    h = _rms_rows(x_ref[...], g_ref[...]).astype(BF16)
    off = 0
    for o_ref, wdt in zip(out_refs, widths):
        for c0 in range(0, wdt, MXU_N):
            n = min(MXU_N, wdt - c0)
            y = _dot(h, w_ref[:, off + c0: off + c0 + n])
            if off + c0 < n_norm:
                raw_sc[:, off + c0: off + c0 + n] = y
            else:
                o_ref[:, c0:c0 + n] = y.astype(o_ref.dtype)
        off += wdt
    r = _block_index(lax.broadcasted_iota(jnp.int32, (MXU_N, MXU_N), 0), head_dim)
    c = _block_index(lax.broadcasted_iota(jnp.int32, (MXU_N, MXU_N), 1), head_dim)
    seg = jnp.where(r == c, 1.0, 0.0).astype(BF16)
    assert n_norm <= widths[0] and n_norm % MXU_N == 0
    for c0 in range(0, n_norm, MXU_N):
        y = raw_sc[:, c0:c0 + MXU_N]
        ss = _dot((y * y).astype(BF16), seg)
        y = y * lax.rsqrt(ss * (1.0 / head_dim) + NORM_EPS) * gain_ref[:, c0:c0 + MXU_N]
        out_refs[0][:, c0:c0 + MXU_N] = y.astype(out_refs[0].dtype)


def _in_proj(x2d, g, w, gain, segments, n_norm, tm=512):
    t, d = x2d.shape
    n = w.shape[1]
    widths = tuple(wd for wd, _ in segments)
    kern = functools.partial(_in_proj_kernel, widths=widths, n_norm=n_norm, head_dim=ATT_HEAD_DIM)
    return pl.pallas_call(
        kern,
        grid=(t // tm,),
        in_specs=[
            pl.BlockSpec((tm, d), lambda i: (i, 0)),
            pl.BlockSpec((1, d), lambda i: (0, 0), pipeline_mode=pl.Buffered(1)),
            pl.BlockSpec((d, n), lambda i: (0, 0), pipeline_mode=pl.Buffered(1)),
            pl.BlockSpec((1, n_norm), lambda i: (0, 0), pipeline_mode=pl.Buffered(1)),
        ],
        out_specs=[pl.BlockSpec((tm, wd), lambda i: (i, 0)) for wd in widths],
        out_shape=[jax.ShapeDtypeStruct((t, wd), dt) for wd, dt in segments],
        scratch_shapes=[pltpu.VMEM((tm, n_norm), F32)],
        compiler_params=_cparams(("parallel",), 52),
        name="in_proj",
    )(x2d, g, w, gain)


def _tri_cumsum(vals, tri):
    hi, mid, lo = _split3(vals)
    return _dot(tri, hi) + _dot(tri, mid) + _dot(tri, lo)


def _lower_tri(n):
    r = lax.broadcasted_iota(jnp.int32, (n, n), 0)
    c = lax.broadcasted_iota(jnp.int32, (n, n), 1)
    return jnp.where(c <= r, 1.0, 0.0).astype(BF16)


def _prep_even_kernel(raw_ref, bias_ref, g_ref, rm_ref):
    s = raw_ref.shape[0]
    ch = SEQ_TILE
    lane = lax.broadcasted_iota(jnp.int32, (1, LANES), 1)
    is_cum = (lane < ATT_HEADS) | ((lane >= ATT_HEADS + MLSTM_HEADS) & (lane < ATT_HEADS + 2 * MLSTM_HEADS))
    tri = _lower_tri(ch)
    carry = jnp.zeros((1, LANES), F32)
    pad = jnp.zeros((6, ch), F32)
    for c in range(s // ch):
        z = raw_ref[c * ch:(c + 1) * ch, :] + bias_ref[...]
        cs = _tri_cumsum(jnp.where(is_cum, _log_sigmoid(z), 0.0), tri) + carry
        carry = cs[ch - 1:ch, :]
        g = jnp.where(is_cum, cs, z)
        g_ref[c * ch:(c + 1) * ch, :] = g
        gt = g.T
        for hh in range(MLSTM_HEADS):
            i_row = gt[ATT_HEADS + hh:ATT_HEADS + hh + 1]
            f_row = gt[ATT_HEADS + MLSTM_HEADS + hh:ATT_HEADS + MLSTM_HEADS + hh + 1]
            rm_ref[hh, :, c * ch:(c + 1) * ch] = jnp.concatenate([i_row - f_row, f_row, pad], axis=0)


def _prep_even(raw, bias):
    b, s, _ = raw.shape
    return pl.pallas_call(
        _prep_even_kernel,
        grid=(b,),
        in_specs=[pl.BlockSpec((None, s, LANES), lambda i: (i, 0, 0)),
                  pl.BlockSpec((1, LANES), lambda i: (0, 0))],
        out_specs=[pl.BlockSpec((None, s, LANES), lambda i: (i, 0, 0)),
                   pl.BlockSpec((None, MLSTM_HEADS, 8, s), lambda i: (i, 0, 0, 0))],
        out_shape=[jax.ShapeDtypeStruct((b, s, LANES), F32),
                   jax.ShapeDtypeStruct((b, MLSTM_HEADS, 8, s), F32)],
        compiler_params=_cparams(("parallel",), 32),
        name="prep_even",
    )(raw, bias)


def _prep_odd_kernel(raw_ref, bias_ref, alog_ref, g_ref, rs_ref):
    s = raw_ref.shape[0]
    ch = SEQ_TILE
    lane = lax.broadcasted_iota(jnp.int32, (1, LANES), 1)
    is_dt = lane < SSD_HEADS
    tri = _lower_tri(ch)
    a_row = jnp.where(is_dt, -jnp.exp(alog_ref[...]), 0.0)
    pad = jnp.zeros((8 - SSD_HEADS_PER_GROUP, ch), F32)
    for c in range(s // ch):
        dt = jnp.where(is_dt, _softplus(raw_ref[c * ch:(c + 1) * ch, :] + bias_ref[...]), 0.0)
        acum = _tri_cumsum(a_row * dt, tri)
        g = jnp.where(is_dt, dt, pltpu.roll(acum, SSD_HEADS, 1))
        g_ref[c * ch:(c + 1) * ch, :] = g
        gt = g.T
        for gi in range(SSD_GROUPS):
            lo = SSD_HEADS + gi * SSD_HEADS_PER_GROUP
            rs_ref[gi, :, c * ch:(c + 1) * ch] = jnp.concatenate([gt[lo:lo + SSD_HEADS_PER_GROUP], pad], axis=0)


def _prep_odd(raw, bias, alog):
    b, s, _ = raw.shape
    return pl.pallas_call(
        _prep_odd_kernel,
        grid=(b,),
        in_specs=[pl.BlockSpec((None, s, LANES), lambda i: (i, 0, 0)),
                  pl.BlockSpec((1, LANES), lambda i: (0, 0)),
                  pl.BlockSpec((1, LANES), lambda i: (0, 0))],
        out_specs=[pl.BlockSpec((None, s, LANES), lambda i: (i, 0, 0)),
                   pl.BlockSpec((None, SSD_GROUPS, 8, s), lambda i: (i, 0, 0, 0))],
        out_shape=[jax.ShapeDtypeStruct((b, s, LANES), F32),
                   jax.ShapeDtypeStruct((b, SSD_GROUPS, 8, s), F32)],
        compiler_params=_cparams(("parallel",), 32),
        name="prep_odd",
    )(raw, bias, alog)


def _two_pass_softmax_pv(n_past, past_span, past_logits, own_span, own_logits, v_rows, s_scr, tq):
    heads = range(2)
    neg = jnp.full((1, tq), NEG_INF, F32)

    def store_max(span, s_t, hh, m):
        s_scr[hh, pl.ds(*span), :] = s_t
        return jnp.maximum(m, jnp.max(s_t, axis=0, keepdims=True))

    def accumulate(span, hh, m, carry):
        l, acc = carry
        p = jnp.exp2(s_scr[hh, pl.ds(*span), :] - m)
        return l + jnp.sum(p, axis=0, keepdims=True), acc + _dot(v_rows(*span, hh), p.astype(BF16))

    ms = lax.fori_loop(
        0, n_past, lambda j, ms: tuple(store_max(past_span(j), past_logits(j, hh), hh, ms[hh]) for hh in heads),
        (neg, neg))
    ms = [store_max(own_span, own_logits(hh), hh, ms[hh]) for hh in heads]
    zero = (jnp.zeros((1, tq), F32), jnp.zeros((ATT_HEAD_DIM, tq), F32))
    carry = tuple(accumulate(own_span, hh, ms[hh], zero) for hh in heads)
    carry = lax.fori_loop(
        0, n_past, lambda j, c: tuple(accumulate(past_span(j), hh, ms[hh], c[hh]) for hh in heads), carry)
    return jnp.concatenate([acc / l for l, acc in carry], axis=0).T


def _stage_v(v_ref, vt):
    for c in range(v_ref.shape[0] // SEQ_TILE):
        vt[:, c * SEQ_TILE:(c + 1) * SEQ_TILE] = v_ref[c * SEQ_TILE:(c + 1) * SEQ_TILE, :].astype(F32).T.astype(BF16)


def _head_queries(q2):
    lane = lax.broadcasted_iota(jnp.int32, (1, LANES), 1)
    return [jnp.where(lane < ATT_HEAD_DIM, q2, 0.0), jnp.where(lane >= ATT_HEAD_DIM, q2, 0.0)]


def _fox_kernel(q_ref, k_ref, v_ref, fo_ref, g_ref, o_ref, vt, cfb, s_scr):
    hp = pl.program_id(1)
    qi = pl.program_id(2)
    tq = q_ref.shape[0]
    s_len = k_ref.shape[0]
    hd = ATT_HEAD_DIM

    @pl.when(qi == 0)
    def _():
        _stage_v(v_ref, vt)
        g = g_ref[...]
        for hh in range(2):
            cfb[hh] = jnp.broadcast_to(_lane_col(g, 2 * hp + hh) * LOG2E, (s_len, LANES))

    qh = _head_queries(q_ref[...])

    def logits(j, hh):
        st = pl.multiple_of(j * tq, tq)
        s_t = _dot_nt(k_ref[pl.ds(st, tq), :], qh[hh])
        return s_t - jnp.concatenate([cfb[hh, pl.ds(st, tq), :]] * (tq // LANES), axis=1)

    def span(j):
        return pl.multiple_of(j * tq, tq), tq

    krow = lax.broadcasted_iota(jnp.int32, (tq, tq), 0)
    qcol = lax.broadcasted_iota(jnp.int32, (tq, tq), 1)
    out = _two_pass_softmax_pv(
        qi, span, logits, span(qi), lambda hh: jnp.where(krow <= qcol, logits(qi, hh), NEG_INF),
        lambda st, size, hh: vt[hh * hd:(hh + 1) * hd, pl.ds(st, size)], s_scr, tq)
    o_ref[...] = (out * jax.nn.sigmoid(fo_ref[...])).astype(o_ref.dtype)


def _fox_attention(qkv, ogate, gcol, tq=ATT_TILE):
    b, s, _ = qkv.shape
    np_ = ATT_PAIRS
    return pl.pallas_call(
        _fox_kernel,
        grid=(b, np_, s // tq),
        in_specs=[
            pl.BlockSpec((None, tq, LANES), lambda bi, hp, qi: (bi, qi, hp)),
            pl.BlockSpec((None, s, LANES), lambda bi, hp, qi: (bi, 0, np_ + hp)),
            pl.BlockSpec((None, s, LANES), lambda bi, hp, qi: (bi, 0, 2 * np_ + hp)),
            pl.BlockSpec((None, tq, LANES), lambda bi, hp, qi: (bi, qi, hp)),
            pl.BlockSpec((None, s, LANES), lambda bi, hp, qi: (bi, 0, 0)),
        ],
        out_specs=pl.BlockSpec((None, tq, LANES), lambda bi, hp, qi: (bi, qi, hp)),
        out_shape=jax.ShapeDtypeStruct((b, s, HALF), MIX_DTYPE),
        scratch_shapes=[pltpu.VMEM((LANES, s), BF16), pltpu.VMEM((2, s, LANES), F32),
                        pltpu.VMEM((2, s, tq), F32)],
        compiler_params=_cparams(("parallel", "parallel", "arbitrary"), 40),
        name="fox_attention",
    )(qkv, qkv, qkv, ogate, gcol)


def _moba_kernel(q_ref, k_ref, v_ref, o_ref, vt, kmh, kml, bias, s_scr):
    qi = pl.program_id(2)
    tq = q_ref.shape[0]
    s_len = k_ref.shape[0]
    blk = MOBA_BLOCK
    nb = s_len // blk
    nbp = kmh.shape[0]
    hd = ATT_HEAD_DIM
    per_tile = tq // blk

    @pl.when(qi == 0)
    def _():
        _stage_v(v_ref, vt)
        means = [jnp.mean(k_ref[n * blk:(n + 1) * blk, :].astype(F32), axis=0, keepdims=True) for n in range(nb)]
        km = jnp.concatenate(means + [jnp.zeros((nbp - nb, LANES), F32)], axis=0)
        kmh[...], kml[...] = _split2(km)

    qcol1 = lax.broadcasted_iota(jnp.int32, (1, tq), 1)
    q_blk = per_tile * qi + _block_index(qcol1, blk)
    nrow = lax.broadcasted_iota(jnp.int32, (nbp, tq), 0)
    valid = nrow < q_blk
    qh = _head_queries(q_ref[...])
    for hh in range(2):
        gate = _dot_nt(kmh[...], qh[hh]) + _dot_nt(kml[...], qh[hh])
        gate = jnp.where(valid, gate, NEG_INF)
        rank = jnp.zeros((nbp, tq), F32)
        for m in range(nb):
            gm = gate[m:m + 1, :]
            rank = rank + jnp.where((gm > gate) | ((gm == gate) & (nrow > m)), 1.0, 0.0)
        selb = jnp.where((rank < float(MOBA_TOPK)) & valid, 0.0, NEG_INF)
        for n in range(nb):
            bias[hh, n] = jnp.broadcast_to(selb[n:n + 1, :], (8, tq))

    def v_rows(st, size, hh):
        return vt[hh * hd:(hh + 1) * hd, pl.ds(st, size)]

    st = pl.multiple_of(qi * tq, tq)
    krow = lax.broadcasted_iota(jnp.int32, (tq, tq), 0)
    qcol = lax.broadcasted_iota(jnp.int32, (tq, tq), 1)
    same_blk = _block_index(krow, blk) == _block_index(qcol, blk)

    def sel_rows(j, hh):
        return jnp.concatenate(
            [jnp.broadcast_to(bias[hh, per_tile * j + n][0:1, :], (blk, tq)) for n in range(per_tile)], axis=0)

    def own_logits(hh):
        s_t = _dot_nt(k_ref[pl.ds(st, tq), :], qh[hh])
        return s_t + jnp.where(krow <= qcol, jnp.where(same_blk, 0.0, sel_rows(qi, hh)), NEG_INF)

    def span(j):
        return pl.multiple_of(j * tq, tq), tq

    def past_logits(j, hh):
        return _dot_nt(k_ref[pl.ds(*span(j)), :], qh[hh]) + sel_rows(j, hh)

    out = _two_pass_softmax_pv(qi, span, past_logits, (st, tq), own_logits, v_rows, s_scr, tq)
    o_ref[...] = out.astype(o_ref.dtype)


def _moba_attention(qkv, tq=ATT_TILE):
    b, s, _ = qkv.shape
    np_ = ATT_PAIRS
    nb = s // MOBA_BLOCK
    nbp = -(-nb // 16) * 16
    return pl.pallas_call(
        _moba_kernel,
        grid=(b, np_, s // tq),
        in_specs=[
            pl.BlockSpec((None, tq, LANES), lambda bi, hp, qi: (bi, qi, hp)),
            pl.BlockSpec((None, s, LANES), lambda bi, hp, qi: (bi, 0, np_ + hp)),
            pl.BlockSpec((None, s, LANES), lambda bi, hp, qi: (bi, 0, 2 * np_ + hp)),
        ],
        out_specs=pl.BlockSpec((None, tq, LANES), lambda bi, hp, qi: (bi, qi, hp)),
        out_shape=jax.ShapeDtypeStruct((b, s, HALF), MIX_DTYPE),
        scratch_shapes=[pltpu.VMEM((LANES, s), BF16),
                        pltpu.VMEM((nbp, LANES), BF16), pltpu.VMEM((nbp, LANES), BF16),
                        pltpu.VMEM((2, nb, 8, tq), F32), pltpu.VMEM((2, s, tq), F32)],
        compiler_params=_cparams(("parallel", "parallel", "arbitrary"), 40),
        name="moba_attention",
    )(qkv, qkv, qkv)


def _conv_silu(x_ref, w_ref, b_ref, o_ref, scale):
    s_len, ch = x_ref.shape
    lc = SEQ_TILE
    w = w_ref[...]
    b = b_ref[...]
    row = lax.broadcasted_iota(jnp.int32, (lc, ch), 0)

    def body(c, _):
        st = pl.multiple_of(c * lc, lc)
        pst = pl.multiple_of(jnp.maximum(c - 1, 0) * lc, lc)
        cur = x_ref[pl.ds(st, lc), :]
        prev = jnp.where(c > 0, x_ref[pl.ds(pst, lc), :], 0.0)
        y = b
        for j in range(CONV_WIDTH - 1):
            sh = CONV_WIDTH - 1 - j
            shifted = jnp.where(row < sh, pltpu.roll(prev, sh, 0), pltpu.roll(cur, sh, 0))
            y = y + shifted * w[j:j + 1]
        y = y + cur * w[CONV_WIDTH - 1:CONV_WIDTH]
        y = y * jax.nn.sigmoid(y)
        o_ref[pl.ds(st, lc), :] = y * scale if scale != 1.0 else y
        return 0

    lax.fori_loop(0, s_len // lc, body, 0)


def _mlstm_kernel(q_ref, k_ref, v_ref, og_ref, cwq_ref, cwk_ref, cbq_ref, cbk_ref, g_ref, r_ref, ng_ref,
                  o_ref, qc, kc, cst):
    hh = pl.program_id(1)
    s_len, d = q_ref.shape
    ln = SEQ_TILE
    _conv_silu(q_ref, cwq_ref, cbq_ref, qc, 1.0)
    _conv_silu(k_ref, cwk_ref, cbk_ref, kc, d ** -0.5)
    cst[...] = jnp.zeros_like(cst)
    row = lax.broadcasted_iota(jnp.int32, (ln, ln), 0)
    col = lax.broadcasted_iota(jnp.int32, (ln, ln), 1)
    tri = col <= row
    lane = lax.broadcasted_iota(jnp.int32, (1, LANES), 1)
    ones_blk = jnp.broadcast_to(jnp.where(lane == 0, 1.0, 0.0), (ln, LANES)).astype(BF16)
    f_lane = ATT_HEADS + MLSTM_HEADS + hh

    def step(c, m_prev):
        st = pl.multiple_of(c * ln, ln)
        q = qc[pl.ds(st, ln), :]
        k = kc[pl.ds(st, ln), :]
        v = v_ref[pl.ds(st, ln), :]
        f_col = _lane_col(g_ref[pl.ds(st, ln), :], f_lane)
        a_row = r_ref[0:1, pl.ds(st, ln)]
        a_mat = jnp.where(tri, a_row, NEG_INF)
        m_col = jnp.maximum(m_prev, jnp.max(a_mat, axis=1, keepdims=True))
        qb = q.astype(BF16)
        w_qk = _dot_nt(qb, k.astype(BF16)) * jnp.exp(a_mat - m_col)
        v_aug = jnp.concatenate([v, ones_blk], axis=1)
        c_prev = cst[...]
        inter = jnp.exp(m_prev - m_col)
        q_c = _dot(qb, c_prev.astype(BF16))
        num = _dot(w_qk.astype(BF16), v_aug[:, :d]) + inter * q_c[:, :d]
        den = jnp.sum(w_qk, axis=1, keepdims=True) + inter * q_c[:, d:d + 1]
        h = num / jnp.maximum(jnp.abs(den), jnp.exp(-f_col - m_col))
        m_end = jnp.maximum(m_prev, jnp.max(a_row, axis=1, keepdims=True))
        k_w = (k.T * jnp.exp(a_row - m_end)).astype(BF16)
        cst[...] = jnp.exp(m_prev - m_end) * c_prev + _dot(k_w, v_aug)
        hn = _rms_rows(h, ng_ref[...])
        o_ref[pl.ds(st, ln), :] = (hn * jax.nn.sigmoid(og_ref[pl.ds(st, ln), :])).astype(o_ref.dtype)
        return m_end

    lax.fori_loop(0, s_len // ln, step, jnp.zeros((1, 1), F32))


def _mlstm(att, rest, conv_w, conv_b, gcol, grow, norm_g):
    b, s, _ = att.shape
    nh = MLSTM_HEADS
    d = MLSTM_HEAD_DIM
    big = lambda off: pl.BlockSpec((None, s, d), lambda bi, h: (bi, 0, off + h))
    return pl.pallas_call(
        _mlstm_kernel,
        grid=(b, nh),
        in_specs=[
            big(nh), big(2 * nh), big(3 * nh), big(3 * nh),
            pl.BlockSpec((CONV_WIDTH, d), lambda bi, h: (0, h)),
            pl.BlockSpec((CONV_WIDTH, d), lambda bi, h: (0, nh + h)),
            pl.BlockSpec((1, d), lambda bi, h: (0, h)),
            pl.BlockSpec((1, d), lambda bi, h: (0, nh + h)),
            pl.BlockSpec((None, s, LANES), lambda bi, h: (bi, 0, 0)),
            pl.BlockSpec((None, None, 8, s), lambda bi, h: (bi, h, 0, 0)),
            pl.BlockSpec((1, d), lambda bi, h: (0, h)),
        ],
        out_specs=pl.BlockSpec((None, s, d), lambda bi, h: (bi, 0, h)),
        out_shape=jax.ShapeDtypeStruct((b, s, HALF), MIX_DTYPE),
        scratch_shapes=[pltpu.VMEM((s, d), F32), pltpu.VMEM((s, d), F32), pltpu.VMEM((d, 2 * d), F32)],
        compiler_params=_cparams(("parallel", "parallel"), 40),
        name="mlstm",
    )(rest, rest, att, rest, conv_w, conv_w, conv_b, conv_b, gcol, grow, norm_g)


def _ssd_kernel(z_ref, x_ref, b_ref, c_ref, cwx_ref, cwb_ref, cwc_ref, cbx_ref, cbb_ref, cbc_ref,
                g_ref, e_ref, r_ref, dsk_ref, ng_ref, o_ref, xc, bc, cc, sst):
    s_len, gw = x_ref.shape
    ln = SEQ_TILE
    _conv_silu(x_ref, cwx_ref, cbx_ref, xc, 1.0)
    _conv_silu(b_ref, cwb_ref, cbb_ref, bc, 1.0)
    _conv_silu(c_ref, cwc_ref, cbc_ref, cc, 1.0)
    sst[...] = jnp.zeros_like(sst)
    row = lax.broadcasted_iota(jnp.int32, (ln, ln), 0)
    col = lax.broadcasted_iota(jnp.int32, (ln, ln), 1)
    tri = col <= row
    head_of_lane = _block_index(lax.broadcasted_iota(jnp.int32, (1, gw), 1), SSD_HEAD_DIM)

    def step(c, _):
        st = pl.multiple_of(c * ln, ln)
        x = xc[pl.ds(st, ln), :]
        bm = bc[pl.ds(st, ln), :]
        cm = cc[pl.ds(st, ln), :]
        z = z_ref[pl.ds(st, ln), :]
        parts = _split3(g_ref[pl.ds(st, ln), :])
        dt_e = sum(_dot(p, e_ref[0]) for p in parts)
        ac_e = sum(_dot(p, e_ref[1]) for p in parts)
        xdt = x * dt_e
        xdt_b = xdt.astype(BF16)
        cmb = cm.astype(BF16)
        cb = _dot_nt(cmb, bm.astype(BF16))
        ac_rows = r_ref[0:SSD_HEADS_PER_GROUP, pl.ds(st, ln)]
        y = jnp.zeros((ln, gw), F32)
        for hh in range(SSD_HEADS_PER_GROUP):
            ac_col = ac_e[:, hh * SSD_HEAD_DIM:hh * SSD_HEAD_DIM + 1]
            l_mat = jnp.exp(jnp.where(tri, ac_col - ac_rows[hh:hh + 1], NEG_INF))
            y = jnp.where(head_of_lane == hh, _dot((cb * l_mat).astype(BF16), xdt_b), y)
        ac_end = ac_e[ln - 1:ln, :]
        prev = sst[...]
        y = y + _dot(cmb, prev.astype(BF16)) * jnp.exp(ac_e)
        states = _dot(bm.T.astype(BF16), (xdt * jnp.exp(ac_end - ac_e)).astype(BF16))
        sst[...] = prev * jnp.exp(ac_end) + states
        y = y + dsk_ref[...] * x
        y = y * (z * jax.nn.sigmoid(z))
        o_ref[pl.ds(st, ln), :] = _rms_rows(y, ng_ref[...]).astype(o_ref.dtype)
        return 0

    lax.fori_loop(0, s_len // ln, step, 0)


def _ssd(zx, bcin, conv_w, conv_b, gcol, expand, grow, d_row, norm_g):
    b, s, _ = zx.shape
    gw = SSD_GROUP_W
    ns = SSD_STATE
    ng = SSD_GROUPS
    xoff = HALF // gw
    boff = HALF // ns
    coff = (HALF + ng * ns) // ns
    return pl.pallas_call(
        _ssd_kernel,
        grid=(b, ng),
        in_specs=[
            pl.BlockSpec((None, s, gw), lambda bi, g: (bi, 0, g)),
            pl.BlockSpec((None, s, gw), lambda bi, g: (bi, 0, xoff + g)),
            pl.BlockSpec((None, s, ns), lambda bi, g: (bi, 0, g)),
            pl.BlockSpec((None, s, ns), lambda bi, g: (bi, 0, ng + g)),
            pl.BlockSpec((CONV_WIDTH, gw), lambda bi, g: (0, g)),
            pl.BlockSpec((CONV_WIDTH, ns), lambda bi, g: (0, boff + g)),
            pl.BlockSpec((CONV_WIDTH, ns), lambda bi, g: (0, coff + g)),
            pl.BlockSpec((1, gw), lambda bi, g: (0, g)),
            pl.BlockSpec((1, ns), lambda bi, g: (0, boff + g)),
            pl.BlockSpec((1, ns), lambda bi, g: (0, coff + g)),
            pl.BlockSpec((None, s, LANES), lambda bi, g: (bi, 0, 0)),
            pl.BlockSpec((None, 2, LANES, gw), lambda bi, g: (g, 0, 0, 0)),
            pl.BlockSpec((None, None, 8, s), lambda bi, g: (bi, g, 0, 0)),
            pl.BlockSpec((1, gw), lambda bi, g: (0, g)),
            pl.BlockSpec((1, gw), lambda bi, g: (0, g)),
        ],
        out_specs=pl.BlockSpec((None, s, gw), lambda bi, g: (bi, 0, g)),
        out_shape=jax.ShapeDtypeStruct((b, s, HALF), MIX_DTYPE),
        scratch_shapes=[pltpu.VMEM((s, gw), F32), pltpu.VMEM((s, ns), F32), pltpu.VMEM((s, ns), F32),
                        pltpu.VMEM((ns, gw), F32)],
        compiler_params=_cparams(("parallel", "parallel"), 48),
        name="ssd",
    )(zx, zx, bcin, bcin, conv_w, conv_w, conv_w, conv_b, conv_b, conv_b, gcol, expand, grow, d_row, norm_g)


def _ssd_expand_matrices():
    e = np.zeros((SSD_GROUPS, 2, LANES, SSD_GROUP_W), np.float32)
    for g in range(SSD_GROUPS):
        for h in range(SSD_HEADS_PER_GROUP):
            head = g * SSD_HEADS_PER_GROUP + h
            e[g, 0, head, h * SSD_HEAD_DIM:(h + 1) * SSD_HEAD_DIM] = 1.0
            e[g, 1, SSD_HEADS + head, h * SSD_HEAD_DIM:(h + 1) * SSD_HEAD_DIM] = 1.0
    return jnp.asarray(e, BF16)


def _moe_route(logits_t):
    gl = [logits_t[g:g + 1, :] for g in range(MOE_GROUPS)]
    g_max = functools.reduce(jnp.maximum, gl)
    g_den = sum(jnp.exp(x - g_max) for x in gl)
    g_w = 1.0 / g_den
    taken = jnp.zeros_like(g_max) > 1.0
    is_g = []
    for g in range(MOE_GROUPS):
        hit = (gl[g] == g_max) & jnp.logical_not(taken)
        is_g.append(hit)
        taken = taken | hit
    e_in = []
    for j in range(MOE_EPG):
        v = jnp.zeros_like(g_max)
        for g in range(MOE_GROUPS):
            row = MOE_GROUPS + g * MOE_EPG + j
            v = jnp.where(is_g[g], logits_t[row:row + 1, :], v)
        e_in.append(v)
    rank = []
    for j in range(MOE_EPG):
        r = jnp.zeros_like(g_max)
        for m in range(MOE_EPG):
            if m == j:
                continue
            ahead = (e_in[m] > e_in[j]) | ((e_in[m] == e_in[j]) & (m < j))
            r = r + jnp.where(ahead, 1.0, 0.0)
        rank.append(r)
    v0 = sum(jnp.where(rank[j] == 0.0, e_in[j], 0.0) for j in range(MOE_EPG))
    v1 = sum(jnp.where(rank[j] == 1.0, e_in[j], 0.0) for j in range(MOE_EPG))
    e1 = jnp.exp(v1 - v0)
    w0 = 1.0 / (1.0 + e1)
    w1 = e1 / (1.0 + e1)
    comb = []
    for g in range(MOE_GROUPS):
        for j in range(MOE_EPG):
            wj = jnp.where(rank[j] == 0.0, w0, jnp.where(rank[j] == 1.0, w1, 0.0))
            comb.append(jnp.where(is_g[g], g_w * wj, 0.0))
    return comb, is_g


ROUTE_GROUP, ROUTE_CHUNK, ROUTE_SLOT, ROUTE_COMB = 0, 1, 2, 8


def _moe_kernel(x_ref, ma_ref, mb_ref, wo_ref, g_ref, wrh_ref, wrl_ref, rb_ref, tri_ref, wgu_ref, wd_ref,
                p_ref, wp_ref, wg_ref, g1_ref, g2_ref, o_ref, h_sc, rows_sc, cols_sc, cnt_sc):
    grp = pl.program_id(1)
    tm, d = x_ref.shape
    ch = MOE_CHUNK

    @pl.when(grp == 0)
    def _():
        half = ma_ref.shape[1]
        x = x_ref[...] + _dot(ma_ref[...], wo_ref[:half, :]) + _dot(mb_ref[...], wo_ref[half:, :])
        h_hi, h_lo = _split2(_rms_rows(x, g_ref[...]))
        h_sc[...] = h_hi
        logits_t = (_dot_nt(wrh_ref[...], h_hi) + _dot_nt(wrh_ref[...], h_lo) + _dot_nt(wrl_ref[...], h_hi)
                    + rb_ref[...])
        comb, is_g = _moe_route(logits_t)
        member = jnp.concatenate([jnp.where(m, 1.0, 0.0) for m in is_g]
                                 + [jnp.zeros((16 - MOE_GROUPS, tm), F32)], axis=0)
        incl = _dot(member.astype(BF16), tri_ref[...])
        pos = sum(jnp.where(is_g[g], incl[g:g + 1, :] - 1.0, 0.0) for g in range(MOE_GROUPS))
        gid = sum(jnp.where(is_g[g], float(g), 0.0) for g in range(MOE_GROUPS))
        chunk = jnp.floor(pos * (1.0 / ch))
        rows = jnp.concatenate([gid, chunk, pos - ch * chunk, jnp.zeros((ROUTE_COMB - 3, tm), F32)] + comb
                               + [jnp.zeros((LANES - ROUTE_COMB - MOE_EXPERTS, tm), F32)], axis=0)
        rows_sc[...] = rows
        cols_sc[...] = rows.T
        cnt_sc[...] = jnp.broadcast_to(incl[:, tm - 1:tm], cnt_sc.shape)
        o_ref[...] = x

    lane = lax.broadcasted_iota(jnp.int32, (1, LANES), 1)
    grp_row = lax.broadcasted_iota(jnp.int32, cnt_sc.shape, 0) == grp
    n_grp = jnp.sum(jnp.where(grp_row, cnt_sc[...], 0.0), axis=0, keepdims=True)[0, 0].astype(jnp.int32)
    grp_f = grp.astype(F32)

    def run_chunk(c, size):
        rows = rows_sc[...]
        cols = cols_sc[...]
        in_chunk = (rows[ROUTE_GROUP:ROUTE_GROUP + 1, :] == grp_f) & (rows[ROUTE_CHUNK:ROUTE_CHUNK + 1, :] == c)
        slot = lax.broadcasted_iota(jnp.int32, (size, tm), 0).astype(F32)
        gather = jnp.where(in_chunk & (rows[ROUTE_SLOT:ROUTE_SLOT + 1, :] == slot), 1.0, 0.0).astype(BF16)
        in_chunk_t = (cols[:, ROUTE_GROUP:ROUTE_GROUP + 1] == grp_f) & (cols[:, ROUTE_CHUNK:ROUTE_CHUNK + 1] == c)
        slot_t = lax.broadcasted_iota(jnp.int32, (tm, size), 1).astype(F32)
        scatter = jnp.where(in_chunk_t & (cols[:, ROUTE_SLOT:ROUTE_SLOT + 1] == slot_t), 1.0, 0.0).astype(BF16)
        hs = _dot(gather, h_sc[...]).astype(BF16)
        table = sum(_dot(gather, part) for part in _split3(cols))
        ab = _dot(hs, wgu_ref[...])
        hid = []
        for j in range(MOE_EPG):
            cw = jnp.sum(jnp.where(lane == ROUTE_COMB + MOE_EPG * grp + j, table, 0.0), axis=1, keepdims=True)
            a = ab[:, 2 * j * MOE_HIDDEN:(2 * j + 1) * MOE_HIDDEN]
            hid.append(((a * jax.nn.sigmoid(a)) * ab[:, (2 * j + 1) * MOE_HIDDEN:(2 * j + 2) * MOE_HIDDEN]
                        * cw).astype(BF16))
        y = _dot(jnp.concatenate(hid, axis=1), wd_ref[...])
        o_ref[...] += _dot(scatter, y.astype(BF16))

    for c in range(tm // ch):
        left = n_grp - c * ch
        pl.when(left > MOE_SMALL_CHUNK)(functools.partial(run_chunk, c, ch))
        pl.when((left > 0) & (left <= MOE_SMALL_CHUNK))(functools.partial(run_chunk, c, MOE_SMALL_CHUNK))

    @pl.when(grp == pl.num_programs(1) - 1)
    def _():
        x2 = o_ref[...]
        e = _dot(p_ref[...].astype(BF16), wp_ref[...])
        gate = jax.nn.sigmoid(_dot(_rms_rows(x2, g1_ref[...]).astype(BF16), wg_ref[...]))
        o_ref[...] = x2 + _rms_rows(e * gate, g2_ref[...])


def _moe(x2d, mix_a, mix_b, w_out, g, wr_hi, wr_lo, rb, wgu, wd, p2d, wp, wg, g1, g2, tm=1024):
    t, d = x2d.shape
    kp = p2d.shape[1]
    half = mix_a.shape[1]
    tri = jnp.triu(jnp.ones((tm, tm), BF16))
    const = lambda shape: pl.BlockSpec(shape, lambda i, gi: (0,) * len(shape), pipeline_mode=pl.Buffered(1))
    return pl.pallas_call(
        _moe_kernel,
        grid=(t // tm, MOE_GROUPS),
        in_specs=[pl.BlockSpec((tm, d), lambda i, gi: (i, 0)),
                  pl.BlockSpec((tm, half), lambda i, gi: (i, 0)),
                  pl.BlockSpec((tm, half), lambda i, gi: (i, 0)),
                  const((2 * half, d)),
                  const((1, d)), const((MOE_ROUTER_ROWS, d)), const((MOE_ROUTER_ROWS, d)),
                  const((MOE_ROUTER_ROWS, 1)), const((tm, tm)),
                  pl.BlockSpec((None, d, MOE_EPG * 2 * MOE_HIDDEN), lambda i, gi: (gi, 0, 0)),
                  pl.BlockSpec((None, MOE_EPG * MOE_HIDDEN, d), lambda i, gi: (gi, 0, 0)),
                  pl.BlockSpec((tm, kp), lambda i, gi: (i, 0)),
                  const((kp, d)), const((d, d)), const((1, d)), const((1, d))],
        out_specs=pl.BlockSpec((tm, d), lambda i, gi: (i, 0)),
        out_shape=jax.ShapeDtypeStruct((t, d), F32),
        scratch_shapes=[pltpu.VMEM((tm, d), BF16), pltpu.VMEM((LANES, tm), F32), pltpu.VMEM((tm, LANES), F32),
                        pltpu.VMEM((16, LANES), F32)],
        compiler_params=_cparams(("parallel", "arbitrary"), 56),
        name="moe",
    )(x2d, mix_a, mix_b, w_out, g, wr_hi, wr_lo, rb, tri, wgu, wd, p2d, wp, wg, g1, g2)


def _pad_lanes(cols, width=LANES):
    return jnp.pad(cols, ((0, 0), (0, width - cols.shape[-1])))


def _row(v):
    return v.reshape(1, -1).astype(F32)


def _even_mix(x2d, bsz, seq, norm_g, w_in, fox_b_f, fox_qn_g, fox_kn_g, conv_w, conv_b, b_i, b_f, mnorm_g):
    hw = HALF
    o = np.cumsum([0, hw, hw, hw, ATT_HEADS, hw, hw, hw, hw, MLSTM_HEADS, MLSTM_HEADS, hw])
    seg = lambda i: w_in[:, o[i]:o[i + 1]]
    fq, fk, fv, ff, fo, mq, mk, mv, mi, mf, mo = [seg(i) for i in range(11)]
    w = jnp.concatenate([fq, fk, fv, mv, fo, mq, mk, mo, _pad_lanes(jnp.concatenate([ff, mi, mf], axis=1))],
                        axis=1).astype(BF16)
    gain = jnp.concatenate([jnp.tile(fox_qn_g, ATT_HEADS) * ATT_Q_SCALE, jnp.tile(fox_kn_g, ATT_HEADS)])
    att, rest, gates = _in_proj(x2d, _row(norm_g), w, _row(gain), ((4 * hw, BF16), (4 * hw, F32), (LANES, F32)),
                                2 * hw)
    att = att.reshape(bsz, seq, 4 * hw)
    rest = rest.reshape(bsz, seq, 4 * hw)
    bias = _pad_lanes(_row(jnp.concatenate([fox_b_f, b_i, b_f])))
    gcol, rm = _prep_even(gates.reshape(bsz, seq, LANES), bias)
    out_a = _fox_attention(att, rest, gcol)
    out_b = _mlstm(att, rest, conv_w, _row(conv_b), gcol, rm, _row(mnorm_g))
    return out_a.reshape(-1, hw), out_b.reshape(-1, hw)


def _odd_mix(x2d, bsz, seq, norm_g, w_in, moba_qn_g, moba_kn_g, conv_w, conv_b, dt_bias, a_log, d_skip, snorm_g):
    hw = HALF
    nbc = SSD_GROUPS * SSD_STATE
    o = np.cumsum([0, hw, hw, hw, hw, hw, nbc, nbc, SSD_HEADS])
    w = jnp.concatenate([w_in[:, :o[7]], _pad_lanes(w_in[:, o[7]:o[8]])], axis=1).astype(BF16)
    gain = jnp.concatenate([jnp.tile(moba_qn_g, ATT_HEADS) * ATT_Q_SCALE, jnp.tile(moba_kn_g, ATT_HEADS)])
    moba_u, zx, bcin, dts = _in_proj(x2d, _row(norm_g), w, _row(gain),
                                     ((3 * hw, BF16), (2 * hw, F32), (2 * nbc, F32), (LANES, F32)), 2 * hw)
    gcol, rs = _prep_odd(dts.reshape(bsz, seq, LANES), _pad_lanes(_row(dt_bias)), _pad_lanes(_row(a_log)))
    out_c = _moba_attention(moba_u.reshape(bsz, seq, 3 * hw))
    out_d = _ssd(zx.reshape(bsz, seq, 2 * hw), bcin.reshape(bsz, seq, 2 * nbc), conv_w, _row(conv_b), gcol,
                 _ssd_expand_matrices(), rs, _row(jnp.repeat(d_skip, SSD_HEAD_DIM)), _row(snorm_g))
    return out_c.reshape(-1, hw), out_d.reshape(-1, hw)


def _layer_tail(x2d, mix_a, mix_b, w_out, norm_g, w_group, b_group, w_router, b_router, w_gate, w_up, w_down,
                p2d, ple_w_proj, ple_w_gate, ple_gate_norm_g, ple_out_norm_g):
    d = x2d.shape[1]
    wr = _pad_lanes(jnp.concatenate([w_group, w_router], axis=1), MOE_ROUTER_ROWS).T
    wr_hi = wr.astype(BF16)
    wr_lo = (wr - wr_hi.astype(F32)).astype(BF16)
    rb = _pad_lanes(_row(jnp.concatenate([b_group, b_router])), MOE_ROUTER_ROWS).T
    wgu = jnp.concatenate([w_gate, w_up], axis=-1).astype(BF16)
    wgu = wgu.transpose(0, 2, 1, 3).reshape(MOE_GROUPS, d, MOE_EPG * 2 * MOE_HIDDEN)
    wd = w_down.reshape(MOE_GROUPS, MOE_EPG * MOE_HIDDEN, d).astype(BF16)
    return _moe(x2d, mix_a, mix_b, w_out.astype(BF16), _row(norm_g), wr_hi, wr_lo, rb, wgu, wd,
                p2d, ple_w_proj.astype(BF16), ple_w_gate.astype(BF16), _row(ple_gate_norm_g), _row(ple_out_norm_g))


def kernel(x, p, norm1_g, norm2_g, ev_w_in, ev_fox_b_f, ev_fox_qn_g, ev_fox_kn_g, ev_mlstm_conv_w, ev_mlstm_conv_b, ev_mlstm_b_i, ev_mlstm_b_f, ev_mlstm_norm_g, ev_w_out, od_w_in, od_moba_qn_g, od_moba_kn_g, od_ssd_conv_w, od_ssd_conv_b, od_ssd_dt_bias, od_ssd_A_log, od_ssd_D, od_ssd_norm_g, od_w_out, moe_w_group, moe_b_group, moe_w_router, moe_b_router, moe_w_gate, moe_w_up, moe_w_down, ple_w_proj, ple_w_gate, ple_gate_norm_g, ple_out_norm_g):
    bsz, seq, d = x.shape
    depth = p.shape[0]
    x2d = x.reshape(bsz * seq, d)
    for i in range(depth):
        j = i // 2
        if i % 2 == 0:
            mix_a, mix_b = _even_mix(x2d, bsz, seq, norm1_g[i], ev_w_in[j], ev_fox_b_f[j], ev_fox_qn_g[j],
                                     ev_fox_kn_g[j], ev_mlstm_conv_w[j], ev_mlstm_conv_b[j], ev_mlstm_b_i[j],
                                     ev_mlstm_b_f[j], ev_mlstm_norm_g[j])
            w_out = ev_w_out[j]
        else:
            mix_a, mix_b = _odd_mix(x2d, bsz, seq, norm1_g[i], od_w_in[j], od_moba_qn_g[j], od_moba_kn_g[j],
                                    od_ssd_conv_w[j], od_ssd_conv_b[j], od_ssd_dt_bias[j], od_ssd_A_log[j],
                                    od_ssd_D[j], od_ssd_norm_g[j])
            w_out = od_w_out[j]
        x2d = _layer_tail(x2d, mix_a, mix_b, w_out, norm2_g[i], moe_w_group[i], moe_b_group[i], moe_w_router[i],
                          moe_b_router[i], moe_w_gate[i], moe_w_up[i], moe_w_down[i], p[i].reshape(bsz * seq, -1),
                          ple_w_proj[i], ple_w_gate[i], ple_gate_norm_g[i], ple_out_norm_g[i])
    return x2d.reshape(bsz, seq, d)
```

```python
import functools

import jax
import jax.numpy as jnp
import numpy as np
from jax import lax
from jax.experimental import pallas as pl
from jax.experimental.pallas import tpu as pltpu

F32 = jnp.float32
BF16 = jnp.bfloat16
MIX_DTYPE = BF16
NEG_INF = float("-inf")
LOG2E = 1.4426950408889634

NORM_EPS = 1e-6
D_MODEL = 1024
HALF = D_MODEL // 2
ATT_HEAD_DIM = 64
ATT_HEADS = HALF // ATT_HEAD_DIM
ATT_PAIRS = ATT_HEADS // 2
ATT_Q_SCALE = ATT_HEAD_DIM ** -0.5 * LOG2E
MLSTM_HEAD_DIM = 128
MLSTM_HEADS = HALF // MLSTM_HEAD_DIM
SSD_HEAD_DIM = 64
SSD_HEADS = HALF // SSD_HEAD_DIM
SSD_GROUPS = 2
SSD_STATE = 128
SSD_GROUP_W = HALF // SSD_GROUPS
SSD_HEADS_PER_GROUP = SSD_HEADS // SSD_GROUPS
CONV_WIDTH = 4
MOBA_BLOCK = 256
MOBA_TOPK = 3
MOE_GROUPS = 4
MOE_EPG = 4
MOE_EXPERTS = MOE_GROUPS * MOE_EPG
MOE_HIDDEN = D_MODEL // 4
MOE_ROUTER_ROWS = 32
MOE_CHUNK = 256
MOE_SMALL_CHUNK = 64
PLE_DIM = 256

LANES = 128
MXU_N = 256
SEQ_TILE = 256
ATT_TILE = 512
MIB = 1024 * 1024


def _cparams(sem, vmem_mib):
    return pltpu.CompilerParams(dimension_semantics=sem, vmem_limit_bytes=vmem_mib * MIB)


def _split2(x):
    hi = x.astype(BF16)
    lo = (x - hi.astype(F32)).astype(BF16)
    return hi, lo


def _split3(x):
    hi = x.astype(BF16)
    r = x - hi.astype(F32)
    mid = r.astype(BF16)
    lo = (r - mid.astype(F32)).astype(BF16)
    return hi, mid, lo


def _dot(a, b):
    return jnp.dot(a, b, preferred_element_type=F32)


def _dot_nt(a, b):
    return lax.dot_general(a, b, (((1,), (1,)), ((), ())), preferred_element_type=F32)


def _log_sigmoid(x):
    return jnp.minimum(x, 0.0) - jnp.log1p(jnp.exp(-jnp.abs(x)))


def _softplus(x):
    return jnp.maximum(x, 0.0) + jnp.log1p(jnp.exp(-jnp.abs(x)))


def _rms_rows(xf, g, eps=NORM_EPS):
    ms = jnp.mean(xf * xf, axis=-1, keepdims=True)
    return xf * lax.rsqrt(ms + eps) * g


def _block_index(i, block):
    shift = block.bit_length() - 1
    assert 1 << shift == block
    return lax.shift_right_logical(i, shift)


def _lane_col(x, idx):
    lane = lax.broadcasted_iota(jnp.int32, (1, x.shape[1]), 1)
    return jnp.sum(jnp.where(lane == idx, x, 0.0), axis=1, keepdims=True)


def _in_proj_kernel(x_ref, g_ref, w_ref, gain_ref, *refs, widths, n_norm, head_dim):
    out_refs, raw_sc = refs[:-1], refs[-1]
    h = _rms_rows(x_ref[...], g_ref[...]).astype(BF16)
    off = 0
    for o_ref, wdt in zip(out_refs, widths):
        for c0 in range(0, wdt, MXU_N):
            n = min(MXU_N, wdt - c0)
            y = _dot(h, w_ref[:, off + c0: off + c0 + n])
            if off + c0 < n_norm:
                raw_sc[:, off + c0: off + c0 + n] = y
            else:
                o_ref[:, c0:c0 + n] = y.astype(o_ref.dtype)
        off += wdt
    r = _block_index(lax.broadcasted_iota(jnp.int32, (MXU_N, MXU_N), 0), head_dim)
    c = _block_index(lax.broadcasted_iota(jnp.int32, (MXU_N, MXU_N), 1), head_dim)
    seg = jnp.where(r == c, 1.0, 0.0).astype(BF16)
    assert n_norm <= widths[0] and n_norm % MXU_N == 0
    for c0 in range(0, n_norm, MXU_N):
        y = raw_sc[:, c0:c0 + MXU_N]
        ss = _dot((y * y).astype(BF16), seg)
        y = y * lax.rsqrt(ss * (1.0 / head_dim) + NORM_EPS) * gain_ref[:, c0:c0 + MXU_N]
        out_refs[0][:, c0:c0 + MXU_N] = y.astype(out_refs[0].dtype)


def _in_proj(x2d, g, w, gain, segments, n_norm, tm=512):
    t, d = x2d.shape
    n = w.shape[1]
    widths = tuple(wd for wd, _ in segments)
    kern = functools.partial(_in_proj_kernel, widths=widths, n_norm=n_norm, head_dim=ATT_HEAD_DIM)
    return pl.pallas_call(
        kern,
        grid=(t // tm,),
        in_specs=[
            pl.BlockSpec((tm, d), lambda i: (i, 0)),
            pl.BlockSpec((1, d), lambda i: (0, 0), pipeline_mode=pl.Buffered(1)),
            pl.BlockSpec((d, n), lambda i: (0, 0), pipeline_mode=pl.Buffered(1)),
            pl.BlockSpec((1, n_norm), lambda i: (0, 0), pipeline_mode=pl.Buffered(1)),
        ],
        out_specs=[pl.BlockSpec((tm, wd), lambda i: (i, 0)) for wd in widths],
        out_shape=[jax.ShapeDtypeStruct((t, wd), dt) for wd, dt in segments],
        scratch_shapes=[pltpu.VMEM((tm, n_norm), F32)],
        compiler_params=_cparams(("parallel",), 52),
        name="in_proj",
    )(x2d, g, w, gain)


def _tri_cumsum(vals, tri):
    hi, mid, lo = _split3(vals)
    return _dot(tri, hi) + _dot(tri, mid) + _dot(tri, lo)


def _lower_tri(n):
    r = lax.broadcasted_iota(jnp.int32, (n, n), 0)
    c = lax.broadcasted_iota(jnp.int32, (n, n), 1)
    return jnp.where(c <= r, 1.0, 0.0).astype(BF16)


def _prep_even_kernel(raw_ref, bias_ref, g_ref, rm_ref):
    s = raw_ref.shape[0]
    ch = SEQ_TILE
    lane = lax.broadcasted_iota(jnp.int32, (1, LANES), 1)
    is_cum = (lane < ATT_HEADS) | ((lane >= ATT_HEADS + MLSTM_HEADS) & (lane < ATT_HEADS + 2 * MLSTM_HEADS))
    tri = _lower_tri(ch)
    carry = jnp.zeros((1, LANES), F32)
    pad = jnp.zeros((6, ch), F32)
    for c in range(s // ch):
        z = raw_ref[c * ch:(c + 1) * ch, :] + bias_ref[...]
        cs = _tri_cumsum(jnp.where(is_cum, _log_sigmoid(z), 0.0), tri) + carry
        carry = cs[ch - 1:ch, :]
        g = jnp.where(is_cum, cs, z)
        g_ref[c * ch:(c + 1) * ch, :] = g
        gt = g.T
        for hh in range(MLSTM_HEADS):
            i_row = gt[ATT_HEADS + hh:ATT_HEADS + hh + 1]
            f_row = gt[ATT_HEADS + MLSTM_HEADS + hh:ATT_HEADS + MLSTM_HEADS + hh + 1]
            rm_ref[hh, :, c * ch:(c + 1) * ch] = jnp.concatenate([i_row - f_row, f_row, pad], axis=0)


def _prep_even(raw, bias):
    b, s, _ = raw.shape
    return pl.pallas_call(
        _prep_even_kernel,
        grid=(b,),
        in_specs=[pl.BlockSpec((None, s, LANES), lambda i: (i, 0, 0)),
                  pl.BlockSpec((1, LANES), lambda i: (0, 0))],
        out_specs=[pl.BlockSpec((None, s, LANES), lambda i: (i, 0, 0)),
                   pl.BlockSpec((None, MLSTM_HEADS, 8, s), lambda i: (i, 0, 0, 0))],
        out_shape=[jax.ShapeDtypeStruct((b, s, LANES), F32),
                   jax.ShapeDtypeStruct((b, MLSTM_HEADS, 8, s), F32)],
        compiler_params=_cparams(("parallel",), 32),
        name="prep_even",
    )(raw, bias)


def _prep_odd_kernel(raw_ref, bias_ref, alog_ref, g_ref, rs_ref):
    s = raw_ref.shape[0]
    ch = SEQ_TILE
    lane = lax.broadcasted_iota(jnp.int32, (1, LANES), 1)
    is_dt = lane < SSD_HEADS
    tri = _lower_tri(ch)
    a_row = jnp.where(is_dt, -jnp.exp(alog_ref[...]), 0.0)
    pad = jnp.zeros((8 - SSD_HEADS_PER_GROUP, ch), F32)
    for c in range(s // ch):
        dt = jnp.where(is_dt, _softplus(raw_ref[c * ch:(c + 1) * ch, :] + bias_ref[...]), 0.0)
        acum = _tri_cumsum(a_row * dt, tri)
        g = jnp.where(is_dt, dt, pltpu.roll(acum, SSD_HEADS, 1))
        g_ref[c * ch:(c + 1) * ch, :] = g
        gt = g.T
        for gi in range(SSD_GROUPS):
            lo = SSD_HEADS + gi * SSD_HEADS_PER_GROUP
            rs_ref[gi, :, c * ch:(c + 1) * ch] = jnp.concatenate([gt[lo:lo + SSD_HEADS_PER_GROUP], pad], axis=0)


def _prep_odd(raw, bias, alog):
    b, s, _ = raw.shape
    return pl.pallas_call(
        _prep_odd_kernel,
        grid=(b,),
        in_specs=[pl.BlockSpec((None, s, LANES), lambda i: (i, 0, 0)),
                  pl.BlockSpec((1, LANES), lambda i: (0, 0)),
                  pl.BlockSpec((1, LANES), lambda i: (0, 0))],
        out_specs=[pl.BlockSpec((None, s, LANES), lambda i: (i, 0, 0)),
                   pl.BlockSpec((None, SSD_GROUPS, 8, s), lambda i: (i, 0, 0, 0))],
        out_shape=[jax.ShapeDtypeStruct((b, s, LANES), F32),
                   jax.ShapeDtypeStruct((b, SSD_GROUPS, 8, s), F32)],
        compiler_params=_cparams(("parallel",), 32),
        name="prep_odd",
    )(raw, bias, alog)


def _two_pass_softmax_pv(n_past, past_span, past_logits, own_span, own_logits, v_rows, s_scr, tq):
    heads = range(2)
    neg = jnp.full((1, tq), NEG_INF, F32)

    def store_max(span, s_t, hh, m):
        s_scr[hh, pl.ds(*span), :] = s_t
        return jnp.maximum(m, jnp.max(s_t, axis=0, keepdims=True))

    def accumulate(span, hh, m, carry):
        l, acc = carry
        p = jnp.exp2(s_scr[hh, pl.ds(*span), :] - m)
        return l + jnp.sum(p, axis=0, keepdims=True), acc + _dot(v_rows(*span, hh), p.astype(BF16))

    ms = [neg, neg]
    for j in range(n_past):
        ms = [store_max(past_span(j), past_logits(j, hh), hh, ms[hh]) for hh in heads]
    ms = [store_max(own_span, own_logits(hh), hh, ms[hh]) for hh in heads]
    zero = (jnp.zeros((1, tq), F32), jnp.zeros((ATT_HEAD_DIM, tq), F32))
    carry = [accumulate(own_span, hh, ms[hh], zero) for hh in heads]
    for j in range(n_past):
        carry = [accumulate(past_span(j), hh, ms[hh], carry[hh]) for hh in heads]
    return jnp.concatenate([acc / l for l, acc in carry], axis=0).T


def _stage_v(v_ref, vt):
    for c in range(v_ref.shape[0] // SEQ_TILE):
        vt[:, c * SEQ_TILE:(c + 1) * SEQ_TILE] = v_ref[c * SEQ_TILE:(c + 1) * SEQ_TILE, :].astype(F32).T.astype(BF16)


def _head_queries(q2):
    lane = lax.broadcasted_iota(jnp.int32, (1, LANES), 1)
    return [jnp.where(lane < ATT_HEAD_DIM, q2, 0.0), jnp.where(lane >= ATT_HEAD_DIM, q2, 0.0)]


def _fox_kernel(q_ref, k_ref, v_ref, fo_ref, g_ref, o_ref, vt, cfb, s_scr):
    hp = pl.program_id(1)
    tq = s_scr.shape[2]
    s_len = k_ref.shape[0]
    hd = ATT_HEAD_DIM
    _stage_v(v_ref, vt)
    g = g_ref[...]
    for hh in range(2):
        cfb[hh] = jnp.broadcast_to(_lane_col(g, 2 * hp + hh) * LOG2E, (s_len, LANES))
    krow = lax.broadcasted_iota(jnp.int32, (tq, tq), 0)
    qcol = lax.broadcasted_iota(jnp.int32, (tq, tq), 1)

    def span(j):
        return j * tq, tq

    for qi in range(s_len // tq):
        rows = slice(qi * tq, (qi + 1) * tq)
        qh = _head_queries(q_ref[rows, :])

        def logits(j, hh, qh=qh):
            s_t = _dot_nt(k_ref[pl.ds(*span(j)), :], qh[hh])
            return s_t - jnp.concatenate([cfb[hh, pl.ds(*span(j)), :]] * (tq // LANES), axis=1)

        out = _two_pass_softmax_pv(
            qi, span, logits, span(qi), lambda hh, qi=qi, logits=logits: jnp.where(krow <= qcol, logits(qi, hh), NEG_INF),
            lambda st, size, hh: vt[hh * hd:(hh + 1) * hd, pl.ds(st, size)], s_scr, tq)
        o_ref[rows, :] = (out * jax.nn.sigmoid(fo_ref[rows, :])).astype(o_ref.dtype)


def _fox_attention(qkv, ogate, gcol, tq=ATT_TILE):
    b, s, _ = qkv.shape
    np_ = ATT_PAIRS
    return pl.pallas_call(
        _fox_kernel,
        grid=(b, np_),
        in_specs=[
            pl.BlockSpec((None, s, LANES), lambda bi, hp: (bi, 0, hp)),
            pl.BlockSpec((None, s, LANES), lambda bi, hp: (bi, 0, np_ + hp)),
            pl.BlockSpec((None, s, LANES), lambda bi, hp: (bi, 0, 2 * np_ + hp)),
            pl.BlockSpec((None, s, LANES), lambda bi, hp: (bi, 0, hp)),
            pl.BlockSpec((None, s, LANES), lambda bi, hp: (bi, 0, 0)),
        ],
        out_specs=pl.BlockSpec((None, s, LANES), lambda bi, hp: (bi, 0, hp)),
        out_shape=jax.ShapeDtypeStruct((b, s, HALF), MIX_DTYPE),
        scratch_shapes=[pltpu.VMEM((LANES, s), BF16), pltpu.VMEM((2, s, LANES), F32),
                        pltpu.VMEM((2, s, tq), F32)],
        compiler_params=_cparams(("parallel", "parallel"), 40),
        name="fox_attention",
    )(qkv, qkv, qkv, ogate, gcol)


def _moba_kernel(q_ref, k_ref, v_ref, o_ref, vt, kmh, kml, s_scr):
    tq = s_scr.shape[2]
    s_len = k_ref.shape[0]
    blk = MOBA_BLOCK
    nb = s_len // blk
    nbp = kmh.shape[0]
    hd = ATT_HEAD_DIM
    per_tile = tq // blk
    _stage_v(v_ref, vt)
    means = [jnp.mean(k_ref[n * blk:(n + 1) * blk, :].astype(F32), axis=0, keepdims=True) for n in range(nb)]
    km = jnp.concatenate(means + [jnp.zeros((nbp - nb, LANES), F32)], axis=0)
    kmh[...], kml[...] = _split2(km)

    qcol1 = lax.broadcasted_iota(jnp.int32, (1, tq), 1)
    nrow = lax.broadcasted_iota(jnp.int32, (nbp, tq), 0)
    krow = lax.broadcasted_iota(jnp.int32, (tq, tq), 0)
    qcol = lax.broadcasted_iota(jnp.int32, (tq, tq), 1)
    same_blk = _block_index(krow, blk) == _block_index(qcol, blk)

    def v_rows(st, size, hh):
        return vt[hh * hd:(hh + 1) * hd, pl.ds(st, size)]

    def span(j):
        return j * tq, tq

    for qi in range(s_len // tq):
        rows = slice(qi * tq, (qi + 1) * tq)
        qh = _head_queries(q_ref[rows, :])
        valid = nrow < per_tile * qi + _block_index(qcol1, blk)
        selb = []
        for hh in range(2):
            gate = _dot_nt(kmh[...], qh[hh]) + _dot_nt(kml[...], qh[hh])
            gate = jnp.where(valid, gate, NEG_INF)
            rank = jnp.zeros((nbp, tq), F32)
            for m in range(nb):
                gm = gate[m:m + 1, :]
                rank = rank + jnp.where((gm > gate) | ((gm == gate) & (nrow > m)), 1.0, 0.0)
            selb.append(jnp.where((rank < float(MOBA_TOPK)) & valid, 0.0, NEG_INF))

        def sel_rows(j, hh):
            return jnp.concatenate(
                [jnp.broadcast_to(selb[hh][per_tile * j + n:per_tile * j + n + 1, :], (blk, tq))
                 for n in range(per_tile)], axis=0)

        def own_logits(hh):
            s_t = _dot_nt(k_ref[pl.ds(*span(qi)), :], qh[hh])
            return s_t + jnp.where(krow <= qcol, jnp.where(same_blk, 0.0, sel_rows(qi, hh)), NEG_INF)

        def past_logits(j, hh):
            return _dot_nt(k_ref[pl.ds(*span(j)), :], qh[hh]) + sel_rows(j, hh)

        out = _two_pass_softmax_pv(qi, span, past_logits, span(qi), own_logits, v_rows, s_scr, tq)
        o_ref[rows, :] = out.astype(o_ref.dtype)


def _moba_attention(qkv, tq=ATT_TILE):
    b, s, _ = qkv.shape
    np_ = ATT_PAIRS
    nb = s // MOBA_BLOCK
    nbp = -(-nb // 16) * 16
    return pl.pallas_call(
        _moba_kernel,
        grid=(b, np_),
        in_specs=[
            pl.BlockSpec((None, s, LANES), lambda bi, hp: (bi, 0, hp)),
            pl.BlockSpec((None, s, LANES), lambda bi, hp: (bi, 0, np_ + hp)),
            pl.BlockSpec((None, s, LANES), lambda bi, hp: (bi, 0, 2 * np_ + hp)),
        ],
        out_specs=pl.BlockSpec((None, s, LANES), lambda bi, hp: (bi, 0, hp)),
        out_shape=jax.ShapeDtypeStruct((b, s, HALF), MIX_DTYPE),
        scratch_shapes=[pltpu.VMEM((LANES, s), BF16),
                        pltpu.VMEM((nbp, LANES), BF16), pltpu.VMEM((nbp, LANES), BF16),
                        pltpu.VMEM((2, s, tq), F32)],
        compiler_params=_cparams(("parallel", "parallel"), 40),
        name="moba_attention",
    )(qkv, qkv, qkv)


def _conv_silu(x_ref, w_ref, b_ref, o_ref, scale):
    s_len, ch = x_ref.shape
    lc = SEQ_TILE
    w = w_ref[...]
    b = b_ref[...]
    row = lax.broadcasted_iota(jnp.int32, (8, ch), 0)
    for c in range(s_len // lc):
        cur = x_ref[c * lc:(c + 1) * lc, :]
        tail = x_ref[c * lc - 8:c * lc, :] if c > 0 else jnp.zeros((8, ch), F32)
        y = b
        for j in range(CONV_WIDTH - 1):
            sh = CONV_WIDTH - 1 - j
            rolled = pltpu.roll(cur, sh, 0)
            head = jnp.where(row < sh, pltpu.roll(tail, sh, 0), rolled[:8])
            y = y + jnp.concatenate([head, rolled[8:]], axis=0) * w[j:j + 1]
        y = y + cur * w[CONV_WIDTH - 1:CONV_WIDTH]
        y = y * jax.nn.sigmoid(y)
        o_ref[c * lc:(c + 1) * lc, :] = y * scale if scale != 1.0 else y


def _mlstm_kernel(q_ref, k_ref, v_ref, og_ref, cwq_ref, cwk_ref, cbq_ref, cbk_ref, g_ref, r_ref, ng_ref,
                  o_ref, qc, kc):
    hh = pl.program_id(1)
    s_len, d = q_ref.shape
    ln = SEQ_TILE
    _conv_silu(q_ref, cwq_ref, cbq_ref, qc, 1.0)
    _conv_silu(k_ref, cwk_ref, cbk_ref, kc, d ** -0.5)
    row = lax.broadcasted_iota(jnp.int32, (ln, ln), 0)
    col = lax.broadcasted_iota(jnp.int32, (ln, ln), 1)
    tri = col <= row
    lane = lax.broadcasted_iota(jnp.int32, (1, LANES), 1)
    ones_blk = jnp.broadcast_to(jnp.where(lane == 0, 1.0, 0.0), (ln, LANES)).astype(BF16)
    f_lane = ATT_HEADS + MLSTM_HEADS + hh
    c_prev = jnp.zeros((d, 2 * d), F32)
    m_prev = jnp.zeros((1, 1), F32)
    for c in range(s_len // ln):
        rows = slice(c * ln, (c + 1) * ln)
        q = qc[rows, :]
        k = kc[rows, :]
        f_col = _lane_col(g_ref[rows, :], f_lane)
        a_row = r_ref[0:1, rows]
        a_mat = jnp.where(tri, a_row, NEG_INF)
        m_col = jnp.maximum(m_prev, jnp.max(a_mat, axis=1, keepdims=True))
        qb = q.astype(BF16)
        w_qk = _dot_nt(qb, k.astype(BF16)) * jnp.exp(a_mat - m_col)
        v_aug = jnp.concatenate([v_ref[rows, :], ones_blk], axis=1)
        inter = jnp.exp(m_prev - m_col)
        q_c = _dot(qb, c_prev.astype(BF16))
        num = _dot(w_qk.astype(BF16), v_aug[:, :d]) + inter * q_c[:, :d]
        den = jnp.sum(w_qk, axis=1, keepdims=True) + inter * q_c[:, d:d + 1]
        h = num / jnp.maximum(jnp.abs(den), jnp.exp(-f_col - m_col))
        m_end = jnp.maximum(m_prev, jnp.max(a_row, axis=1, keepdims=True))
        k_w = (k.T * jnp.exp(a_row - m_end)).astype(BF16)
        c_prev = jnp.exp(m_prev - m_end) * c_prev + _dot(k_w, v_aug)
        m_prev = m_end
        hn = _rms_rows(h, ng_ref[...])
        o_ref[rows, :] = (hn * jax.nn.sigmoid(og_ref[rows, :])).astype(o_ref.dtype)


def _mlstm(att, rest, conv_w, conv_b, gcol, grow, norm_g):
    b, s, _ = att.shape
    nh = MLSTM_HEADS
    d = MLSTM_HEAD_DIM
    big = lambda off: pl.BlockSpec((None, s, d), lambda bi, h: (bi, 0, off + h))
    return pl.pallas_call(
        _mlstm_kernel,
        grid=(b, nh),
        in_specs=[
            big(nh), big(2 * nh), big(3 * nh), big(3 * nh),
            pl.BlockSpec((CONV_WIDTH, d), lambda bi, h: (0, h)),
            pl.BlockSpec((CONV_WIDTH, d), lambda bi, h: (0, nh + h)),
            pl.BlockSpec((1, d), lambda bi, h: (0, h)),
            pl.BlockSpec((1, d), lambda bi, h: (0, nh + h)),
            pl.BlockSpec((None, s, LANES), lambda bi, h: (bi, 0, 0)),
            pl.BlockSpec((None, None, 8, s), lambda bi, h: (bi, h, 0, 0)),
            pl.BlockSpec((1, d), lambda bi, h: (0, h)),
        ],
        out_specs=pl.BlockSpec((None, s, d), lambda bi, h: (bi, 0, h)),
        out_shape=jax.ShapeDtypeStruct((b, s, HALF), MIX_DTYPE),
        scratch_shapes=[pltpu.VMEM((s, d), F32), pltpu.VMEM((s, d), F32)],
        compiler_params=_cparams(("parallel", "parallel"), 40),
        name="mlstm",
    )(rest, rest, att, rest, conv_w, conv_w, conv_b, conv_b, gcol, grow, norm_g)


def _ssd_kernel(z_ref, x_ref, b_ref, c_ref, cwx_ref, cwb_ref, cwc_ref, cbx_ref, cbb_ref, cbc_ref,
                g_ref, e_ref, r_ref, dsk_ref, ng_ref, o_ref, xc, bc, cc):
    s_len, gw = x_ref.shape
    ln = SEQ_TILE
    _conv_silu(x_ref, cwx_ref, cbx_ref, xc, 1.0)
    _conv_silu(b_ref, cwb_ref, cbb_ref, bc, 1.0)
    _conv_silu(c_ref, cwc_ref, cbc_ref, cc, 1.0)
    row = lax.broadcasted_iota(jnp.int32, (ln, ln), 0)
    col = lax.broadcasted_iota(jnp.int32, (ln, ln), 1)
    tri = col <= row
    head_of_lane = _block_index(lax.broadcasted_iota(jnp.int32, (1, gw), 1), SSD_HEAD_DIM)
    e_dt2 = jnp.concatenate([e_ref[0], e_ref[0]], axis=0)
    e_ac2 = jnp.concatenate([e_ref[1], e_ref[1]], axis=0)
    prev = jnp.zeros((b_ref.shape[1], gw), F32)
    for c in range(s_len // ln):
        rows = slice(c * ln, (c + 1) * ln)
        x = xc[rows, :]
        bm = bc[rows, :]
        cm = cc[rows, :]
        z = z_ref[rows, :]
        hi, mid, lo = _split3(g_ref[rows, :])
        hi_mid = jnp.concatenate([hi, mid], axis=1)
        dt_e = _dot(hi_mid, e_dt2) + _dot(lo, e_ref[0])
        ac_e = _dot(hi_mid, e_ac2) + _dot(lo, e_ref[1])
        xdt = x * dt_e
        xdt_b = xdt.astype(BF16)
        cmb = cm.astype(BF16)
        cb = _dot_nt(cmb, bm.astype(BF16))
        ac_rows = r_ref[0:SSD_HEADS_PER_GROUP, rows]
        y = jnp.zeros((ln, gw), F32)
        for hh in range(SSD_HEADS_PER_GROUP):
            ac_col = ac_e[:, hh * SSD_HEAD_DIM:hh * SSD_HEAD_DIM + 1]
            l_mat = jnp.exp(jnp.where(tri, ac_col - ac_rows[hh:hh + 1], NEG_INF))
            y = jnp.where(head_of_lane == hh, _dot((cb * l_mat).astype(BF16), xdt_b), y)
        ac_end = ac_e[ln - 1:ln, :]
        y = y + _dot(cmb, prev.astype(BF16)) * jnp.exp(ac_e)
        states = _dot(bm.T.astype(BF16), (xdt * jnp.exp(ac_end - ac_e)).astype(BF16))
        prev = prev * jnp.exp(ac_end) + states
        y = y + dsk_ref[...] * x
        y = y * (z * jax.nn.sigmoid(z))
        o_ref[rows, :] = _rms_rows(y, ng_ref[...]).astype(o_ref.dtype)


def _ssd(zx, bcin, conv_w, conv_b, gcol, expand, grow, d_row, norm_g):
    b, s, _ = zx.shape
    gw = SSD_GROUP_W
    ns = SSD_STATE
    ng = SSD_GROUPS
    xoff = HALF // gw
    boff = HALF // ns
    coff = (HALF + ng * ns) // ns
    return pl.pallas_call(
        _ssd_kernel,
        grid=(b, ng),
        in_specs=[
            pl.BlockSpec((None, s, gw), lambda bi, g: (bi, 0, g)),
            pl.BlockSpec((None, s, gw), lambda bi, g: (bi, 0, xoff + g)),
            pl.BlockSpec((None, s, ns), lambda bi, g: (bi, 0, g)),
            pl.BlockSpec((None, s, ns), lambda bi, g: (bi, 0, ng + g)),
            pl.BlockSpec((CONV_WIDTH, gw), lambda bi, g: (0, g)),
            pl.BlockSpec((CONV_WIDTH, ns), lambda bi, g: (0, boff + g)),
            pl.BlockSpec((CONV_WIDTH, ns), lambda bi, g: (0, coff + g)),
            pl.BlockSpec((1, gw), lambda bi, g: (0, g)),
            pl.BlockSpec((1, ns), lambda bi, g: (0, boff + g)),
            pl.BlockSpec((1, ns), lambda bi, g: (0, coff + g)),
            pl.BlockSpec((None, s, LANES), lambda bi, g: (bi, 0, 0)),
            pl.BlockSpec((None, 2, LANES, gw), lambda bi, g: (g, 0, 0, 0)),
            pl.BlockSpec((None, None, 8, s), lambda bi, g: (bi, g, 0, 0)),
            pl.BlockSpec((1, gw), lambda bi, g: (0, g)),
            pl.BlockSpec((1, gw), lambda bi, g: (0, g)),
        ],
        out_specs=pl.BlockSpec((None, s, gw), lambda bi, g: (bi, 0, g)),
        out_shape=jax.ShapeDtypeStruct((b, s, HALF), MIX_DTYPE),
        scratch_shapes=[pltpu.VMEM((s, gw), F32), pltpu.VMEM((s, ns), F32), pltpu.VMEM((s, ns), F32)],
        compiler_params=_cparams(("parallel", "parallel"), 48),
        name="ssd",
    )(zx, zx, bcin, bcin, conv_w, conv_w, conv_w, conv_b, conv_b, conv_b, gcol, expand, grow, d_row, norm_g)


def _ssd_expand_matrices():
    e = np.zeros((SSD_GROUPS, 2, LANES, SSD_GROUP_W), np.float32)
    for g in range(SSD_GROUPS):
        for h in range(SSD_HEADS_PER_GROUP):
            head = g * SSD_HEADS_PER_GROUP + h
            e[g, 0, head, h * SSD_HEAD_DIM:(h + 1) * SSD_HEAD_DIM] = 1.0
            e[g, 1, SSD_HEADS + head, h * SSD_HEAD_DIM:(h + 1) * SSD_HEAD_DIM] = 1.0
    return jnp.asarray(e, BF16)


def _moe_route(logits_t):
    gl = [logits_t[g:g + 1, :] for g in range(MOE_GROUPS)]
    g_max = functools.reduce(jnp.maximum, gl)
    g_den = sum(jnp.exp(x - g_max) for x in gl)
    g_w = 1.0 / g_den
    taken = jnp.zeros_like(g_max) > 1.0
    is_g = []
    for g in range(MOE_GROUPS):
        hit = (gl[g] == g_max) & jnp.logical_not(taken)
        is_g.append(hit)
        taken = taken | hit
    e_in = []
    for j in range(MOE_EPG):
        v = jnp.zeros_like(g_max)
        for g in range(MOE_GROUPS):
            row = MOE_GROUPS + g * MOE_EPG + j
            v = jnp.where(is_g[g], logits_t[row:row + 1, :], v)
        e_in.append(v)
    rank = []
    for j in range(MOE_EPG):
        r = jnp.zeros_like(g_max)
        for m in range(MOE_EPG):
            if m == j:
                continue
            ahead = (e_in[m] > e_in[j]) | ((e_in[m] == e_in[j]) & (m < j))
            r = r + jnp.where(ahead, 1.0, 0.0)
        rank.append(r)
    v0 = sum(jnp.where(rank[j] == 0.0, e_in[j], 0.0) for j in range(MOE_EPG))
    v1 = sum(jnp.where(rank[j] == 1.0, e_in[j], 0.0) for j in range(MOE_EPG))
    e1 = jnp.exp(v1 - v0)
    w0 = 1.0 / (1.0 + e1)
    w1 = e1 / (1.0 + e1)
    comb = []
    for g in range(MOE_GROUPS):
        for j in range(MOE_EPG):
            wj = jnp.where(rank[j] == 0.0, w0, jnp.where(rank[j] == 1.0, w1, 0.0))
            comb.append(jnp.where(is_g[g], g_w * wj, 0.0))
    return comb, is_g


ROUTE_GROUP, ROUTE_CHUNK, ROUTE_SLOT, ROUTE_COMB = 0, 1, 2, 8


def _moe_kernel(x_ref, ma_ref, mb_ref, wo_ref, g_ref, wrh_ref, wrl_ref, rb_ref, tri_ref, wgu_ref, wd_ref,
                p_ref, wp_ref, wg_ref, g1_ref, g2_ref, o_ref, h_sc, rows_sc, cols_sc, cnt_sc):
    grp = pl.program_id(1)
    tm, d = x_ref.shape
    ch = MOE_CHUNK

    @pl.when(grp == 0)
    def _():
        half = ma_ref.shape[1]
        x = x_ref[...] + _dot(ma_ref[...], wo_ref[:half, :]) + _dot(mb_ref[...], wo_ref[half:, :])
        h_hi, h_lo = _split2(_rms_rows(x, g_ref[...]))
        h_sc[...] = h_hi
        logits_t = (_dot_nt(wrh_ref[...], h_hi) + _dot_nt(wrh_ref[...], h_lo) + _dot_nt(wrl_ref[...], h_hi)
                    + rb_ref[...])
        comb, is_g = _moe_route(logits_t)
        member = jnp.concatenate([jnp.where(m, 1.0, 0.0) for m in is_g]
                                 + [jnp.zeros((16 - MOE_GROUPS, tm), F32)], axis=0)
        incl = _dot(member.astype(BF16), tri_ref[...])
        pos = sum(jnp.where(is_g[g], incl[g:g + 1, :] - 1.0, 0.0) for g in range(MOE_GROUPS))
        gid = sum(jnp.where(is_g[g], float(g), 0.0) for g in range(MOE_GROUPS))
        chunk = jnp.floor(pos * (1.0 / ch))
        rows = jnp.concatenate([gid, chunk, pos - ch * chunk, jnp.zeros((ROUTE_COMB - 3, tm), F32)] + comb
                               + [jnp.zeros((LANES - ROUTE_COMB - MOE_EXPERTS, tm), F32)], axis=0)
        rows_sc[...] = rows
        cols_sc[...] = rows.T
        cnt_sc[...] = jnp.broadcast_to(incl[:, tm - 1:tm], cnt_sc.shape)
        o_ref[...] = x

    lane = lax.broadcasted_iota(jnp.int32, (1, LANES), 1)
    grp_row = lax.broadcasted_iota(jnp.int32, cnt_sc.shape, 0) == grp
    n_grp = jnp.sum(jnp.where(grp_row, cnt_sc[...], 0.0), axis=0, keepdims=True)[0, 0].astype(jnp.int32)
    grp_f = grp.astype(F32)

    def run_chunk(c, size):
        rows = rows_sc[...]
        cols = cols_sc[...]
        in_chunk = (rows[ROUTE_GROUP:ROUTE_GROUP + 1, :] == grp_f) & (rows[ROUTE_CHUNK:ROUTE_CHUNK + 1, :] == c)
        slot = lax.broadcasted_iota(jnp.int32, (size, tm), 0).astype(F32)
        gather = jnp.where(in_chunk & (rows[ROUTE_SLOT:ROUTE_SLOT + 1, :] == slot), 1.0, 0.0).astype(BF16)
        in_chunk_t = (cols[:, ROUTE_GROUP:ROUTE_GROUP + 1] == grp_f) & (cols[:, ROUTE_CHUNK:ROUTE_CHUNK + 1] == c)
        slot_t = lax.broadcasted_iota(jnp.int32, (tm, size), 1).astype(F32)
        scatter = jnp.where(in_chunk_t & (cols[:, ROUTE_SLOT:ROUTE_SLOT + 1] == slot_t), 1.0, 0.0).astype(BF16)
        hs = _dot(gather, h_sc[...]).astype(BF16)
        table = sum(_dot(gather, part) for part in _split3(cols))
        ab = _dot(hs, wgu_ref[...])
        hid = []
        for j in range(MOE_EPG):
            cw = jnp.sum(jnp.where(lane == ROUTE_COMB + MOE_EPG * grp + j, table, 0.0), axis=1, keepdims=True)
            a = ab[:, 2 * j * MOE_HIDDEN:(2 * j + 1) * MOE_HIDDEN]
            hid.append(((a * jax.nn.sigmoid(a)) * ab[:, (2 * j + 1) * MOE_HIDDEN:(2 * j + 2) * MOE_HIDDEN]
                        * cw).astype(BF16))
        y = _dot(jnp.concatenate(hid, axis=1), wd_ref[...])
        o_ref[...] += _dot(scatter, y.astype(BF16))

    for c in range(tm // ch):
        left = n_grp - c * ch
        pl.when(left > MOE_SMALL_CHUNK)(functools.partial(run_chunk, c, ch))
        pl.when((left > 0) & (left <= MOE_SMALL_CHUNK))(functools.partial(run_chunk, c, MOE_SMALL_CHUNK))

    @pl.when(grp == pl.num_programs(1) - 1)
    def _():
        x2 = o_ref[...]
        e = _dot(p_ref[...].astype(BF16), wp_ref[...])
        gate = jax.nn.sigmoid(_dot(_rms_rows(x2, g1_ref[...]).astype(BF16), wg_ref[...]))
        o_ref[...] = x2 + _rms_rows(e * gate, g2_ref[...])


def _moe(x2d, mix_a, mix_b, w_out, g, wr_hi, wr_lo, rb, wgu, wd, p2d, wp, wg, g1, g2, tm=1024):
    t, d = x2d.shape
    kp = p2d.shape[1]
    half = mix_a.shape[1]
    tri = jnp.triu(jnp.ones((tm, tm), BF16))
    const = lambda shape: pl.BlockSpec(shape, lambda i, gi: (0,) * len(shape), pipeline_mode=pl.Buffered(1))
    return pl.pallas_call(
        _moe_kernel,
        grid=(t // tm, MOE_GROUPS),
        in_specs=[pl.BlockSpec((tm, d), lambda i, gi: (i, 0)),
                  pl.BlockSpec((tm, half), lambda i, gi: (i, 0)),
                  pl.BlockSpec((tm, half), lambda i, gi: (i, 0)),
                  const((2 * half, d)),
                  const((1, d)), const((MOE_ROUTER_ROWS, d)), const((MOE_ROUTER_ROWS, d)),
                  const((MOE_ROUTER_ROWS, 1)), const((tm, tm)),
                  pl.BlockSpec((None, d, MOE_EPG * 2 * MOE_HIDDEN), lambda i, gi: (gi, 0, 0)),
                  pl.BlockSpec((None, MOE_EPG * MOE_HIDDEN, d), lambda i, gi: (gi, 0, 0)),
                  pl.BlockSpec((tm, kp), lambda i, gi: (i, 0)),
                  const((kp, d)), const((d, d)), const((1, d)), const((1, d))],
        out_specs=pl.BlockSpec((tm, d), lambda i, gi: (i, 0)),
        out_shape=jax.ShapeDtypeStruct((t, d), F32),
        scratch_shapes=[pltpu.VMEM((tm, d), BF16), pltpu.VMEM((LANES, tm), F32), pltpu.VMEM((tm, LANES), F32),
                        pltpu.VMEM((16, LANES), F32)],
        compiler_params=_cparams(("parallel", "arbitrary"), 56),
        name="moe",
    )(x2d, mix_a, mix_b, w_out, g, wr_hi, wr_lo, rb, tri, wgu, wd, p2d, wp, wg, g1, g2)


def _pad_lanes(cols, width=LANES):
    return jnp.pad(cols, ((0, 0), (0, width - cols.shape[-1])))


def _row(v):
    return v.reshape(1, -1).astype(F32)


def _even_mix(x2d, bsz, seq, norm_g, w_in, fox_b_f, fox_qn_g, fox_kn_g, conv_w, conv_b, b_i, b_f, mnorm_g):
    hw = HALF
    o = np.cumsum([0, hw, hw, hw, ATT_HEADS, hw, hw, hw, hw, MLSTM_HEADS, MLSTM_HEADS, hw])
    seg = lambda i: w_in[:, o[i]:o[i + 1]]
    fq, fk, fv, ff, fo, mq, mk, mv, mi, mf, mo = [seg(i) for i in range(11)]
    w = jnp.concatenate([fq, fk, fv, mv, fo, mq, mk, mo, _pad_lanes(jnp.concatenate([ff, mi, mf], axis=1))],
                        axis=1).astype(BF16)
    gain = jnp.concatenate([jnp.tile(fox_qn_g, ATT_HEADS) * ATT_Q_SCALE, jnp.tile(fox_kn_g, ATT_HEADS)])
    att, rest, gates = _in_proj(x2d, _row(norm_g), w, _row(gain), ((4 * hw, BF16), (4 * hw, F32), (LANES, F32)),
                                2 * hw)
    att = att.reshape(bsz, seq, 4 * hw)
    rest = rest.reshape(bsz, seq, 4 * hw)
    bias = _pad_lanes(_row(jnp.concatenate([fox_b_f, b_i, b_f])))
    gcol, rm = _prep_even(gates.reshape(bsz, seq, LANES), bias)
    out_a = _fox_attention(att, rest, gcol)
    out_b = _mlstm(att, rest, conv_w, _row(conv_b), gcol, rm, _row(mnorm_g))
    return out_a.reshape(-1, hw), out_b.reshape(-1, hw)


def _odd_mix(x2d, bsz, seq, norm_g, w_in, moba_qn_g, moba_kn_g, conv_w, conv_b, dt_bias, a_log, d_skip, snorm_g):
    hw = HALF
    nbc = SSD_GROUPS * SSD_STATE
    o = np.cumsum([0, hw, hw, hw, hw, hw, nbc, nbc, SSD_HEADS])
    w = jnp.concatenate([w_in[:, :o[7]], _pad_lanes(w_in[:, o[7]:o[8]])], axis=1).astype(BF16)
    gain = jnp.concatenate([jnp.tile(moba_qn_g, ATT_HEADS) * ATT_Q_SCALE, jnp.tile(moba_kn_g, ATT_HEADS)])
    moba_u, zx, bcin, dts = _in_proj(x2d, _row(norm_g), w, _row(gain),
                                     ((3 * hw, BF16), (2 * hw, F32), (2 * nbc, F32), (LANES, F32)), 2 * hw)
    gcol, rs = _prep_odd(dts.reshape(bsz, seq, LANES), _pad_lanes(_row(dt_bias)), _pad_lanes(_row(a_log)))
    out_c = _moba_attention(moba_u.reshape(bsz, seq, 3 * hw))
    out_d = _ssd(zx.reshape(bsz, seq, 2 * hw), bcin.reshape(bsz, seq, 2 * nbc), conv_w, _row(conv_b), gcol,
                 _ssd_expand_matrices(), rs, _row(jnp.repeat(d_skip, SSD_HEAD_DIM)), _row(snorm_g))
    return out_c.reshape(-1, hw), out_d.reshape(-1, hw)


def _layer_tail(x2d, mix_a, mix_b, w_out, norm_g, w_group, b_group, w_router, b_router, w_gate, w_up, w_down,
                p2d, ple_w_proj, ple_w_gate, ple_gate_norm_g, ple_out_norm_g):
    d = x2d.shape[1]
    wr = _pad_lanes(jnp.concatenate([w_group, w_router], axis=1), MOE_ROUTER_ROWS).T
    wr_hi = wr.astype(BF16)
    wr_lo = (wr - wr_hi.astype(F32)).astype(BF16)
    rb = _pad_lanes(_row(jnp.concatenate([b_group, b_router])), MOE_ROUTER_ROWS).T
    wgu = jnp.concatenate([w_gate, w_up], axis=-1).astype(BF16)
    wgu = wgu.transpose(0, 2, 1, 3).reshape(MOE_GROUPS, d, MOE_EPG * 2 * MOE_HIDDEN)
    wd = w_down.reshape(MOE_GROUPS, MOE_EPG * MOE_HIDDEN, d).astype(BF16)
    return _moe(x2d, mix_a, mix_b, w_out.astype(BF16), _row(norm_g), wr_hi, wr_lo, rb, wgu, wd,
                p2d, ple_w_proj.astype(BF16), ple_w_gate.astype(BF16), _row(ple_gate_norm_g), _row(ple_out_norm_g))


def kernel(x, p, norm1_g, norm2_g, ev_w_in, ev_fox_b_f, ev_fox_qn_g, ev_fox_kn_g, ev_mlstm_conv_w, ev_mlstm_conv_b, ev_mlstm_b_i, ev_mlstm_b_f, ev_mlstm_norm_g, ev_w_out, od_w_in, od_moba_qn_g, od_moba_kn_g, od_ssd_conv_w, od_ssd_conv_b, od_ssd_dt_bias, od_ssd_A_log, od_ssd_D, od_ssd_norm_g, od_w_out, moe_w_group, moe_b_group, moe_w_router, moe_b_router, moe_w_gate, moe_w_up, moe_w_down, ple_w_proj, ple_w_gate, ple_gate_norm_g, ple_out_norm_g):
    bsz, seq, d = x.shape
    depth = p.shape[0]
    x2d = x.reshape(bsz * seq, d)
    for i in range(depth):
        j = i // 2
        if i % 2 == 0:
            mix_a, mix_b = _even_mix(x2d, bsz, seq, norm1_g[i], ev_w_in[j], ev_fox_b_f[j], ev_fox_qn_g[j],
                                     ev_fox_kn_g[j], ev_mlstm_conv_w[j], ev_mlstm_conv_b[j], ev_mlstm_b_i[j],
                                     ev_mlstm_b_f[j], ev_mlstm_norm_g[j])
            w_out = ev_w_out[j]
        else:
            mix_a, mix_b = _odd_mix(x2d, bsz, seq, norm1_g[i], od_w_in[j], od_moba_qn_g[j], od_moba_kn_g[j],
                                    od_ssd_conv_w[j], od_ssd_conv_b[j], od_ssd_dt_bias[j], od_ssd_A_log[j],
                                    od_ssd_D[j], od_ssd_norm_g[j])
            w_out = od_w_out[j]
        x2d = _layer_tail(x2d, mix_a, mix_b, w_out, norm2_g[i], moe_w_group[i], moe_b_group[i], moe_w_router[i],
                          moe_b_router[i], moe_w_gate[i], moe_w_up[i], moe_w_down[i], p[i].reshape(bsz * seq, -1),
                          ple_w_proj[i], ple_w_gate[i], ple_gate_norm_g[i], ple_out_norm_g[i])
    return x2d.reshape(bsz, seq, d)
```

```python
import functools

import jax
import jax.numpy as jnp
import numpy as np
from jax import lax
from jax.experimental import pallas as pl
from jax.experimental.pallas import tpu as pltpu

F32 = jnp.float32
BF16 = jnp.bfloat16
MIX_DTYPE = BF16
NEG_INF = float("-inf")
LOG2E = 1.4426950408889634

NORM_EPS = 1e-6
D_MODEL = 1024
HALF = D_MODEL // 2
ATT_HEAD_DIM = 64
ATT_HEADS = HALF // ATT_HEAD_DIM
ATT_PAIRS = ATT_HEADS // 2
ATT_Q_SCALE = ATT_HEAD_DIM ** -0.5 * LOG2E
MLSTM_HEAD_DIM = 128
MLSTM_HEADS = HALF // MLSTM_HEAD_DIM
SSD_HEAD_DIM = 64
SSD_HEADS = HALF // SSD_HEAD_DIM
SSD_GROUPS = 2
SSD_STATE = 128
SSD_GROUP_W = HALF // SSD_GROUPS
SSD_HEADS_PER_GROUP = SSD_HEADS // SSD_GROUPS
CONV_WIDTH = 4
MOBA_BLOCK = 256
MOBA_TOPK = 3
MOE_GROUPS = 4
MOE_EPG = 4
MOE_EXPERTS = MOE_GROUPS * MOE_EPG
MOE_HIDDEN = D_MODEL // 4
MOE_ROUTER_ROWS = 32
MOE_CHUNK = 256
MOE_SMALL_CHUNK = 64
PLE_DIM = 256

LANES = 128
MXU_N = 256
SEQ_TILE = 256
ATT_TILE = 512
MIB = 1024 * 1024


def _cparams(sem, vmem_mib):
    return pltpu.CompilerParams(dimension_semantics=sem, vmem_limit_bytes=vmem_mib * MIB)


def _split2(x):
    hi = x.astype(BF16)
    lo = (x - hi.astype(F32)).astype(BF16)
    return hi, lo


def _split3(x):
    hi = x.astype(BF16)
    r = x - hi.astype(F32)
    mid = r.astype(BF16)
    lo = (r - mid.astype(F32)).astype(BF16)
    return hi, mid, lo


def _dot(a, b):
    return jnp.dot(a, b, preferred_element_type=F32)


def _dot_nt(a, b):
    return lax.dot_general(a, b, (((1,), (1,)), ((), ())), preferred_element_type=F32)


def _log_sigmoid(x):
    return jnp.minimum(x, 0.0) - jnp.log1p(jnp.exp(-jnp.abs(x)))


def _softplus(x):
    return jnp.maximum(x, 0.0) + jnp.log1p(jnp.exp(-jnp.abs(x)))


def _rms_rows(xf, g, eps=NORM_EPS):
    ms = jnp.mean(xf * xf, axis=-1, keepdims=True)
    return xf * lax.rsqrt(ms + eps) * g


def _block_index(i, block):
    shift = block.bit_length() - 1
    assert 1 << shift == block
    return lax.shift_right_logical(i, shift)


def _lane_col(x, idx):
    lane = lax.broadcasted_iota(jnp.int32, (1, x.shape[1]), 1)
    return jnp.sum(jnp.where(lane == idx, x, 0.0), axis=1, keepdims=True)


def _in_proj_kernel(x_ref, g_ref, w_ref, gain_ref, *refs, widths, n_norm, head_dim):
    out_refs, raw_sc = refs[:-1], refs[-1]
    h = _rms_rows(x_ref[...], g_ref[...]).astype(BF16)
    off = 0
    for o_ref, wdt in zip(out_refs, widths):
        for c0 in range(0, wdt, MXU_N):
            n = min(MXU_N, wdt - c0)
            y = _dot(h, w_ref[:, off + c0: off + c0 + n])
            if off + c0 < n_norm:
                raw_sc[:, off + c0: off + c0 + n] = y
            else:
                o_ref[:, c0:c0 + n] = y.astype(o_ref.dtype)
        off += wdt
    r = _block_index(lax.broadcasted_iota(jnp.int32, (MXU_N, MXU_N), 0), head_dim)
    c = _block_index(lax.broadcasted_iota(jnp.int32, (MXU_N, MXU_N), 1), head_dim)
    seg = jnp.where(r == c, 1.0, 0.0).astype(BF16)
    assert n_norm <= widths[0] and n_norm % MXU_N == 0
    for c0 in range(0, n_norm, MXU_N):
        y = raw_sc[:, c0:c0 + MXU_N]
        ss = _dot((y * y).astype(BF16), seg)
        y = y * lax.rsqrt(ss * (1.0 / head_dim) + NORM_EPS) * gain_ref[:, c0:c0 + MXU_N]
        out_refs[0][:, c0:c0 + MXU_N] = y.astype(out_refs[0].dtype)


def _in_proj(x2d, g, w, gain, segments, n_norm, tm=512):
    t, d = x2d.shape
    n = w.shape[1]
    widths = tuple(wd for wd, _ in segments)
    kern = functools.partial(_in_proj_kernel, widths=widths, n_norm=n_norm, head_dim=ATT_HEAD_DIM)
    return pl.pallas_call(
        kern,
        grid=(t // tm,),
        in_specs=[
            pl.BlockSpec((tm, d), lambda i: (i, 0)),
            pl.BlockSpec((1, d), lambda i: (0, 0), pipeline_mode=pl.Buffered(1)),
            pl.BlockSpec((d, n), lambda i: (0, 0), pipeline_mode=pl.Buffered(1)),
            pl.BlockSpec((1, n_norm), lambda i: (0, 0), pipeline_mode=pl.Buffered(1)),
        ],
        out_specs=[pl.BlockSpec((tm, wd), lambda i: (i, 0)) for wd in widths],
        out_shape=[jax.ShapeDtypeStruct((t, wd), dt) for wd, dt in segments],
        scratch_shapes=[pltpu.VMEM((tm, n_norm), F32)],
        compiler_params=_cparams(("parallel",), 52),
        name="in_proj",
    )(x2d, g, w, gain)


def _tri_cumsum(vals, tri):
    hi, mid, lo = _split3(vals)
    return _dot(tri, hi) + _dot(tri, mid) + _dot(tri, lo)


def _lower_tri(n):
    r = lax.broadcasted_iota(jnp.int32, (n, n), 0)
    c = lax.broadcasted_iota(jnp.int32, (n, n), 1)
    return jnp.where(c <= r, 1.0, 0.0).astype(BF16)


def _prep_even_kernel(raw_ref, bias_ref, g_ref, rm_ref):
    s = raw_ref.shape[0]
    ch = SEQ_TILE
    lane = lax.broadcasted_iota(jnp.int32, (1, LANES), 1)
    is_cum = (lane < ATT_HEADS) | ((lane >= ATT_HEADS + MLSTM_HEADS) & (lane < ATT_HEADS + 2 * MLSTM_HEADS))
    tri = _lower_tri(ch)
    carry = jnp.zeros((1, LANES), F32)
    pad = jnp.zeros((6, ch), F32)
    for c in range(s // ch):
        z = raw_ref[c * ch:(c + 1) * ch, :] + bias_ref[...]
        cs = _tri_cumsum(jnp.where(is_cum, _log_sigmoid(z), 0.0), tri) + carry
        carry = cs[ch - 1:ch, :]
        g = jnp.where(is_cum, cs, z)
        g_ref[c * ch:(c + 1) * ch, :] = g
        gt = g.T
        for hh in range(MLSTM_HEADS):
            i_row = gt[ATT_HEADS + hh:ATT_HEADS + hh + 1]
            f_row = gt[ATT_HEADS + MLSTM_HEADS + hh:ATT_HEADS + MLSTM_HEADS + hh + 1]
            rm_ref[hh, :, c * ch:(c + 1) * ch] = jnp.concatenate([i_row - f_row, f_row, pad], axis=0)


def _prep_even(raw, bias):
    b, s, _ = raw.shape
    return pl.pallas_call(
        _prep_even_kernel,
        grid=(b,),
        in_specs=[pl.BlockSpec((None, s, LANES), lambda i: (i, 0, 0)),
                  pl.BlockSpec((1, LANES), lambda i: (0, 0))],
        out_specs=[pl.BlockSpec((None, s, LANES), lambda i: (i, 0, 0)),
                   pl.BlockSpec((None, MLSTM_HEADS, 8, s), lambda i: (i, 0, 0, 0))],
        out_shape=[jax.ShapeDtypeStruct((b, s, LANES), F32),
                   jax.ShapeDtypeStruct((b, MLSTM_HEADS, 8, s), F32)],
        compiler_params=_cparams(("parallel",), 32),
        name="prep_even",
    )(raw, bias)


def _prep_odd_kernel(raw_ref, bias_ref, alog_ref, g_ref, rs_ref):
    s = raw_ref.shape[0]
    ch = SEQ_TILE
    lane = lax.broadcasted_iota(jnp.int32, (1, LANES), 1)
    is_dt = lane < SSD_HEADS
    tri = _lower_tri(ch)
    a_row = jnp.where(is_dt, -jnp.exp(alog_ref[...]), 0.0)
    pad = jnp.zeros((8 - SSD_HEADS_PER_GROUP, ch), F32)
    for c in range(s // ch):
        dt = jnp.where(is_dt, _softplus(raw_ref[c * ch:(c + 1) * ch, :] + bias_ref[...]), 0.0)
        acum = _tri_cumsum(a_row * dt, tri)
        g = jnp.where(is_dt, dt, pltpu.roll(acum, SSD_HEADS, 1))
        g_ref[c * ch:(c + 1) * ch, :] = g
        gt = g.T
        for gi in range(SSD_GROUPS):
            lo = SSD_HEADS + gi * SSD_HEADS_PER_GROUP
            rs_ref[gi, :, c * ch:(c + 1) * ch] = jnp.concatenate([gt[lo:lo + SSD_HEADS_PER_GROUP], pad], axis=0)


def _prep_odd(raw, bias, alog):
    b, s, _ = raw.shape
    return pl.pallas_call(
        _prep_odd_kernel,
        grid=(b,),
        in_specs=[pl.BlockSpec((None, s, LANES), lambda i: (i, 0, 0)),
                  pl.BlockSpec((1, LANES), lambda i: (0, 0)),
                  pl.BlockSpec((1, LANES), lambda i: (0, 0))],
        out_specs=[pl.BlockSpec((None, s, LANES), lambda i: (i, 0, 0)),
                   pl.BlockSpec((None, SSD_GROUPS, 8, s), lambda i: (i, 0, 0, 0))],
        out_shape=[jax.ShapeDtypeStruct((b, s, LANES), F32),
                   jax.ShapeDtypeStruct((b, SSD_GROUPS, 8, s), F32)],
        compiler_params=_cparams(("parallel",), 32),
        name="prep_odd",
    )(raw, bias, alog)


def _two_pass_softmax_pv(n_past, past_span, past_logits, own_span, own_logits, v_rows, s_scr, tq):
    heads = range(2)
    neg = jnp.full((1, tq), NEG_INF, F32)

    def store_max(span, s_t, hh, m):
        s_scr[hh, pl.ds(*span), :] = s_t
        return jnp.maximum(m, jnp.max(s_t, axis=0, keepdims=True))

    def accumulate(span, hh, m, carry):
        l, acc = carry
        p = jnp.exp2(s_scr[hh, pl.ds(*span), :] - m)
        return l + jnp.sum(p, axis=0, keepdims=True), acc + _dot(v_rows(*span, hh), p.astype(BF16))

    ms = [neg, neg]
    for j in range(n_past):
        ms = [store_max(past_span(j), past_logits(j, hh), hh, ms[hh]) for hh in heads]
    ms = [store_max(own_span, own_logits(hh), hh, ms[hh]) for hh in heads]
    zero = (jnp.zeros((1, tq), F32), jnp.zeros((ATT_HEAD_DIM, tq), F32))
    carry = [accumulate(own_span, hh, ms[hh], zero) for hh in heads]
    for j in range(n_past):
        carry = [accumulate(past_span(j), hh, ms[hh], carry[hh]) for hh in heads]
    return jnp.concatenate([acc / l for l, acc in carry], axis=0).T


def _stage_v(v_ref, vt):
    for c in range(v_ref.shape[0] // SEQ_TILE):
        vt[:, c * SEQ_TILE:(c + 1) * SEQ_TILE] = v_ref[c * SEQ_TILE:(c + 1) * SEQ_TILE, :].astype(F32).T.astype(BF16)


def _head_queries(q2):
    lane = lax.broadcasted_iota(jnp.int32, (1, LANES), 1)
    return [jnp.where(lane < ATT_HEAD_DIM, q2, 0.0), jnp.where(lane >= ATT_HEAD_DIM, q2, 0.0)]


def _fox_kernel(q_ref, k_ref, v_ref, fo_ref, g_ref, o_ref, vt, cfb, s_scr):
    hp = pl.program_id(1)
    tq = s_scr.shape[2]
    s_len = k_ref.shape[0]
    hd = ATT_HEAD_DIM
    _stage_v(v_ref, vt)
    g = g_ref[...]
    for hh in range(2):
        cfb[hh] = jnp.broadcast_to(_lane_col(g, 2 * hp + hh) * LOG2E, (s_len, LANES))
    krow = lax.broadcasted_iota(jnp.int32, (tq, tq), 0)
    qcol = lax.broadcasted_iota(jnp.int32, (tq, tq), 1)

    def span(j):
        return j * tq, tq

    for qi in range(s_len // tq):
        rows = slice(qi * tq, (qi + 1) * tq)
        qh = _head_queries(q_ref[rows, :])

        def logits(j, hh, qh=qh):
            s_t = _dot_nt(k_ref[pl.ds(*span(j)), :], qh[hh])
            return s_t - jnp.concatenate([cfb[hh, pl.ds(*span(j)), :]] * (tq // LANES), axis=1)

        out = _two_pass_softmax_pv(
            qi, span, logits, span(qi), lambda hh, qi=qi, logits=logits: jnp.where(krow <= qcol, logits(qi, hh), NEG_INF),
            lambda st, size, hh: vt[hh * hd:(hh + 1) * hd, pl.ds(st, size)], s_scr, tq)
        o_ref[rows, :] = (out * jax.nn.sigmoid(fo_ref[rows, :])).astype(o_ref.dtype)


def _fox_attention(qkv, ogate, gcol, tq=ATT_TILE):
    b, s, _ = qkv.shape
    np_ = ATT_PAIRS
    return pl.pallas_call(
        _fox_kernel,
        grid=(b, np_),
        in_specs=[
            pl.BlockSpec((None, s, LANES), lambda bi, hp: (bi, 0, hp)),
            pl.BlockSpec((None, s, LANES), lambda bi, hp: (bi, 0, np_ + hp)),
            pl.BlockSpec((None, s, LANES), lambda bi, hp: (bi, 0, 2 * np_ + hp)),
            pl.BlockSpec((None, s, LANES), lambda bi, hp: (bi, 0, hp)),
            pl.BlockSpec((None, s, LANES), lambda bi, hp: (bi, 0, 0)),
        ],
        out_specs=pl.BlockSpec((None, s, LANES), lambda bi, hp: (bi, 0, hp)),
        out_shape=jax.ShapeDtypeStruct((b, s, HALF), MIX_DTYPE),
        scratch_shapes=[pltpu.VMEM((LANES, s), BF16), pltpu.VMEM((2, s, LANES), F32),
                        pltpu.VMEM((2, s, tq), F32)],
        compiler_params=_cparams(("parallel", "parallel"), 40),
        name="fox_attention",
    )(qkv, qkv, qkv, ogate, gcol)


def _moba_kernel(q_ref, k_ref, v_ref, o_ref, vt, kmh, kml, s_scr):
    tq = s_scr.shape[2]
    s_len = k_ref.shape[0]
    blk = MOBA_BLOCK
    nb = s_len // blk
    nbp = kmh.shape[0]
    hd = ATT_HEAD_DIM
    per_tile = tq // blk
    _stage_v(v_ref, vt)
    means = [jnp.mean(k_ref[n * blk:(n + 1) * blk, :].astype(F32), axis=0, keepdims=True) for n in range(nb)]
    km = jnp.concatenate(means + [jnp.zeros((nbp - nb, LANES), F32)], axis=0)
    kmh[...], kml[...] = _split2(km)

    qcol1 = lax.broadcasted_iota(jnp.int32, (1, tq), 1)
    nrow = lax.broadcasted_iota(jnp.int32, (nbp, tq), 0)
    krow = lax.broadcasted_iota(jnp.int32, (tq, tq), 0)
    qcol = lax.broadcasted_iota(jnp.int32, (tq, tq), 1)
    same_blk = _block_index(krow, blk) == _block_index(qcol, blk)

    def v_rows(st, size, hh):
        return vt[hh * hd:(hh + 1) * hd, pl.ds(st, size)]

    def span(j):
        return j * tq, tq

    for qi in range(s_len // tq):
        rows = slice(qi * tq, (qi + 1) * tq)
        qh = _head_queries(q_ref[rows, :])
        valid = nrow < per_tile * qi + _block_index(qcol1, blk)
        selb = []
        for hh in range(2):
            gate = _dot_nt(kmh[...], qh[hh]) + _dot_nt(kml[...], qh[hh])
            gate = jnp.where(valid, gate, NEG_INF)
            rank = jnp.zeros((nbp, tq), F32)
            for m in range(nb):
                gm = gate[m:m + 1, :]
                rank = rank + jnp.where((gm > gate) | ((gm == gate) & (nrow > m)), 1.0, 0.0)
            selb.append(jnp.where((rank < float(MOBA_TOPK)) & valid, 0.0, NEG_INF))

        def sel_rows(j, hh):
            return jnp.concatenate(
                [jnp.broadcast_to(selb[hh][per_tile * j + n:per_tile * j + n + 1, :], (blk, tq))
                 for n in range(per_tile)], axis=0)

        def own_logits(hh):
            s_t = _dot_nt(k_ref[pl.ds(*span(qi)), :], qh[hh])
            return s_t + jnp.where(krow <= qcol, jnp.where(same_blk, 0.0, sel_rows(qi, hh)), NEG_INF)

        def past_logits(j, hh):
            return _dot_nt(k_ref[pl.ds(*span(j)), :], qh[hh]) + sel_rows(j, hh)

        out = _two_pass_softmax_pv(qi, span, past_logits, span(qi), own_logits, v_rows, s_scr, tq)
        o_ref[rows, :] = out.astype(o_ref.dtype)


def _moba_attention(qkv, tq=ATT_TILE):
    b, s, _ = qkv.shape
    np_ = ATT_PAIRS
    nb = s // MOBA_BLOCK
    nbp = -(-nb // 16) * 16
    return pl.pallas_call(
        _moba_kernel,
        grid=(b, np_),
        in_specs=[
            pl.BlockSpec((None, s, LANES), lambda bi, hp: (bi, 0, hp)),
            pl.BlockSpec((None, s, LANES), lambda bi, hp: (bi, 0, np_ + hp)),
            pl.BlockSpec((None, s, LANES), lambda bi, hp: (bi, 0, 2 * np_ + hp)),
        ],
        out_specs=pl.BlockSpec((None, s, LANES), lambda bi, hp: (bi, 0, hp)),
        out_shape=jax.ShapeDtypeStruct((b, s, HALF), MIX_DTYPE),
        scratch_shapes=[pltpu.VMEM((LANES, s), BF16),
                        pltpu.VMEM((nbp, LANES), BF16), pltpu.VMEM((nbp, LANES), BF16),
                        pltpu.VMEM((2, s, tq), F32)],
        compiler_params=_cparams(("parallel", "parallel"), 40),
        name="moba_attention",
    )(qkv, qkv, qkv)


def _conv_silu(x_ref, w_ref, b_ref, o_ref, scale):
    s_len, ch = x_ref.shape
    lc = SEQ_TILE
    w = w_ref[...]
    b = b_ref[...]
    row = lax.broadcasted_iota(jnp.int32, (8, ch), 0)
    for c in range(s_len // lc):
        cur = x_ref[c * lc:(c + 1) * lc, :]
        tail = x_ref[c * lc - 8:c * lc, :] if c > 0 else jnp.zeros((8, ch), F32)
        y = b
        for j in range(CONV_WIDTH - 1):
            sh = CONV_WIDTH - 1 - j
            rolled = pltpu.roll(cur, sh, 0)
            head = jnp.where(row < sh, pltpu.roll(tail, sh, 0), rolled[:8])
            y = y + jnp.concatenate([head, rolled[8:]], axis=0) * w[j:j + 1]
        y = y + cur * w[CONV_WIDTH - 1:CONV_WIDTH]
        y = y * jax.nn.sigmoid(y)
        o_ref[c * lc:(c + 1) * lc, :] = y * scale if scale != 1.0 else y


def _mlstm_kernel(q_ref, k_ref, v_ref, og_ref, cwq_ref, cwk_ref, cbq_ref, cbk_ref, g_ref, r_ref, ng_ref,
                  o_ref, qc, kc):
    hh = pl.program_id(1)
    s_len, d = q_ref.shape
    ln = SEQ_TILE
    _conv_silu(q_ref, cwq_ref, cbq_ref, qc, 1.0)
    _conv_silu(k_ref, cwk_ref, cbk_ref, kc, d ** -0.5)
    row = lax.broadcasted_iota(jnp.int32, (ln, ln), 0)
    col = lax.broadcasted_iota(jnp.int32, (ln, ln), 1)
    tri = col <= row
    lane = lax.broadcasted_iota(jnp.int32, (1, LANES), 1)
    ones_blk = jnp.broadcast_to(jnp.where(lane == 0, 1.0, 0.0), (ln, LANES)).astype(BF16)
    f_lane = ATT_HEADS + MLSTM_HEADS + hh
    c_prev = jnp.zeros((d, 2 * d), F32)
    m_prev = jnp.zeros((1, 1), F32)
    for c in range(s_len // ln):
        rows = slice(c * ln, (c + 1) * ln)
        q = qc[rows, :]
        k = kc[rows, :]
        f_col = _lane_col(g_ref[rows, :], f_lane)
        a_row = r_ref[0:1, rows]
        a_mat = jnp.where(tri, a_row, NEG_INF)
        m_col = jnp.maximum(m_prev, jnp.max(a_mat, axis=1, keepdims=True))
        qb = q.astype(BF16)
        w_qk = _dot_nt(qb, k.astype(BF16)) * jnp.exp(a_mat - m_col)
        v_aug = jnp.concatenate([v_ref[rows, :], ones_blk], axis=1)
        inter = jnp.exp(m_prev - m_col)
        q_c = _dot(qb, c_prev.astype(BF16))
        num = _dot(w_qk.astype(BF16), v_aug[:, :d]) + inter * q_c[:, :d]
        den = jnp.sum(w_qk, axis=1, keepdims=True) + inter * q_c[:, d:d + 1]
        h = num / jnp.maximum(jnp.abs(den), jnp.exp(-f_col - m_col))
        m_end = jnp.maximum(m_prev, jnp.max(a_row, axis=1, keepdims=True))
        k_w = (k.T * jnp.exp(a_row - m_end)).astype(BF16)
        c_prev = jnp.exp(m_prev - m_end) * c_prev + _dot(k_w, v_aug)
        m_prev = m_end
        hn = _rms_rows(h, ng_ref[...])
        o_ref[rows, :] = (hn * jax.nn.sigmoid(og_ref[rows, :])).astype(o_ref.dtype)


def _mlstm(att, rest, conv_w, conv_b, gcol, grow, norm_g):
    b, s, _ = att.shape
    nh = MLSTM_HEADS
    d = MLSTM_HEAD_DIM
    big = lambda off: pl.BlockSpec((None, s, d), lambda bi, h: (bi, 0, off + h))
    return pl.pallas_call(
        _mlstm_kernel,
        grid=(b, nh),
        in_specs=[
            big(nh), big(2 * nh), big(3 * nh), big(3 * nh),
            pl.BlockSpec((CONV_WIDTH, d), lambda bi, h: (0, h)),
            pl.BlockSpec((CONV_WIDTH, d), lambda bi, h: (0, nh + h)),
            pl.BlockSpec((1, d), lambda bi, h: (0, h)),
            pl.BlockSpec((1, d), lambda bi, h: (0, nh + h)),
            pl.BlockSpec((None, s, LANES), lambda bi, h: (bi, 0, 0)),
            pl.BlockSpec((None, None, 8, s), lambda bi, h: (bi, h, 0, 0)),
            pl.BlockSpec((1, d), lambda bi, h: (0, h)),
        ],
        out_specs=pl.BlockSpec((None, s, d), lambda bi, h: (bi, 0, h)),
        out_shape=jax.ShapeDtypeStruct((b, s, HALF), MIX_DTYPE),
        scratch_shapes=[pltpu.VMEM((s, d), F32), pltpu.VMEM((s, d), F32)],
        compiler_params=_cparams(("parallel", "parallel"), 40),
        name="mlstm",
    )(rest, rest, att, rest, conv_w, conv_w, conv_b, conv_b, gcol, grow, norm_g)


def _ssd_kernel(z_ref, x_ref, b_ref, c_ref, cwx_ref, cwb_ref, cwc_ref, cbx_ref, cbb_ref, cbc_ref,
                g_ref, e_ref, r_ref, dsk_ref, ng_ref, o_ref, xc, bc, cc):
    s_len, gw = x_ref.shape
    ln = SEQ_TILE
    _conv_silu(x_ref, cwx_ref, cbx_ref, xc, 1.0)
    _conv_silu(b_ref, cwb_ref, cbb_ref, bc, 1.0)
    _conv_silu(c_ref, cwc_ref, cbc_ref, cc, 1.0)
    row = lax.broadcasted_iota(jnp.int32, (ln, ln), 0)
    col = lax.broadcasted_iota(jnp.int32, (ln, ln), 1)
    tri = col <= row
    head_of_lane = _block_index(lax.broadcasted_iota(jnp.int32, (1, gw), 1), SSD_HEAD_DIM)
    e_dt2 = jnp.concatenate([e_ref[0], e_ref[0]], axis=0)
    e_ac2 = jnp.concatenate([e_ref[1], e_ref[1]], axis=0)
    prev = jnp.zeros((b_ref.shape[1], gw), F32)
    for c in range(s_len // ln):
        rows = slice(c * ln, (c + 1) * ln)
        x = xc[rows, :]
        bm = bc[rows, :]
        cm = cc[rows, :]
        z = z_ref[rows, :]
        hi, mid, lo = _split3(g_ref[rows, :])
        hi_mid = jnp.concatenate([hi, mid], axis=1)
        dt_e = _dot(hi_mid, e_dt2) + _dot(lo, e_ref[0])
        ac_e = _dot(hi_mid, e_ac2) + _dot(lo, e_ref[1])
        xdt = x * dt_e
        xdt_b = xdt.astype(BF16)
        cmb = cm.astype(BF16)
        cb = _dot_nt(cmb, bm.astype(BF16))
        ac_rows = r_ref[0:SSD_HEADS_PER_GROUP, rows]
        y = jnp.zeros((ln, gw), F32)
        for hh in range(SSD_HEADS_PER_GROUP):
            ac_col = ac_e[:, hh * SSD_HEAD_DIM:hh * SSD_HEAD_DIM + 1]
            l_mat = jnp.exp(jnp.where(tri, ac_col - ac_rows[hh:hh + 1], NEG_INF))
            y = jnp.where(head_of_lane == hh, _dot((cb * l_mat).astype(BF16), xdt_b), y)
        ac_end = ac_e[ln - 1:ln, :]
        y = y + _dot(cmb, prev.astype(BF16)) * jnp.exp(ac_e)
        states = _dot(bm.T.astype(BF16), (xdt * jnp.exp(ac_end - ac_e)).astype(BF16))
        prev = prev * jnp.exp(ac_end) + states
        y = y + dsk_ref[...] * x
        y = y * (z * jax.nn.sigmoid(z))
        o_ref[rows, :] = _rms_rows(y, ng_ref[...]).astype(o_ref.dtype)


def _ssd(zx, bcin, conv_w, conv_b, gcol, expand, grow, d_row, norm_g):
    b, s, _ = zx.shape
    gw = SSD_GROUP_W
    ns = SSD_STATE
    ng = SSD_GROUPS
    xoff = HALF // gw
    boff = HALF // ns
    coff = (HALF + ng * ns) // ns
    return pl.pallas_call(
        _ssd_kernel,
        grid=(b, ng),
        in_specs=[
            pl.BlockSpec((None, s, gw), lambda bi, g: (bi, 0, g)),
            pl.BlockSpec((None, s, gw), lambda bi, g: (bi, 0, xoff + g)),
            pl.BlockSpec((None, s, ns), lambda bi, g: (bi, 0, g)),
            pl.BlockSpec((None, s, ns), lambda bi, g: (bi, 0, ng + g)),
            pl.BlockSpec((CONV_WIDTH, gw), lambda bi, g: (0, g)),
            pl.BlockSpec((CONV_WIDTH, ns), lambda bi, g: (0, boff + g)),
            pl.BlockSpec((CONV_WIDTH, ns), lambda bi, g: (0, coff + g)),
            pl.BlockSpec((1, gw), lambda bi, g: (0, g)),
            pl.BlockSpec((1, ns), lambda bi, g: (0, boff + g)),
            pl.BlockSpec((1, ns), lambda bi, g: (0, coff + g)),
            pl.BlockSpec((None, s, LANES), lambda bi, g: (bi, 0, 0)),
            pl.BlockSpec((None, 2, LANES, gw), lambda bi, g: (g, 0, 0, 0)),
            pl.BlockSpec((None, None, 8, s), lambda bi, g: (bi, g, 0, 0)),
            pl.BlockSpec((1, gw), lambda bi, g: (0, g)),
            pl.BlockSpec((1, gw), lambda bi, g: (0, g)),
        ],
        out_specs=pl.BlockSpec((None, s, gw), lambda bi, g: (bi, 0, g)),
        out_shape=jax.ShapeDtypeStruct((b, s, HALF), MIX_DTYPE),
        scratch_shapes=[pltpu.VMEM((s, gw), F32), pltpu.VMEM((s, ns), F32), pltpu.VMEM((s, ns), F32)],
        compiler_params=_cparams(("parallel", "parallel"), 48),
        name="ssd",
    )(zx, zx, bcin, bcin, conv_w, conv_w, conv_w, conv_b, conv_b, conv_b, gcol, expand, grow, d_row, norm_g)


def _ssd_expand_matrices():
    e = np.zeros((SSD_GROUPS, 2, LANES, SSD_GROUP_W), np.float32)
    for g in range(SSD_GROUPS):
        for h in range(SSD_HEADS_PER_GROUP):
            head = g * SSD_HEADS_PER_GROUP + h
            e[g, 0, head, h * SSD_HEAD_DIM:(h + 1) * SSD_HEAD_DIM] = 1.0
            e[g, 1, SSD_HEADS + head, h * SSD_HEAD_DIM:(h + 1) * SSD_HEAD_DIM] = 1.0
    return jnp.asarray(e, BF16)


def _moe_route(logits_t):
    gl = [logits_t[g:g + 1, :] for g in range(MOE_GROUPS)]
    g_max = functools.reduce(jnp.maximum, gl)
    g_den = sum(jnp.exp(x - g_max) for x in gl)
    g_w = 1.0 / g_den
    taken = jnp.zeros_like(g_max) > 1.0
    is_g = []
    for g in range(MOE_GROUPS):
        hit = (gl[g] == g_max) & jnp.logical_not(taken)
        is_g.append(hit)
        taken = taken | hit
    e_in = []
    for j in range(MOE_EPG):
        v = jnp.zeros_like(g_max)
        for g in range(MOE_GROUPS):
            row = MOE_GROUPS + g * MOE_EPG + j
            v = jnp.where(is_g[g], logits_t[row:row + 1, :], v)
        e_in.append(v)
    rank = []
    for j in range(MOE_EPG):
        r = jnp.zeros_like(g_max)
        for m in range(MOE_EPG):
            if m == j:
                continue
            ahead = (e_in[m] > e_in[j]) | ((e_in[m] == e_in[j]) & (m < j))
            r = r + jnp.where(ahead, 1.0, 0.0)
        rank.append(r)
    v0 = sum(jnp.where(rank[j] == 0.0, e_in[j], 0.0) for j in range(MOE_EPG))
    v1 = sum(jnp.where(rank[j] == 1.0, e_in[j], 0.0) for j in range(MOE_EPG))
    e1 = jnp.exp(v1 - v0)
    w0 = 1.0 / (1.0 + e1)
    w1 = e1 / (1.0 + e1)
    comb = []
    for g in range(MOE_GROUPS):
        for j in range(MOE_EPG):
            wj = jnp.where(rank[j] == 0.0, w0, jnp.where(rank[j] == 1.0, w1, 0.0))
            comb.append(jnp.where(is_g[g], g_w * wj, 0.0))
    return comb, is_g


ROUTE_GROUP, ROUTE_CHUNK, ROUTE_SLOT = 0, 1, 2


def _moe_kernel(x_ref, ma_ref, mb_ref, wo_ref, g_ref, wrh_ref, wrl_ref, rb_ref, tri_ref, wgu_ref, wd_ref,
                p_ref, wp_ref, wg_ref, g1_ref, g2_ref, o_ref, h_sc, rows_sc, cols_sc, cw_sc, cnt_sc):
    grp = pl.program_id(1)
    tm, d = x_ref.shape
    ch = MOE_CHUNK

    @pl.when(grp == 0)
    def _():
        half = ma_ref.shape[1]
        x = x_ref[...] + _dot(ma_ref[...], wo_ref[:half, :]) + _dot(mb_ref[...], wo_ref[half:, :])
        h_hi, h_lo = _split2(_rms_rows(x, g_ref[...]))
        h_sc[...] = h_hi
        logits_t = (_dot_nt(wrh_ref[...], h_hi) + _dot_nt(wrh_ref[...], h_lo) + _dot_nt(wrl_ref[...], h_hi)
                    + rb_ref[...])
        comb, is_g = _moe_route(logits_t)
        member = jnp.concatenate([jnp.where(m, 1.0, 0.0) for m in is_g]
                                 + [jnp.zeros((16 - MOE_GROUPS, tm), F32)], axis=0)
        incl = _dot(member.astype(BF16), tri_ref[...])
        pos = sum(jnp.where(is_g[g], incl[g:g + 1, :] - 1.0, 0.0) for g in range(MOE_GROUPS))
        gid = sum(jnp.where(is_g[g], float(g), 0.0) for g in range(MOE_GROUPS))
        chunk = jnp.floor(pos * (1.0 / ch))
        rows = jnp.concatenate([gid, chunk, pos - ch * chunk, jnp.zeros((LANES - 3, tm), F32)], axis=0)
        rows_sc[...] = rows[:rows_sc.shape[0]]
        cols_sc[...] = rows.T
        parts = _split3(jnp.concatenate(comb, axis=0))
        cw_rows = jnp.concatenate([p.astype(F32) for p in parts]
                                  + [jnp.zeros((LANES - 3 * MOE_EXPERTS, tm), F32)], axis=0)
        cw_sc[...] = cw_rows.T.astype(BF16)
        cnt_sc[...] = jnp.broadcast_to(incl[:, tm - 1:tm], cnt_sc.shape)
        o_ref[...] = x

    lane = lax.broadcasted_iota(jnp.int32, (1, LANES), 1)
    expert_of_lane = jnp.where(lane < 3 * MOE_EXPERTS, lane & (MOE_EXPERTS - 1), -1)
    grp_row = lax.broadcasted_iota(jnp.int32, cnt_sc.shape, 0) == grp
    n_grp = jnp.sum(jnp.where(grp_row, cnt_sc[...], 0.0), axis=0, keepdims=True)[0, 0].astype(jnp.int32)
    grp_f = grp.astype(F32)

    def run_chunk(c, size):
        rows = rows_sc[...]
        cols = cols_sc[...]
        in_chunk = (rows[ROUTE_GROUP:ROUTE_GROUP + 1, :] == grp_f) & (rows[ROUTE_CHUNK:ROUTE_CHUNK + 1, :] == c)
        slot = lax.broadcasted_iota(jnp.int32, (size, tm), 0).astype(F32)
        gather = jnp.where(in_chunk & (rows[ROUTE_SLOT:ROUTE_SLOT + 1, :] == slot), 1.0, 0.0).astype(BF16)
        in_chunk_t = (cols[:, ROUTE_GROUP:ROUTE_GROUP + 1] == grp_f) & (cols[:, ROUTE_CHUNK:ROUTE_CHUNK + 1] == c)
        slot_t = lax.broadcasted_iota(jnp.int32, (tm, size), 1).astype(F32)
        scatter = jnp.where(in_chunk_t & (cols[:, ROUTE_SLOT:ROUTE_SLOT + 1] == slot_t), 1.0, 0.0).astype(BF16)
        hs = _dot(gather, h_sc[...]).astype(BF16)
        table = _dot(gather, cw_sc[...])
        ab = _dot(hs, wgu_ref[...])
        hid = []
        for j in range(MOE_EPG):
            cw = jnp.sum(jnp.where(expert_of_lane == MOE_EPG * grp + j, table, 0.0), axis=1, keepdims=True)
            a = ab[:, 2 * j * MOE_HIDDEN:(2 * j + 1) * MOE_HIDDEN]
            hid.append(((a * jax.nn.sigmoid(a)) * ab[:, (2 * j + 1) * MOE_HIDDEN:(2 * j + 2) * MOE_HIDDEN]
                        * cw).astype(BF16))
        y = _dot(jnp.concatenate(hid, axis=1), wd_ref[...])
        o_ref[...] += _dot(scatter, y.astype(BF16))

    for c in range(tm // ch):
        left = n_grp - c * ch
        pl.when(left > MOE_SMALL_CHUNK)(functools.partial(run_chunk, c, ch))
        pl.when((left > 0) & (left <= MOE_SMALL_CHUNK))(functools.partial(run_chunk, c, MOE_SMALL_CHUNK))

    @pl.when(grp == pl.num_programs(1) - 1)
    def _():
        x2 = o_ref[...]
        e = _dot(p_ref[...].astype(BF16), wp_ref[...])
        gate = jax.nn.sigmoid(_dot(_rms_rows(x2, g1_ref[...]).astype(BF16), wg_ref[...]))
        o_ref[...] = x2 + _rms_rows(e * gate, g2_ref[...])


def _moe(x2d, mix_a, mix_b, w_out, g, wr_hi, wr_lo, rb, wgu, wd, p2d, wp, wg, g1, g2, tm=1024):
    t, d = x2d.shape
    kp = p2d.shape[1]
    half = mix_a.shape[1]
    tri = jnp.triu(jnp.ones((tm, tm), BF16))
    const = lambda shape: pl.BlockSpec(shape, lambda i, gi: (0,) * len(shape), pipeline_mode=pl.Buffered(1))
    return pl.pallas_call(
        _moe_kernel,
        grid=(t // tm, MOE_GROUPS),
        in_specs=[pl.BlockSpec((tm, d), lambda i, gi: (i, 0)),
                  pl.BlockSpec((tm, half), lambda i, gi: (i, 0)),
                  pl.BlockSpec((tm, half), lambda i, gi: (i, 0)),
                  const((2 * half, d)),
                  const((1, d)), const((MOE_ROUTER_ROWS, d)), const((MOE_ROUTER_ROWS, d)),
                  const((MOE_ROUTER_ROWS, 1)), const((tm, tm)),
                  pl.BlockSpec((None, d, MOE_EPG * 2 * MOE_HIDDEN), lambda i, gi: (gi, 0, 0)),
                  pl.BlockSpec((None, MOE_EPG * MOE_HIDDEN, d), lambda i, gi: (gi, 0, 0)),
                  pl.BlockSpec((tm, kp), lambda i, gi: (i, 0)),
                  const((kp, d)), const((d, d)), const((1, d)), const((1, d))],
        out_specs=pl.BlockSpec((tm, d), lambda i, gi: (i, 0)),
        out_shape=jax.ShapeDtypeStruct((t, d), F32),
        scratch_shapes=[pltpu.VMEM((tm, d), BF16), pltpu.VMEM((8, tm), F32), pltpu.VMEM((tm, LANES), F32),
                        pltpu.VMEM((tm, LANES), BF16), pltpu.VMEM((16, LANES), F32)],
        compiler_params=_cparams(("parallel", "arbitrary"), 56),
        name="moe",
    )(x2d, mix_a, mix_b, w_out, g, wr_hi, wr_lo, rb, tri, wgu, wd, p2d, wp, wg, g1, g2)


def _pad_lanes(cols, width=LANES):
    return jnp.pad(cols, ((0, 0), (0, width - cols.shape[-1])))


def _row(v):
    return v.reshape(1, -1).astype(F32)


def _even_mix(x2d, bsz, seq, norm_g, w_in, fox_b_f, fox_qn_g, fox_kn_g, conv_w, conv_b, b_i, b_f, mnorm_g):
    hw = HALF
    o = np.cumsum([0, hw, hw, hw, ATT_HEADS, hw, hw, hw, hw, MLSTM_HEADS, MLSTM_HEADS, hw])
    seg = lambda i: w_in[:, o[i]:o[i + 1]]
    fq, fk, fv, ff, fo, mq, mk, mv, mi, mf, mo = [seg(i) for i in range(11)]
    w = jnp.concatenate([fq, fk, fv, mv, fo, mq, mk, mo, _pad_lanes(jnp.concatenate([ff, mi, mf], axis=1))],
                        axis=1).astype(BF16)
    gain = jnp.concatenate([jnp.tile(fox_qn_g, ATT_HEADS) * ATT_Q_SCALE, jnp.tile(fox_kn_g, ATT_HEADS)])
    att, rest, gates = _in_proj(x2d, _row(norm_g), w, _row(gain), ((4 * hw, BF16), (4 * hw, F32), (LANES, F32)),
                                2 * hw)
    att = att.reshape(bsz, seq, 4 * hw)
    rest = rest.reshape(bsz, seq, 4 * hw)
    bias = _pad_lanes(_row(jnp.concatenate([fox_b_f, b_i, b_f])))
    gcol, rm = _prep_even(gates.reshape(bsz, seq, LANES), bias)
    out_a = _fox_attention(att, rest, gcol)
    out_b = _mlstm(att, rest, conv_w, _row(conv_b), gcol, rm, _row(mnorm_g))
    return out_a.reshape(-1, hw), out_b.reshape(-1, hw)


def _odd_mix(x2d, bsz, seq, norm_g, w_in, moba_qn_g, moba_kn_g, conv_w, conv_b, dt_bias, a_log, d_skip, snorm_g):
    hw = HALF
    nbc = SSD_GROUPS * SSD_STATE
    o = np.cumsum([0, hw, hw, hw, hw, hw, nbc, nbc, SSD_HEADS])
    w = jnp.concatenate([w_in[:, :o[7]], _pad_lanes(w_in[:, o[7]:o[8]])], axis=1).astype(BF16)
    gain = jnp.concatenate([jnp.tile(moba_qn_g, ATT_HEADS) * ATT_Q_SCALE, jnp.tile(moba_kn_g, ATT_HEADS)])
    moba_u, zx, bcin, dts = _in_proj(x2d, _row(norm_g), w, _row(gain),
                                     ((3 * hw, BF16), (2 * hw, F32), (2 * nbc, F32), (LANES, F32)), 2 * hw)
    gcol, rs = _prep_odd(dts.reshape(bsz, seq, LANES), _pad_lanes(_row(dt_bias)), _pad_lanes(_row(a_log)))
    out_c = _moba_attention(moba_u.reshape(bsz, seq, 3 * hw))
    out_d = _ssd(zx.reshape(bsz, seq, 2 * hw), bcin.reshape(bsz, seq, 2 * nbc), conv_w, _row(conv_b), gcol,
                 _ssd_expand_matrices(), rs, _row(jnp.repeat(d_skip, SSD_HEAD_DIM)), _row(snorm_g))
    return out_c.reshape(-1, hw), out_d.reshape(-1, hw)


def _layer_tail(x2d, mix_a, mix_b, w_out, norm_g, w_group, b_group, w_router, b_router, w_gate, w_up, w_down,
                p2d, ple_w_proj, ple_w_gate, ple_gate_norm_g, ple_out_norm_g):
    d = x2d.shape[1]
    wr = _pad_lanes(jnp.concatenate([w_group, w_router], axis=1), MOE_ROUTER_ROWS).T
    wr_hi = wr.astype(BF16)
    wr_lo = (wr - wr_hi.astype(F32)).astype(BF16)
    rb = _pad_lanes(_row(jnp.concatenate([b_group, b_router])), MOE_ROUTER_ROWS).T
    wgu = jnp.concatenate([w_gate, w_up], axis=-1).astype(BF16)
    wgu = wgu.transpose(0, 2, 1, 3).reshape(MOE_GROUPS, d, MOE_EPG * 2 * MOE_HIDDEN)
    wd = w_down.reshape(MOE_GROUPS, MOE_EPG * MOE_HIDDEN, d).astype(BF16)
    return _moe(x2d, mix_a, mix_b, w_out.astype(BF16), _row(norm_g), wr_hi, wr_lo, rb, wgu, wd,
                p2d, ple_w_proj.astype(BF16), ple_w_gate.astype(BF16), _row(ple_gate_norm_g), _row(ple_out_norm_g))


def kernel(x, p, norm1_g, norm2_g, ev_w_in, ev_fox_b_f, ev_fox_qn_g, ev_fox_kn_g, ev_mlstm_conv_w, ev_mlstm_conv_b, ev_mlstm_b_i, ev_mlstm_b_f, ev_mlstm_norm_g, ev_w_out, od_w_in, od_moba_qn_g, od_moba_kn_g, od_ssd_conv_w, od_ssd_conv_b, od_ssd_dt_bias, od_ssd_A_log, od_ssd_D, od_ssd_norm_g, od_w_out, moe_w_group, moe_b_group, moe_w_router, moe_b_router, moe_w_gate, moe_w_up, moe_w_down, ple_w_proj, ple_w_gate, ple_gate_norm_g, ple_out_norm_g):
    bsz, seq, d = x.shape
    depth = p.shape[0]
    x2d = x.reshape(bsz * seq, d)
    for i in range(depth):
        j = i // 2
        if i % 2 == 0:
            mix_a, mix_b = _even_mix(x2d, bsz, seq, norm1_g[i], ev_w_in[j], ev_fox_b_f[j], ev_fox_qn_g[j],
                                     ev_fox_kn_g[j], ev_mlstm_conv_w[j], ev_mlstm_conv_b[j], ev_mlstm_b_i[j],
                                     ev_mlstm_b_f[j], ev_mlstm_norm_g[j])
            w_out = ev_w_out[j]
        else:
            mix_a, mix_b = _odd_mix(x2d, bsz, seq, norm1_g[i], od_w_in[j], od_moba_qn_g[j], od_moba_kn_g[j],
                                    od_ssd_conv_w[j], od_ssd_conv_b[j], od_ssd_dt_bias[j], od_ssd_A_log[j],
                                    od_ssd_D[j], od_ssd_norm_g[j])
            w_out = od_w_out[j]
        x2d = _layer_tail(x2d, mix_a, mix_b, w_out, norm2_g[i], moe_w_group[i], moe_b_group[i], moe_w_router[i],
                          moe_b_router[i], moe_w_gate[i], moe_w_up[i], moe_w_down[i], p[i].reshape(bsz * seq, -1),
                          ple_w_proj[i], ple_w_gate[i], ple_gate_norm_g[i], ple_out_norm_g[i])
    return x2d.reshape(bsz, seq, d)
```

```python
import functools

import jax
import jax.numpy as jnp
import numpy as np
from jax import lax
from jax.experimental import pallas as pl
from jax.experimental.pallas import tpu as pltpu

F32 = jnp.float32
BF16 = jnp.bfloat16
MIX_DTYPE = BF16
NEG_INF = float("-inf")
MASKED_LOGIT = -2.0 ** 100
LOG2E = 1.4426950408889634

NORM_EPS = 1e-6
D_MODEL = 1024
HALF = D_MODEL // 2
ATT_HEAD_DIM = 64
ATT_HEADS = HALF // ATT_HEAD_DIM
ATT_PAIRS = ATT_HEADS // 2
ATT_Q_SCALE = ATT_HEAD_DIM ** -0.5 * LOG2E
ATT_V_ROWS = ATT_HEAD_DIM + 16
MLSTM_HEAD_DIM = 128
MLSTM_HEADS = HALF // MLSTM_HEAD_DIM
SSD_HEAD_DIM = 64
SSD_HEADS = HALF // SSD_HEAD_DIM
SSD_GROUPS = 2
SSD_STATE = 128
SSD_GROUP_W = HALF // SSD_GROUPS
SSD_HEADS_PER_GROUP = SSD_HEADS // SSD_GROUPS
CONV_WIDTH = 4
MOBA_BLOCK = 256
MOBA_TOPK = 3
MOE_GROUPS = 4
MOE_EPG = 4
MOE_EXPERTS = MOE_GROUPS * MOE_EPG
MOE_HIDDEN = D_MODEL // 4
MOE_ROUTER_ROWS = 32
MOE_CHUNK = 256
MOE_SMALL_CHUNK = 64

LANES = 128
MXU_N = 256
MIB = 1024 * 1024

SEQ_TILE = 256
ATT_TILE = 512
IN_PROJ_TILE = 512
TAIL_TILE = 1024
VMEM_LIMIT_MIB = {"in_proj": 52, "prep": 32, "attention": 40, "mlstm": 40, "ssd": 48, "tail": 56}


def _cparams(sem, call):
    return pltpu.CompilerParams(dimension_semantics=sem, vmem_limit_bytes=VMEM_LIMIT_MIB[call] * MIB)


def _split2(x):
    hi = x.astype(BF16)
    lo = (x - hi.astype(F32)).astype(BF16)
    return hi, lo


def _split3(x):
    hi = x.astype(BF16)
    r = x - hi.astype(F32)
    mid = r.astype(BF16)
    lo = (r - mid.astype(F32)).astype(BF16)
    return hi, mid, lo


def _dot(a, b):
    return jnp.dot(a, b, preferred_element_type=F32)


def _dot_nt(a, b):
    return lax.dot_general(a, b, (((1,), (1,)), ((), ())), preferred_element_type=F32)


def _log_sigmoid(x):
    return jnp.minimum(x, 0.0) - jnp.log1p(jnp.exp(-jnp.abs(x)))


def _softplus(x):
    return jnp.maximum(x, 0.0) + jnp.log1p(jnp.exp(-jnp.abs(x)))


def _rms_rows(xf, g, eps=NORM_EPS):
    ms = jnp.mean(xf * xf, axis=-1, keepdims=True)
    return xf * lax.rsqrt(ms + eps) * g


def _block_index(i, block):
    shift = block.bit_length() - 1
    assert 1 << shift == block
    return lax.shift_right_logical(i, shift)


def _lane_col(x, idx):
    lane = lax.broadcasted_iota(jnp.int32, (1, x.shape[1]), 1)
    return jnp.sum(jnp.where(lane == idx, x, 0.0), axis=1, keepdims=True)


def _in_proj_kernel(x_ref, g_ref, w_ref, gain_ref, *refs, widths, n_norm, head_dim):
    out_refs, raw_sc = refs[:-1], refs[-1]
    h = _rms_rows(x_ref[...], g_ref[...]).astype(BF16)
    off = 0
    for o_ref, wdt in zip(out_refs, widths):
        for c0 in range(0, wdt, MXU_N):
            n = min(MXU_N, wdt - c0)
            y = _dot(h, w_ref[:, off + c0: off + c0 + n])
            if off + c0 < n_norm:
                raw_sc[:, off + c0: off + c0 + n] = y
            else:
                o_ref[:, c0:c0 + n] = y.astype(o_ref.dtype)
        off += wdt
    r = _block_index(lax.broadcasted_iota(jnp.int32, (MXU_N, MXU_N), 0), head_dim)
    c = _block_index(lax.broadcasted_iota(jnp.int32, (MXU_N, MXU_N), 1), head_dim)
    seg = jnp.where(r == c, 1.0, 0.0).astype(BF16)
    assert n_norm <= widths[0] and n_norm % MXU_N == 0
    for c0 in range(0, n_norm, MXU_N):
        y = raw_sc[:, c0:c0 + MXU_N]
        ss = _dot((y * y).astype(BF16), seg)
        y = y * lax.rsqrt(ss * (1.0 / head_dim) + NORM_EPS) * gain_ref[:, c0:c0 + MXU_N]
        out_refs[0][:, c0:c0 + MXU_N] = y.astype(out_refs[0].dtype)


def _in_proj(x2d, g, w, gain, segments, n_norm, tm=IN_PROJ_TILE):
    t, d = x2d.shape
    n = w.shape[1]
    widths = tuple(wd for wd, _ in segments)
    kern = functools.partial(_in_proj_kernel, widths=widths, n_norm=n_norm, head_dim=ATT_HEAD_DIM)
    return pl.pallas_call(
        kern,
        grid=(t // tm,),
        in_specs=[
            pl.BlockSpec((tm, d), lambda i: (i, 0)),
            pl.BlockSpec((1, d), lambda i: (0, 0), pipeline_mode=pl.Buffered(1)),
            pl.BlockSpec((d, n), lambda i: (0, 0), pipeline_mode=pl.Buffered(1)),
            pl.BlockSpec((1, n_norm), lambda i: (0, 0), pipeline_mode=pl.Buffered(1)),
        ],
        out_specs=[pl.BlockSpec((tm, wd), lambda i: (i, 0)) for wd in widths],
        out_shape=[jax.ShapeDtypeStruct((t, wd), dt) for wd, dt in segments],
        scratch_shapes=[pltpu.VMEM((tm, n_norm), F32)],
        compiler_params=_cparams(("parallel",), "in_proj"),
        name="in_proj",
    )(x2d, g, w, gain)


def _tri_cumsum(vals, tri):
    hi, mid, lo = _split3(vals)
    return _dot(tri, hi) + _dot(tri, mid) + _dot(tri, lo)


def _lower_tri(n):
    r = lax.broadcasted_iota(jnp.int32, (n, n), 0)
    c = lax.broadcasted_iota(jnp.int32, (n, n), 1)
    return jnp.where(c <= r, 1.0, 0.0).astype(BF16)


def _prep_even_kernel(raw_ref, bias_ref, g_ref, rm_ref):
    s = raw_ref.shape[0]
    ch = SEQ_TILE
    lane = lax.broadcasted_iota(jnp.int32, (1, LANES), 1)
    is_cum = (lane < ATT_HEADS) | ((lane >= ATT_HEADS + MLSTM_HEADS) & (lane < ATT_HEADS + 2 * MLSTM_HEADS))
    tri = _lower_tri(ch)
    carry = jnp.zeros((1, LANES), F32)
    pad = jnp.zeros((6, ch), F32)
    for c in range(s // ch):
        z = raw_ref[c * ch:(c + 1) * ch, :] + bias_ref[...]
        cs = _tri_cumsum(jnp.where(is_cum, _log_sigmoid(z), 0.0), tri) + carry
        carry = cs[ch - 1:ch, :]
        g = jnp.where(is_cum, cs, z)
        g_ref[c * ch:(c + 1) * ch, :] = g
        gt = g.T
        for hh in range(MLSTM_HEADS):
            i_row = gt[ATT_HEADS + hh:ATT_HEADS + hh + 1]
            f_row = gt[ATT_HEADS + MLSTM_HEADS + hh:ATT_HEADS + MLSTM_HEADS + hh + 1]
            rm_ref[hh, :, c * ch:(c + 1) * ch] = jnp.concatenate([i_row - f_row, f_row, pad], axis=0)


def _prep_even(raw, bias):
    b, s, _ = raw.shape
    return pl.pallas_call(
        _prep_even_kernel,
        grid=(b,),
        in_specs=[pl.BlockSpec((None, s, LANES), lambda i: (i, 0, 0)),
                  pl.BlockSpec((1, LANES), lambda i: (0, 0))],
        out_specs=[pl.BlockSpec((None, s, LANES), lambda i: (i, 0, 0)),
                   pl.BlockSpec((None, MLSTM_HEADS, 8, s), lambda i: (i, 0, 0, 0))],
        out_shape=[jax.ShapeDtypeStruct((b, s, LANES), F32),
                   jax.ShapeDtypeStruct((b, MLSTM_HEADS, 8, s), F32)],
        compiler_params=_cparams(("parallel",), "prep"),
        name="prep_even",
    )(raw, bias)


def _prep_odd_kernel(raw_ref, bias_ref, alog_ref, g_ref, rs_ref):
    s = raw_ref.shape[0]
    ch = SEQ_TILE
    lane = lax.broadcasted_iota(jnp.int32, (1, LANES), 1)
    is_dt = lane < SSD_HEADS
    tri = _lower_tri(ch)
    a_row = jnp.where(is_dt, -jnp.exp(alog_ref[...]), 0.0)
    pad = jnp.zeros((8 - SSD_HEADS_PER_GROUP, ch), F32)
    for c in range(s // ch):
        dt = jnp.where(is_dt, _softplus(raw_ref[c * ch:(c + 1) * ch, :] + bias_ref[...]), 0.0)
        acum = _tri_cumsum(a_row * dt, tri)
        g = jnp.where(is_dt, dt, pltpu.roll(acum, SSD_HEADS, 1))
        g_ref[c * ch:(c + 1) * ch, :] = g
        gt = g.T
        for gi in range(SSD_GROUPS):
            lo = SSD_HEADS + gi * SSD_HEADS_PER_GROUP
            rs_ref[gi, :, c * ch:(c + 1) * ch] = jnp.concatenate([gt[lo:lo + SSD_HEADS_PER_GROUP], pad], axis=0)


def _prep_odd(raw, bias, alog):
    b, s, _ = raw.shape
    return pl.pallas_call(
        _prep_odd_kernel,
        grid=(b,),
        in_specs=[pl.BlockSpec((None, s, LANES), lambda i: (i, 0, 0)),
                  pl.BlockSpec((1, LANES), lambda i: (0, 0)),
                  pl.BlockSpec((1, LANES), lambda i: (0, 0))],
        out_specs=[pl.BlockSpec((None, s, LANES), lambda i: (i, 0, 0)),
                   pl.BlockSpec((None, SSD_GROUPS, 8, s), lambda i: (i, 0, 0, 0))],
        out_shape=[jax.ShapeDtypeStruct((b, s, LANES), F32),
                   jax.ShapeDtypeStruct((b, SSD_GROUPS, 8, s), F32)],
        compiler_params=_cparams(("parallel",), "prep"),
        name="prep_odd",
    )(raw, bias, alog)


def _two_pass_softmax_pv(n_past, past_span, past_logits, own_span, own_logits, v_rows, s_scr, tq):
    heads = range(2)
    hd = ATT_HEAD_DIM
    neg = jnp.full((1, tq), NEG_INF, F32)

    def store_max(span, s_t, hh, m):
        s_scr[hh, pl.ds(*span), :] = s_t
        return jnp.maximum(m, jnp.max(s_t, axis=0, keepdims=True))

    def accumulate(span, hh, m, acc):
        p = jnp.exp2(s_scr[hh, pl.ds(*span), :] - m)
        return acc + _dot(v_rows(*span, hh), p.astype(BF16))

    ms = [neg, neg]
    for j in range(n_past):
        ms = [store_max(past_span(j), past_logits(j, hh), hh, ms[hh]) for hh in heads]
    ms = [store_max(own_span, own_logits(hh), hh, ms[hh]) for hh in heads]
    acc = [accumulate(own_span, hh, ms[hh], jnp.zeros((ATT_V_ROWS, tq), F32)) for hh in heads]
    for j in range(n_past):
        acc = [accumulate(past_span(j), hh, ms[hh], acc[hh]) for hh in heads]
    return jnp.concatenate([a[:hd] / a[hd:hd + 1] for a in acc], axis=0).T


def _stage_v(v_ref, vt):
    hd = ATT_HEAD_DIM
    pad = ATT_V_ROWS - hd
    for c in range(v_ref.shape[0] // SEQ_TILE):
        cols = slice(c * SEQ_TILE, (c + 1) * SEQ_TILE)
        v_t = v_ref[cols, :].astype(F32).T
        ones_row = jnp.where(lax.broadcasted_iota(jnp.int32, (pad, SEQ_TILE), 0) == 0, 1.0, 0.0)
        for hh in range(2):
            vt[hh * ATT_V_ROWS:(hh + 1) * ATT_V_ROWS, cols] = jnp.concatenate(
                [v_t[hh * hd:(hh + 1) * hd], ones_row], axis=0).astype(BF16)


def _head_queries(q2):
    lane = lax.broadcasted_iota(jnp.int32, (1, LANES), 1)
    return [jnp.where(lane < ATT_HEAD_DIM, q2, 0.0), jnp.where(lane >= ATT_HEAD_DIM, q2, 0.0)]


def _fox_kernel(q_ref, k_ref, v_ref, fo_ref, g_ref, o_ref, vt, kaug, s_scr):
    hp = pl.program_id(1)
    tq = s_scr.shape[2]
    s_len = k_ref.shape[0]
    _stage_v(v_ref, vt)
    lane = lax.broadcasted_iota(jnp.int32, (1, LANES), 1)
    g = g_ref[...]
    k2 = k_ref[...]
    spare, own = [], []
    for hh in range(2):
        own.append((lane < ATT_HEAD_DIM) if hh == 0 else (lane >= ATT_HEAD_DIM))
        spare.append(ATT_HEAD_DIM if hh == 0 else 0)
        hi, mid, lo = [p.astype(F32) for p in _split3(_lane_col(g, 2 * hp + hh) * (-LOG2E))]
        parts = jnp.where(lane == spare[hh], hi, jnp.where(lane == spare[hh] + 1, mid,
                          jnp.where(lane == spare[hh] + 2, lo, 0.0)))
        kaug[hh] = jnp.where(own[hh], k2, parts.astype(k2.dtype))
    krow = lax.broadcasted_iota(jnp.int32, (tq, tq), 0)
    qcol = lax.broadcasted_iota(jnp.int32, (tq, tq), 1)

    def span(j):
        return j * tq, tq

    for qi in range(s_len // tq):
        rows = slice(qi * tq, (qi + 1) * tq)
        q2 = q_ref[rows, :]
        qh = [jnp.where(own[hh], q2, jnp.where((lane >= spare[hh]) & (lane < spare[hh] + 3), 1.0, 0.0)
                        .astype(q2.dtype)) for hh in range(2)]

        def logits(j, hh, qh=qh):
            return _dot_nt(kaug[hh, pl.ds(*span(j)), :], qh[hh])

        out = _two_pass_softmax_pv(
            qi, span, logits, span(qi), lambda hh, qi=qi, logits=logits: jnp.where(krow <= qcol, logits(qi, hh), NEG_INF),
            lambda st, size, hh: vt[hh * ATT_V_ROWS:(hh + 1) * ATT_V_ROWS, pl.ds(st, size)], s_scr, tq)
        o_ref[rows, :] = (out * jax.nn.sigmoid(fo_ref[rows, :])).astype(o_ref.dtype)


def _fox_attention(qkv, ogate, gcol, tq=ATT_TILE):
    b, s, _ = qkv.shape
    np_ = ATT_PAIRS
    return pl.pallas_call(
        _fox_kernel,
        grid=(b, np_),
        in_specs=[
            pl.BlockSpec((None, s, LANES), lambda bi, hp: (bi, 0, hp)),
            pl.BlockSpec((None, s, LANES), lambda bi, hp: (bi, 0, np_ + hp)),
            pl.BlockSpec((None, s, LANES), lambda bi, hp: (bi, 0, 2 * np_ + hp)),
            pl.BlockSpec((None, s, LANES), lambda bi, hp: (bi, 0, hp)),
            pl.BlockSpec((None, s, LANES), lambda bi, hp: (bi, 0, 0)),
        ],
        out_specs=pl.BlockSpec((None, s, LANES), lambda bi, hp: (bi, 0, hp)),
        out_shape=jax.ShapeDtypeStruct((b, s, HALF), MIX_DTYPE),
        scratch_shapes=[pltpu.VMEM((2 * ATT_V_ROWS, s), BF16), pltpu.VMEM((2, s, LANES), BF16),
                        pltpu.VMEM((2, s, tq), F32)],
        compiler_params=_cparams(("parallel", "parallel"), "attention"),
        name="fox_attention",
    )(qkv, qkv, qkv, ogate, gcol)


def _moba_kernel(q_ref, k_ref, v_ref, o_ref, vt, kmh, kml, kaug, s_scr):
    tq = s_scr.shape[2]
    s_len = k_ref.shape[0]
    blk = MOBA_BLOCK
    nb = s_len // blk
    nbp = kmh.shape[0]
    per_tile = tq // blk
    _stage_v(v_ref, vt)
    means = [jnp.mean(k_ref[n * blk:(n + 1) * blk, :].astype(F32), axis=0, keepdims=True) for n in range(nb)]
    km = jnp.concatenate(means + [jnp.zeros((nbp - nb, LANES), F32)], axis=0)
    kmh[...], kml[...] = _split2(km)
    lane = lax.broadcasted_iota(jnp.int32, (1, LANES), 1)
    key_blk = _block_index(lax.broadcasted_iota(jnp.int32, (s_len, LANES), 0), blk)
    k2 = k_ref[...]
    own = [lane < ATT_HEAD_DIM, lane >= ATT_HEAD_DIM]
    spare = [ATT_HEAD_DIM, 0]
    for hh in range(2):
        kaug[hh] = jnp.where(own[hh], k2, jnp.where(lane - spare[hh] == key_blk, 1.0, 0.0).astype(k2.dtype))

    qcol1 = lax.broadcasted_iota(jnp.int32, (1, tq), 1)
    nrow = lax.broadcasted_iota(jnp.int32, (nbp, tq), 0)
    krow = lax.broadcasted_iota(jnp.int32, (tq, tq), 0)
    qcol = lax.broadcasted_iota(jnp.int32, (tq, tq), 1)
    same_blk = _block_index(krow, blk) == _block_index(qcol, blk)

    def v_rows(st, size, hh):
        return vt[hh * ATT_V_ROWS:(hh + 1) * ATT_V_ROWS, pl.ds(st, size)]

    def span(j):
        return j * tq, tq

    for qi in range(s_len // tq):
        rows = slice(qi * tq, (qi + 1) * tq)
        q2 = q_ref[rows, :]
        qh = _head_queries(q2)
        valid = nrow < per_tile * qi + _block_index(qcol1, blk)
        selected, q_aug = [], []
        for hh in range(2):
            gate = _dot_nt(kmh[...], qh[hh]) + _dot_nt(kml[...], qh[hh])
            gate = jnp.where(valid, gate, NEG_INF)
            rank = jnp.zeros((nbp, tq), F32)
            for m in range(nb):
                gm = gate[m:m + 1, :]
                rank = rank + jnp.where((gm > gate) | ((gm == gate) & (nrow > m)), 1.0, 0.0)
            selected.append((rank < float(MOBA_TOPK)) & valid)
            pieces = [jnp.where(selected[hh], 0.0, MASKED_LOGIT), jnp.zeros((LANES - spare[hh] - nbp, tq), F32)]
            if spare[hh]:
                pieces.insert(0, jnp.zeros((spare[hh], tq), F32))
            bias_rows = jnp.concatenate(pieces, axis=0)
            q_aug.append(jnp.where(own[hh], q2, bias_rows.T.astype(q2.dtype)))

        def own_logits(hh):
            sel = jnp.concatenate(
                [jnp.broadcast_to(selected[hh][per_tile * qi + n:per_tile * qi + n + 1, :], (blk, tq))
                 for n in range(per_tile)], axis=0)
            s_t = _dot_nt(k_ref[pl.ds(*span(qi)), :], qh[hh])
            return jnp.where((krow <= qcol) & (same_blk | sel), s_t, NEG_INF)

        def past_logits(j, hh):
            return _dot_nt(kaug[hh, pl.ds(*span(j)), :], q_aug[hh])

        out = _two_pass_softmax_pv(qi, span, past_logits, span(qi), own_logits, v_rows, s_scr, tq)
        o_ref[rows, :] = out.astype(o_ref.dtype)


def _moba_attention(qkv, tq=ATT_TILE):
    b, s, _ = qkv.shape
    np_ = ATT_PAIRS
    nb = s // MOBA_BLOCK
    nbp = -(-nb // 16) * 16
    return pl.pallas_call(
        _moba_kernel,
        grid=(b, np_),
        in_specs=[
            pl.BlockSpec((None, s, LANES), lambda bi, hp: (bi, 0, hp)),
            pl.BlockSpec((None, s, LANES), lambda bi, hp: (bi, 0, np_ + hp)),
            pl.BlockSpec((None, s, LANES), lambda bi, hp: (bi, 0, 2 * np_ + hp)),
        ],
        out_specs=pl.BlockSpec((None, s, LANES), lambda bi, hp: (bi, 0, hp)),
        out_shape=jax.ShapeDtypeStruct((b, s, HALF), MIX_DTYPE),
        scratch_shapes=[pltpu.VMEM((2 * ATT_V_ROWS, s), BF16),
                        pltpu.VMEM((nbp, LANES), BF16), pltpu.VMEM((nbp, LANES), BF16),
                        pltpu.VMEM((2, s, LANES), BF16), pltpu.VMEM((2, s, tq), F32)],
        compiler_params=_cparams(("parallel", "parallel"), "attention"),
        name="moba_attention",
    )(qkv, qkv, qkv)


def _conv_silu(x_ref, w_ref, b_ref, o_ref, scale):
    s_len, ch = x_ref.shape
    lc = SEQ_TILE
    w = w_ref[...]
    b = b_ref[...]
    row = lax.broadcasted_iota(jnp.int32, (8, ch), 0)
    for c in range(s_len // lc):
        cur = x_ref[c * lc:(c + 1) * lc, :]
        tail = x_ref[c * lc - 8:c * lc, :] if c > 0 else jnp.zeros((8, ch), F32)
        y = b
        for j in range(CONV_WIDTH - 1):
            sh = CONV_WIDTH - 1 - j
            rolled = pltpu.roll(cur, sh, 0)
            head = jnp.where(row < sh, pltpu.roll(tail, sh, 0), rolled[:8])
            y = y + jnp.concatenate([head, rolled[8:]], axis=0) * w[j:j + 1]
        y = y + cur * w[CONV_WIDTH - 1:CONV_WIDTH]
        y = y * jax.nn.sigmoid(y)
        o_ref[c * lc:(c + 1) * lc, :] = y * scale if scale != 1.0 else y


def _mlstm_kernel(q_ref, k_ref, v_ref, og_ref, cwq_ref, cwk_ref, cbq_ref, cbk_ref, g_ref, r_ref, ng_ref,
                  o_ref, qc, kc):
    hh = pl.program_id(1)
    s_len, d = q_ref.shape
    ln = SEQ_TILE
    _conv_silu(q_ref, cwq_ref, cbq_ref, qc, 1.0)
    _conv_silu(k_ref, cwk_ref, cbk_ref, kc, d ** -0.5)
    row = lax.broadcasted_iota(jnp.int32, (ln, ln), 0)
    col = lax.broadcasted_iota(jnp.int32, (ln, ln), 1)
    tri = col <= row
    lane = lax.broadcasted_iota(jnp.int32, (1, LANES), 1)
    ones_blk = jnp.broadcast_to(jnp.where(lane == 0, 1.0, 0.0), (ln, LANES)).astype(BF16)
    f_lane = ATT_HEADS + MLSTM_HEADS + hh
    c_prev = jnp.zeros((d, 2 * d), F32)
    m_prev = jnp.zeros((1, 1), F32)
    for c in range(s_len // ln):
        rows = slice(c * ln, (c + 1) * ln)
        q = qc[rows, :]
        k = kc[rows, :]
        f_col = _lane_col(g_ref[rows, :], f_lane)
        a_row = r_ref[0:1, rows]
        a_mat = jnp.where(tri, a_row, NEG_INF)
        m_col = jnp.maximum(m_prev, jnp.max(a_mat, axis=1, keepdims=True))
        qb = q.astype(BF16)
        w_qk = _dot_nt(qb, k.astype(BF16)) * jnp.exp(a_mat - m_col)
        v_aug = jnp.concatenate([v_ref[rows, :], ones_blk], axis=1)
        inter = jnp.exp(m_prev - m_col)
        q_c = _dot(qb, c_prev.astype(BF16))
        num = _dot(w_qk.astype(BF16), v_aug[:, :d]) + inter * q_c[:, :d]
        den = jnp.sum(w_qk, axis=1, keepdims=True) + inter * q_c[:, d:d + 1]
        h = num / jnp.maximum(jnp.abs(den), jnp.exp(-f_col - m_col))
        m_end = jnp.maximum(m_prev, jnp.max(a_row, axis=1, keepdims=True))
        k_w = (k.T * jnp.exp(a_row - m_end)).astype(BF16)
        c_prev = jnp.exp(m_prev - m_end) * c_prev + _dot(k_w, v_aug)
        m_prev = m_end
        hn = _rms_rows(h, ng_ref[...])
        o_ref[rows, :] = (hn * jax.nn.sigmoid(og_ref[rows, :])).astype(o_ref.dtype)


def _mlstm(att, rest, conv_w, conv_b, gcol, grow, norm_g):
    b, s, _ = att.shape
    nh = MLSTM_HEADS
    d = MLSTM_HEAD_DIM
    big = lambda off: pl.BlockSpec((None, s, d), lambda bi, h: (bi, 0, off + h))
    return pl.pallas_call(
        _mlstm_kernel,
        grid=(b, nh),
        in_specs=[
            big(nh), big(2 * nh), big(3 * nh), big(3 * nh),
            pl.BlockSpec((CONV_WIDTH, d), lambda bi, h: (0, h)),
            pl.BlockSpec((CONV_WIDTH, d), lambda bi, h: (0, nh + h)),
            pl.BlockSpec((1, d), lambda bi, h: (0, h)),
            pl.BlockSpec((1, d), lambda bi, h: (0, nh + h)),
            pl.BlockSpec((None, s, LANES), lambda bi, h: (bi, 0, 0)),
            pl.BlockSpec((None, None, 8, s), lambda bi, h: (bi, h, 0, 0)),
            pl.BlockSpec((1, d), lambda bi, h: (0, h)),
        ],
        out_specs=pl.BlockSpec((None, s, d), lambda bi, h: (bi, 0, h)),
        out_shape=jax.ShapeDtypeStruct((b, s, HALF), MIX_DTYPE),
        scratch_shapes=[pltpu.VMEM((s, d), F32), pltpu.VMEM((s, d), F32)],
        compiler_params=_cparams(("parallel", "parallel"), "mlstm"),
        name="mlstm",
    )(rest, rest, att, rest, conv_w, conv_w, conv_b, conv_b, gcol, grow, norm_g)


def _ssd_kernel(z_ref, x_ref, b_ref, c_ref, cwx_ref, cwb_ref, cwc_ref, cbx_ref, cbb_ref, cbc_ref,
                g_ref, e_ref, r_ref, dsk_ref, ng_ref, o_ref, xc, bc, cc):
    s_len, gw = x_ref.shape
    ln = SEQ_TILE
    _conv_silu(x_ref, cwx_ref, cbx_ref, xc, 1.0)
    _conv_silu(b_ref, cwb_ref, cbb_ref, bc, 1.0)
    _conv_silu(c_ref, cwc_ref, cbc_ref, cc, 1.0)
    row = lax.broadcasted_iota(jnp.int32, (ln, ln), 0)
    col = lax.broadcasted_iota(jnp.int32, (ln, ln), 1)
    tri = col <= row
    head_of_lane = _block_index(lax.broadcasted_iota(jnp.int32, (1, gw), 1), SSD_HEAD_DIM)
    e_dt2 = jnp.concatenate([e_ref[0], e_ref[0]], axis=0)
    e_ac2 = jnp.concatenate([e_ref[1], e_ref[1]], axis=0)
    prev = jnp.zeros((b_ref.shape[1], gw), F32)
    for c in range(s_len // ln):
        rows = slice(c * ln, (c + 1) * ln)
        x = xc[rows, :]
        bm = bc[rows, :]
        cm = cc[rows, :]
        z = z_ref[rows, :]
        hi, mid, lo = _split3(g_ref[rows, :])
        hi_mid = jnp.concatenate([hi, mid], axis=1)
        dt_e = _dot(hi_mid, e_dt2) + _dot(lo, e_ref[0])
        ac_e = _dot(hi_mid, e_ac2) + _dot(lo, e_ref[1])
        xdt = x * dt_e
        xdt_b = xdt.astype(BF16)
        cmb = cm.astype(BF16)
        cb = _dot_nt(cmb, bm.astype(BF16))
        ac_rows = r_ref[0:SSD_HEADS_PER_GROUP, rows]
        y = jnp.zeros((ln, gw), F32)
        for hh in range(SSD_HEADS_PER_GROUP):
            ac_col = ac_e[:, hh * SSD_HEAD_DIM:hh * SSD_HEAD_DIM + 1]
            l_mat = jnp.exp(jnp.where(tri, ac_col - ac_rows[hh:hh + 1], NEG_INF))
            y = jnp.where(head_of_lane == hh, _dot((cb * l_mat).astype(BF16), xdt_b), y)
        ac_end = ac_e[ln - 1:ln, :]
        y = y + _dot(cmb, prev.astype(BF16)) * jnp.exp(ac_e)
        states = _dot(bm.T.astype(BF16), (xdt * jnp.exp(ac_end - ac_e)).astype(BF16))
        prev = prev * jnp.exp(ac_end) + states
        y = y + dsk_ref[...] * x
        y = y * (z * jax.nn.sigmoid(z))
        o_ref[rows, :] = _rms_rows(y, ng_ref[...]).astype(o_ref.dtype)


def _ssd(zx, bcin, conv_w, conv_b, gcol, expand, grow, d_row, norm_g):
    b, s, _ = zx.shape
    gw = SSD_GROUP_W
    ns = SSD_STATE
    ng = SSD_GROUPS
    xoff = HALF // gw
    boff = HALF // ns
    coff = (HALF + ng * ns) // ns
    return pl.pallas_call(
        _ssd_kernel,
        grid=(b, ng),
        in_specs=[
            pl.BlockSpec((None, s, gw), lambda bi, g: (bi, 0, g)),
            pl.BlockSpec((None, s, gw), lambda bi, g: (bi, 0, xoff + g)),
            pl.BlockSpec((None, s, ns), lambda bi, g: (bi, 0, g)),
            pl.BlockSpec((None, s, ns), lambda bi, g: (bi, 0, ng + g)),
            pl.BlockSpec((CONV_WIDTH, gw), lambda bi, g: (0, g)),
            pl.BlockSpec((CONV_WIDTH, ns), lambda bi, g: (0, boff + g)),
            pl.BlockSpec((CONV_WIDTH, ns), lambda bi, g: (0, coff + g)),
            pl.BlockSpec((1, gw), lambda bi, g: (0, g)),
            pl.BlockSpec((1, ns), lambda bi, g: (0, boff + g)),
            pl.BlockSpec((1, ns), lambda bi, g: (0, coff + g)),
            pl.BlockSpec((None, s, LANES), lambda bi, g: (bi, 0, 0)),
            pl.BlockSpec((None, 2, LANES, gw), lambda bi, g: (g, 0, 0, 0)),
            pl.BlockSpec((None, None, 8, s), lambda bi, g: (bi, g, 0, 0)),
            pl.BlockSpec((1, gw), lambda bi, g: (0, g)),
            pl.BlockSpec((1, gw), lambda bi, g: (0, g)),
        ],
        out_specs=pl.BlockSpec((None, s, gw), lambda bi, g: (bi, 0, g)),
        out_shape=jax.ShapeDtypeStruct((b, s, HALF), MIX_DTYPE),
        scratch_shapes=[pltpu.VMEM((s, gw), F32), pltpu.VMEM((s, ns), F32), pltpu.VMEM((s, ns), F32)],
        compiler_params=_cparams(("parallel", "parallel"), "ssd"),
        name="ssd",
    )(zx, zx, bcin, bcin, conv_w, conv_w, conv_w, conv_b, conv_b, conv_b, gcol, expand, grow, d_row, norm_g)


def _ssd_expand_matrices():
    e = np.zeros((SSD_GROUPS, 2, LANES, SSD_GROUP_W), np.float32)
    for g in range(SSD_GROUPS):
        for h in range(SSD_HEADS_PER_GROUP):
            head = g * SSD_HEADS_PER_GROUP + h
            e[g, 0, head, h * SSD_HEAD_DIM:(h + 1) * SSD_HEAD_DIM] = 1.0
            e[g, 1, SSD_HEADS + head, h * SSD_HEAD_DIM:(h + 1) * SSD_HEAD_DIM] = 1.0
    return jnp.asarray(e, BF16)


def _moe_route(logits_t):
    gl = [logits_t[g:g + 1, :] for g in range(MOE_GROUPS)]
    g_max = functools.reduce(jnp.maximum, gl)
    g_den = sum(jnp.exp(x - g_max) for x in gl)
    g_w = 1.0 / g_den
    taken = jnp.zeros_like(g_max) > 1.0
    is_g = []
    for g in range(MOE_GROUPS):
        hit = (gl[g] == g_max) & jnp.logical_not(taken)
        is_g.append(hit)
        taken = taken | hit
    e_in = []
    for j in range(MOE_EPG):
        v = jnp.zeros_like(g_max)
        for g in range(MOE_GROUPS):
            row = MOE_GROUPS + g * MOE_EPG + j
            v = jnp.where(is_g[g], logits_t[row:row + 1, :], v)
        e_in.append(v)
    rank = []
    for j in range(MOE_EPG):
        r = jnp.zeros_like(g_max)
        for m in range(MOE_EPG):
            if m == j:
                continue
            ahead = (e_in[m] > e_in[j]) | ((e_in[m] == e_in[j]) & (m < j))
            r = r + jnp.where(ahead, 1.0, 0.0)
        rank.append(r)
    v0 = sum(jnp.where(rank[j] == 0.0, e_in[j], 0.0) for j in range(MOE_EPG))
    v1 = sum(jnp.where(rank[j] == 1.0, e_in[j], 0.0) for j in range(MOE_EPG))
    e1 = jnp.exp(v1 - v0)
    w0 = 1.0 / (1.0 + e1)
    w1 = e1 / (1.0 + e1)
    comb = []
    for g in range(MOE_GROUPS):
        for j in range(MOE_EPG):
            wj = jnp.where(rank[j] == 0.0, w0, jnp.where(rank[j] == 1.0, w1, 0.0))
            comb.append(jnp.where(is_g[g], g_w * wj, 0.0))
    return comb, is_g


ROUTE_GROUP, ROUTE_CHUNK, ROUTE_SLOT = 0, 1, 2


def _moe_kernel(x_ref, ma_ref, mb_ref, wo_ref, g_ref, wrh_ref, wrl_ref, rb_ref, tri_ref, wgu_ref, wd_ref,
                p_ref, wp_ref, wg_ref, g1_ref, g2_ref, o_ref, h_sc, rows_sc, cols_sc, cw_sc, cnt_sc):
    grp = pl.program_id(1)
    tm, d = x_ref.shape
    ch = MOE_CHUNK

    @pl.when(grp == 0)
    def _():
        half = ma_ref.shape[1]
        x = x_ref[...] + _dot(ma_ref[...], wo_ref[:half, :]) + _dot(mb_ref[...], wo_ref[half:, :])
        h_hi, h_lo = _split2(_rms_rows(x, g_ref[...]))
        h_sc[...] = h_hi
        logits_t = (_dot_nt(wrh_ref[...], h_hi) + _dot_nt(wrh_ref[...], h_lo) + _dot_nt(wrl_ref[...], h_hi)
                    + rb_ref[...])
        comb, is_g = _moe_route(logits_t)
        member = jnp.concatenate([jnp.where(m, 1.0, 0.0) for m in is_g]
                                 + [jnp.zeros((16 - MOE_GROUPS, tm), F32)], axis=0)
        incl = _dot(member.astype(BF16), tri_ref[...])
        pos = sum(jnp.where(is_g[g], incl[g:g + 1, :] - 1.0, 0.0) for g in range(MOE_GROUPS))
        gid = sum(jnp.where(is_g[g], float(g), 0.0) for g in range(MOE_GROUPS))
        chunk = jnp.floor(pos * (1.0 / ch))
        rows = jnp.concatenate([gid, chunk, pos - ch * chunk, jnp.zeros((LANES - 3, tm), F32)], axis=0)
        rows_sc[...] = rows[:rows_sc.shape[0]]
        cols_sc[...] = rows.T
        parts = _split3(jnp.concatenate(comb, axis=0))
        cw_rows = jnp.concatenate([p.astype(F32) for p in parts]
                                  + [jnp.zeros((LANES - 3 * MOE_EXPERTS, tm), F32)], axis=0)
        cw_sc[...] = cw_rows.T.astype(BF16)
        cnt_sc[...] = jnp.broadcast_to(incl[:, tm - 1:tm], cnt_sc.shape)
        o_ref[...] = x

    lane = lax.broadcasted_iota(jnp.int32, (1, LANES), 1)
    expert_of_lane = jnp.where(lane < 3 * MOE_EXPERTS, lane & (MOE_EXPERTS - 1), -1)
    grp_row = lax.broadcasted_iota(jnp.int32, cnt_sc.shape, 0) == grp
    n_grp = jnp.sum(jnp.where(grp_row, cnt_sc[...], 0.0), axis=0, keepdims=True)[0, 0].astype(jnp.int32)
    grp_f = grp.astype(F32)

    def run_chunk(c, size):
        rows = rows_sc[...]
        cols = cols_sc[...]
        in_chunk = (rows[ROUTE_GROUP:ROUTE_GROUP + 1, :] == grp_f) & (rows[ROUTE_CHUNK:ROUTE_CHUNK + 1, :] == c)
        slot = lax.broadcasted_iota(jnp.int32, (size, tm), 0).astype(F32)
        gather = jnp.where(in_chunk & (rows[ROUTE_SLOT:ROUTE_SLOT + 1, :] == slot), 1.0, 0.0).astype(BF16)
        in_chunk_t = (cols[:, ROUTE_GROUP:ROUTE_GROUP + 1] == grp_f) & (cols[:, ROUTE_CHUNK:ROUTE_CHUNK + 1] == c)
        slot_t = lax.broadcasted_iota(jnp.int32, (tm, size), 1).astype(F32)
        scatter = jnp.where(in_chunk_t & (cols[:, ROUTE_SLOT:ROUTE_SLOT + 1] == slot_t), 1.0, 0.0).astype(BF16)
        hs = _dot(gather, h_sc[...]).astype(BF16)
        table = _dot(gather, cw_sc[...])
        ab = _dot(hs, wgu_ref[...])
        hid = []
        for j in range(MOE_EPG):
            cw = jnp.sum(jnp.where(expert_of_lane == MOE_EPG * grp + j, table, 0.0), axis=1, keepdims=True)
            a = ab[:, 2 * j * MOE_HIDDEN:(2 * j + 1) * MOE_HIDDEN]
            hid.append(((a * jax.nn.sigmoid(a)) * ab[:, (2 * j + 1) * MOE_HIDDEN:(2 * j + 2) * MOE_HIDDEN]
                        * cw).astype(BF16))
        y = _dot(jnp.concatenate(hid, axis=1), wd_ref[...])
        o_ref[...] += _dot(scatter, y.astype(BF16))

    for c in range(tm // ch):
        left = n_grp - c * ch
        pl.when(left > MOE_SMALL_CHUNK)(functools.partial(run_chunk, c, ch))
        pl.when((left > 0) & (left <= MOE_SMALL_CHUNK))(functools.partial(run_chunk, c, MOE_SMALL_CHUNK))

    @pl.when(grp == pl.num_programs(1) - 1)
    def _():
        x2 = o_ref[...]
        e = _dot(p_ref[...].astype(BF16), wp_ref[...])
        gate = jax.nn.sigmoid(_dot(_rms_rows(x2, g1_ref[...]).astype(BF16), wg_ref[...]))
        o_ref[...] = x2 + _rms_rows(e * gate, g2_ref[...])


def _moe(x2d, mix_a, mix_b, w_out, g, wr_hi, wr_lo, rb, wgu, wd, p2d, wp, wg, g1, g2, tm=TAIL_TILE):
    t, d = x2d.shape
    kp = p2d.shape[1]
    half = mix_a.shape[1]
    tri = jnp.triu(jnp.ones((tm, tm), BF16))
    const = lambda shape: pl.BlockSpec(shape, lambda i, gi: (0,) * len(shape), pipeline_mode=pl.Buffered(1))
    return pl.pallas_call(
        _moe_kernel,
        grid=(t // tm, MOE_GROUPS),
        in_specs=[pl.BlockSpec((tm, d), lambda i, gi: (i, 0)),
                  pl.BlockSpec((tm, half), lambda i, gi: (i, 0)),
                  pl.BlockSpec((tm, half), lambda i, gi: (i, 0)),
                  const((2 * half, d)),
                  const((1, d)), const((MOE_ROUTER_ROWS, d)), const((MOE_ROUTER_ROWS, d)),
                  const((MOE_ROUTER_ROWS, 1)), const((tm, tm)),
                  pl.BlockSpec((None, d, MOE_EPG * 2 * MOE_HIDDEN), lambda i, gi: (gi, 0, 0)),
                  pl.BlockSpec((None, MOE_EPG * MOE_HIDDEN, d), lambda i, gi: (gi, 0, 0)),
                  pl.BlockSpec((tm, kp), lambda i, gi: (i, 0)),
                  const((kp, d)), const((d, d)), const((1, d)), const((1, d))],
        out_specs=pl.BlockSpec((tm, d), lambda i, gi: (i, 0)),
        out_shape=jax.ShapeDtypeStruct((t, d), F32),
        scratch_shapes=[pltpu.VMEM((tm, d), BF16), pltpu.VMEM((8, tm), F32), pltpu.VMEM((tm, LANES), F32),
                        pltpu.VMEM((tm, LANES), BF16), pltpu.VMEM((16, LANES), F32)],
        compiler_params=_cparams(("parallel", "arbitrary"), "tail"),
        name="moe",
    )(x2d, mix_a, mix_b, w_out, g, wr_hi, wr_lo, rb, tri, wgu, wd, p2d, wp, wg, g1, g2)


def _pad_lanes(cols, width=LANES):
    return jnp.pad(cols, ((0, 0), (0, width - cols.shape[-1])))


def _row(v):
    return v.reshape(1, -1).astype(F32)


def _even_mix(x2d, bsz, seq, norm_g, w_in, fox_b_f, fox_qn_g, fox_kn_g, conv_w, conv_b, b_i, b_f, mnorm_g):
    hw = HALF
    o = np.cumsum([0, hw, hw, hw, ATT_HEADS, hw, hw, hw, hw, MLSTM_HEADS, MLSTM_HEADS, hw])
    seg = lambda i: w_in[:, o[i]:o[i + 1]]
    fq, fk, fv, ff, fo, mq, mk, mv, mi, mf, mo = [seg(i) for i in range(11)]
    w = jnp.concatenate([fq, fk, fv, mv, fo, mq, mk, mo, _pad_lanes(jnp.concatenate([ff, mi, mf], axis=1))],
                        axis=1).astype(BF16)
    gain = jnp.concatenate([jnp.tile(fox_qn_g, ATT_HEADS) * ATT_Q_SCALE, jnp.tile(fox_kn_g, ATT_HEADS)])
    att, rest, gates = _in_proj(x2d, _row(norm_g), w, _row(gain), ((4 * hw, BF16), (4 * hw, F32), (LANES, F32)),
                                2 * hw)
    att = att.reshape(bsz, seq, 4 * hw)
    rest = rest.reshape(bsz, seq, 4 * hw)
    bias = _pad_lanes(_row(jnp.concatenate([fox_b_f, b_i, b_f])))
    gcol, rm = _prep_even(gates.reshape(bsz, seq, LANES), bias)
    out_a = _fox_attention(att, rest, gcol)
    out_b = _mlstm(att, rest, conv_w, _row(conv_b), gcol, rm, _row(mnorm_g))
    return out_a.reshape(-1, hw), out_b.reshape(-1, hw)


def _odd_mix(x2d, bsz, seq, norm_g, w_in, moba_qn_g, moba_kn_g, conv_w, conv_b, dt_bias, a_log, d_skip, snorm_g):
    hw = HALF
    nbc = SSD_GROUPS * SSD_STATE
    o = np.cumsum([0, hw, hw, hw, hw, hw, nbc, nbc, SSD_HEADS])
    w = jnp.concatenate([w_in[:, :o[7]], _pad_lanes(w_in[:, o[7]:o[8]])], axis=1).astype(BF16)
    gain = jnp.concatenate([jnp.tile(moba_qn_g, ATT_HEADS) * ATT_Q_SCALE, jnp.tile(moba_kn_g, ATT_HEADS)])
    moba_u, zx, bcin, dts = _in_proj(x2d, _row(norm_g), w, _row(gain),
                                     ((3 * hw, BF16), (2 * hw, F32), (2 * nbc, F32), (LANES, F32)), 2 * hw)
    gcol, rs = _prep_odd(dts.reshape(bsz, seq, LANES), _pad_lanes(_row(dt_bias)), _pad_lanes(_row(a_log)))
    out_c = _moba_attention(moba_u.reshape(bsz, seq, 3 * hw))
    out_d = _ssd(zx.reshape(bsz, seq, 2 * hw), bcin.reshape(bsz, seq, 2 * nbc), conv_w, _row(conv_b), gcol,
                 _ssd_expand_matrices(), rs, _row(jnp.repeat(d_skip, SSD_HEAD_DIM)), _row(snorm_g))
    return out_c.reshape(-1, hw), out_d.reshape(-1, hw)


def _layer_tail(x2d, mix_a, mix_b, w_out, norm_g, w_group, b_group, w_router, b_router, w_gate, w_up, w_down,
                p2d, ple_w_proj, ple_w_gate, ple_gate_norm_g, ple_out_norm_g):
    d = x2d.shape[1]
    wr = _pad_lanes(jnp.concatenate([w_group, w_router], axis=1), MOE_ROUTER_ROWS).T
    wr_hi = wr.astype(BF16)
    wr_lo = (wr - wr_hi.astype(F32)).astype(BF16)
    rb = _pad_lanes(_row(jnp.concatenate([b_group, b_router])), MOE_ROUTER_ROWS).T
    wgu = jnp.concatenate([w_gate, w_up], axis=-1).astype(BF16)
    wgu = wgu.transpose(0, 2, 1, 3).reshape(MOE_GROUPS, d, MOE_EPG * 2 * MOE_HIDDEN)
    wd = w_down.reshape(MOE_GROUPS, MOE_EPG * MOE_HIDDEN, d).astype(BF16)
    return _moe(x2d, mix_a, mix_b, w_out.astype(BF16), _row(norm_g), wr_hi, wr_lo, rb, wgu, wd,
                p2d, ple_w_proj.astype(BF16), ple_w_gate.astype(BF16), _row(ple_gate_norm_g), _row(ple_out_norm_g))


def kernel(x, p, norm1_g, norm2_g, ev_w_in, ev_fox_b_f, ev_fox_qn_g, ev_fox_kn_g, ev_mlstm_conv_w, ev_mlstm_conv_b, ev_mlstm_b_i, ev_mlstm_b_f, ev_mlstm_norm_g, ev_w_out, od_w_in, od_moba_qn_g, od_moba_kn_g, od_ssd_conv_w, od_ssd_conv_b, od_ssd_dt_bias, od_ssd_A_log, od_ssd_D, od_ssd_norm_g, od_w_out, moe_w_group, moe_b_group, moe_w_router, moe_b_router, moe_w_gate, moe_w_up, moe_w_down, ple_w_proj, ple_w_gate, ple_gate_norm_g, ple_out_norm_g):
    bsz, seq, d = x.shape
    depth = p.shape[0]
    x2d = x.reshape(bsz * seq, d)
    for i in range(depth):
        j = i // 2
        if i % 2 == 0:
            mix_a, mix_b = _even_mix(x2d, bsz, seq, norm1_g[i], ev_w_in[j], ev_fox_b_f[j], ev_fox_qn_g[j],
                                     ev_fox_kn_g[j], ev_mlstm_conv_w[j], ev_mlstm_conv_b[j], ev_mlstm_b_i[j],
                                     ev_mlstm_b_f[j], ev_mlstm_norm_g[j])
            w_out = ev_w_out[j]
        else:
            mix_a, mix_b = _odd_mix(x2d, bsz, seq, norm1_g[i], od_w_in[j], od_moba_qn_g[j], od_moba_kn_g[j],
                                    od_ssd_conv_w[j], od_ssd_conv_b[j], od_ssd_dt_bias[j], od_ssd_A_log[j],
                                    od_ssd_D[j], od_ssd_norm_g[j])
            w_out = od_w_out[j]
        x2d = _layer_tail(x2d, mix_a, mix_b, w_out, norm2_g[i], moe_w_group[i], moe_b_group[i], moe_w_router[i],
                          moe_b_router[i], moe_w_gate[i], moe_w_up[i], moe_w_down[i], p[i].reshape(bsz * seq, -1),
                          ple_w_proj[i], ple_w_gate[i], ple_gate_norm_g[i], ple_out_norm_g[i])
    return x2d.reshape(bsz, seq, d)
```

```python
import functools

import jax
import jax.numpy as jnp
import numpy as np
from jax import lax
from jax.experimental import pallas as pl
from jax.experimental.pallas import tpu as pltpu

F32 = jnp.float32
BF16 = jnp.bfloat16
MIX_DTYPE = BF16
NEG_INF = float("-inf")
MASKED_LOGIT = -2.0 ** 100
LOG2E = 1.4426950408889634

NORM_EPS = 1e-6
D_MODEL = 1024
HALF = D_MODEL // 2
ATT_HEAD_DIM = 64
ATT_HEADS = HALF // ATT_HEAD_DIM
ATT_PAIRS = ATT_HEADS // 2
ATT_Q_SCALE = ATT_HEAD_DIM ** -0.5 * LOG2E
ATT_V_ROWS = ATT_HEAD_DIM + 16
MLSTM_HEAD_DIM = 128
MLSTM_HEADS = HALF // MLSTM_HEAD_DIM
SSD_HEAD_DIM = 64
SSD_HEADS = HALF // SSD_HEAD_DIM
SSD_GROUPS = 2
SSD_STATE = 128
SSD_GROUP_W = HALF // SSD_GROUPS
SSD_HEADS_PER_GROUP = SSD_HEADS // SSD_GROUPS
CONV_WIDTH = 4
MOBA_BLOCK = 256
MOBA_TOPK = 3
MOE_GROUPS = 4
MOE_EPG = 4
MOE_EXPERTS = MOE_GROUPS * MOE_EPG
MOE_HIDDEN = D_MODEL // 4
MOE_ROUTER_ROWS = 32
MOE_CHUNK = 256
MOE_SMALL_CHUNK = 64

LANES = 128
MXU_N = 256
MIB = 1024 * 1024

SEQ_TILE = 256
ATT_TILE = 512
IN_PROJ_TILE = 512
TAIL_TILE = 1024
VMEM_LIMIT_MIB = {"in_proj": 52, "prep": 32, "attention": 40, "mlstm": 40, "ssd": 48, "tail": 56}


def _cparams(sem, call):
    return pltpu.CompilerParams(dimension_semantics=sem, vmem_limit_bytes=VMEM_LIMIT_MIB[call] * MIB)


def _split2(x):
    hi = x.astype(BF16)
    lo = (x - hi.astype(F32)).astype(BF16)
    return hi, lo


def _split3(x):
    hi = x.astype(BF16)
    r = x - hi.astype(F32)
    mid = r.astype(BF16)
    lo = (r - mid.astype(F32)).astype(BF16)
    return hi, mid, lo


def _dot(a, b):
    return jnp.dot(a, b, preferred_element_type=F32)


def _dot_nt(a, b):
    return lax.dot_general(a, b, (((1,), (1,)), ((), ())), preferred_element_type=F32)


def _log_sigmoid(x):
    return jnp.minimum(x, 0.0) - jnp.log1p(jnp.exp(-jnp.abs(x)))


def _softplus(x):
    return jnp.maximum(x, 0.0) + jnp.log1p(jnp.exp(-jnp.abs(x)))


def _rms_rows(xf, g, eps=NORM_EPS):
    ms = jnp.mean(xf * xf, axis=-1, keepdims=True)
    return xf * lax.rsqrt(ms + eps) * g


def _block_index(i, block):
    shift = block.bit_length() - 1
    assert 1 << shift == block
    return lax.shift_right_logical(i, shift)


def _lane_col(x, idx):
    lane = lax.broadcasted_iota(jnp.int32, (1, x.shape[1]), 1)
    return jnp.sum(jnp.where(lane == idx, x, 0.0), axis=1, keepdims=True)


def _in_proj_kernel(x_ref, g_ref, w_ref, gain_ref, *refs, widths, n_norm, head_dim):
    out_refs, raw_sc = refs[:-1], refs[-1]
    h = _rms_rows(x_ref[...], g_ref[...]).astype(BF16)
    off = 0
    for o_ref, wdt in zip(out_refs, widths):
        for c0 in range(0, wdt, MXU_N):
            n = min(MXU_N, wdt - c0)
            y = _dot(h, w_ref[:, off + c0: off + c0 + n])
            if off + c0 < n_norm:
                raw_sc[:, off + c0: off + c0 + n] = y
            else:
                o_ref[:, c0:c0 + n] = y.astype(o_ref.dtype)
        off += wdt
    r = _block_index(lax.broadcasted_iota(jnp.int32, (MXU_N, MXU_N), 0), head_dim)
    c = _block_index(lax.broadcasted_iota(jnp.int32, (MXU_N, MXU_N), 1), head_dim)
    seg = jnp.where(r == c, 1.0, 0.0).astype(BF16)
    assert n_norm <= widths[0] and n_norm % MXU_N == 0
    for c0 in range(0, n_norm, MXU_N):
        y = raw_sc[:, c0:c0 + MXU_N]
        ss = _dot((y * y).astype(BF16), seg)
        y = y * lax.rsqrt(ss * (1.0 / head_dim) + NORM_EPS) * gain_ref[:, c0:c0 + MXU_N]
        out_refs[0][:, c0:c0 + MXU_N] = y.astype(out_refs[0].dtype)


def _in_proj(x2d, g, w, gain, segments, n_norm, tm=IN_PROJ_TILE):
    t, d = x2d.shape
    n = w.shape[1]
    widths = tuple(wd for wd, _ in segments)
    kern = functools.partial(_in_proj_kernel, widths=widths, n_norm=n_norm, head_dim=ATT_HEAD_DIM)
    return pl.pallas_call(
        kern,
        grid=(t // tm,),
        in_specs=[
            pl.BlockSpec((tm, d), lambda i: (i, 0)),
            pl.BlockSpec((1, d), lambda i: (0, 0), pipeline_mode=pl.Buffered(1)),
            pl.BlockSpec((d, n), lambda i: (0, 0), pipeline_mode=pl.Buffered(1)),
            pl.BlockSpec((1, n_norm), lambda i: (0, 0), pipeline_mode=pl.Buffered(1)),
        ],
        out_specs=[pl.BlockSpec((tm, wd), lambda i: (i, 0)) for wd in widths],
        out_shape=[jax.ShapeDtypeStruct((t, wd), dt) for wd, dt in segments],
        scratch_shapes=[pltpu.VMEM((tm, n_norm), F32)],
        compiler_params=_cparams(("parallel",), "in_proj"),
        name="in_proj",
    )(x2d, g, w, gain)


def _tri_cumsum(vals, tri):
    hi, mid, lo = _split3(vals)
    return _dot(tri, hi) + _dot(tri, mid) + _dot(tri, lo)


def _lower_tri(n):
    r = lax.broadcasted_iota(jnp.int32, (n, n), 0)
    c = lax.broadcasted_iota(jnp.int32, (n, n), 1)
    return jnp.where(c <= r, 1.0, 0.0).astype(BF16)


def _prep_even_kernel(raw_ref, bias_ref, g_ref, rm_ref):
    s = raw_ref.shape[0]
    ch = SEQ_TILE
    lane = lax.broadcasted_iota(jnp.int32, (1, LANES), 1)
    is_cum = (lane < ATT_HEADS) | ((lane >= ATT_HEADS + MLSTM_HEADS) & (lane < ATT_HEADS + 2 * MLSTM_HEADS))
    tri = _lower_tri(ch)
    carry = jnp.zeros((1, LANES), F32)
    pad = jnp.zeros((6, ch), F32)
    for c in range(s // ch):
        z = raw_ref[c * ch:(c + 1) * ch, :] + bias_ref[...]
        cs = _tri_cumsum(jnp.where(is_cum, _log_sigmoid(z), 0.0), tri) + carry
        carry = cs[ch - 1:ch, :]
        g = jnp.where(is_cum, cs, z)
        g_ref[c * ch:(c + 1) * ch, :] = g
        gt = g.T
        for hh in range(MLSTM_HEADS):
            i_row = gt[ATT_HEADS + hh:ATT_HEADS + hh + 1]
            f_row = gt[ATT_HEADS + MLSTM_HEADS + hh:ATT_HEADS + MLSTM_HEADS + hh + 1]
            rm_ref[hh, :, c * ch:(c + 1) * ch] = jnp.concatenate([i_row - f_row, f_row, pad], axis=0)


def _prep_even(raw, bias):
    b, s, _ = raw.shape
    return pl.pallas_call(
        _prep_even_kernel,
        grid=(b,),
        in_specs=[pl.BlockSpec((None, s, LANES), lambda i: (i, 0, 0)),
                  pl.BlockSpec((1, LANES), lambda i: (0, 0))],
        out_specs=[pl.BlockSpec((None, s, LANES), lambda i: (i, 0, 0)),
                   pl.BlockSpec((None, MLSTM_HEADS, 8, s), lambda i: (i, 0, 0, 0))],
        out_shape=[jax.ShapeDtypeStruct((b, s, LANES), F32),
                   jax.ShapeDtypeStruct((b, MLSTM_HEADS, 8, s), F32)],
        compiler_params=_cparams(("parallel",), "prep"),
        name="prep_even",
    )(raw, bias)


def _prep_odd_kernel(raw_ref, bias_ref, alog_ref, g_ref, rs_ref):
    s = raw_ref.shape[0]
    ch = SEQ_TILE
    lane = lax.broadcasted_iota(jnp.int32, (1, LANES), 1)
    is_dt = lane < SSD_HEADS
    tri = _lower_tri(ch)
    a_row = jnp.where(is_dt, -jnp.exp(alog_ref[...]), 0.0)
    pad = jnp.zeros((8 - SSD_HEADS_PER_GROUP, ch), F32)
    for c in range(s // ch):
        dt = jnp.where(is_dt, _softplus(raw_ref[c * ch:(c + 1) * ch, :] + bias_ref[...]), 0.0)
        acum = _tri_cumsum(a_row * dt, tri)
        g = jnp.where(is_dt, dt, pltpu.roll(acum, SSD_HEADS, 1))
        g_ref[c * ch:(c + 1) * ch, :] = g
        gt = g.T
        for gi in range(SSD_GROUPS):
            lo = SSD_HEADS + gi * SSD_HEADS_PER_GROUP
            rs_ref[gi, :, c * ch:(c + 1) * ch] = jnp.concatenate([gt[lo:lo + SSD_HEADS_PER_GROUP], pad], axis=0)


def _prep_odd(raw, bias, alog):
    b, s, _ = raw.shape
    return pl.pallas_call(
        _prep_odd_kernel,
        grid=(b,),
        in_specs=[pl.BlockSpec((None, s, LANES), lambda i: (i, 0, 0)),
                  pl.BlockSpec((1, LANES), lambda i: (0, 0)),
                  pl.BlockSpec((1, LANES), lambda i: (0, 0))],
        out_specs=[pl.BlockSpec((None, s, LANES), lambda i: (i, 0, 0)),
                   pl.BlockSpec((None, SSD_GROUPS, 8, s), lambda i: (i, 0, 0, 0))],
        out_shape=[jax.ShapeDtypeStruct((b, s, LANES), F32),
                   jax.ShapeDtypeStruct((b, SSD_GROUPS, 8, s), F32)],
        compiler_params=_cparams(("parallel",), "prep"),
        name="prep_odd",
    )(raw, bias, alog)


def _two_pass_softmax_pv(n_past, past_span, past_logits, own_span, own_logits, v_rows, s_scr, tq):
    heads = range(2)
    hd = ATT_HEAD_DIM
    neg = jnp.full((1, tq), NEG_INF, F32)

    def store_max(span, s_t, hh, m):
        s_scr[hh, pl.ds(*span), :] = s_t
        return jnp.maximum(m, jnp.max(s_t, axis=0, keepdims=True))

    def accumulate(span, hh, m, acc):
        p = jnp.exp2(s_scr[hh, pl.ds(*span), :] - m)
        return acc + _dot(v_rows(*span, hh), p.astype(BF16))

    ms = [neg, neg]
    for j in range(n_past):
        ms = [store_max(past_span(j), past_logits(j, hh), hh, ms[hh]) for hh in heads]
    ms = [store_max(own_span, own_logits(hh), hh, ms[hh]) for hh in heads]
    acc = [accumulate(own_span, hh, ms[hh], jnp.zeros((ATT_V_ROWS, tq), F32)) for hh in heads]
    for j in range(n_past):
        acc = [accumulate(past_span(j), hh, ms[hh], acc[hh]) for hh in heads]
    return jnp.concatenate([a[:hd] / a[hd:hd + 1] for a in acc], axis=0).T


def _stage_v(v_ref, vt):
    hd = ATT_HEAD_DIM
    pad = ATT_V_ROWS - hd
    for c in range(v_ref.shape[0] // SEQ_TILE):
        cols = slice(c * SEQ_TILE, (c + 1) * SEQ_TILE)
        v_t = v_ref[cols, :].astype(F32).T
        ones_row = jnp.where(lax.broadcasted_iota(jnp.int32, (pad, SEQ_TILE), 0) == 0, 1.0, 0.0)
        for hh in range(2):
            vt[hh * ATT_V_ROWS:(hh + 1) * ATT_V_ROWS, cols] = jnp.concatenate(
                [v_t[hh * hd:(hh + 1) * hd], ones_row], axis=0).astype(BF16)


def _head_queries(q2):
    lane = lax.broadcasted_iota(jnp.int32, (1, LANES), 1)
    return [jnp.where(lane < ATT_HEAD_DIM, q2, 0.0), jnp.where(lane >= ATT_HEAD_DIM, q2, 0.0)]


def _fox_kernel(q_ref, k_ref, v_ref, fo_ref, g_ref, o_ref, vt, kaug, s_scr):
    hp = pl.program_id(1)
    tq = s_scr.shape[2]
    s_len = k_ref.shape[0]
    _stage_v(v_ref, vt)
    lane = lax.broadcasted_iota(jnp.int32, (1, LANES), 1)
    g = g_ref[...]
    k2 = k_ref[...]
    spare, own = [], []
    for hh in range(2):
        own.append((lane < ATT_HEAD_DIM) if hh == 0 else (lane >= ATT_HEAD_DIM))
        spare.append(ATT_HEAD_DIM if hh == 0 else 0)
        hi, mid, lo = [p.astype(F32) for p in _split3(_lane_col(g, 2 * hp + hh) * (-LOG2E))]
        parts = jnp.where(lane == spare[hh], hi, jnp.where(lane == spare[hh] + 1, mid,
                          jnp.where(lane == spare[hh] + 2, lo, 0.0)))
        kaug[hh] = jnp.where(own[hh], k2, parts.astype(k2.dtype))
    krow = lax.broadcasted_iota(jnp.int32, (tq, tq), 0)
    qcol = lax.broadcasted_iota(jnp.int32, (tq, tq), 1)

    def span(j):
        return j * tq, tq

    for qi in range(s_len // tq):
        rows = slice(qi * tq, (qi + 1) * tq)
        q2 = q_ref[rows, :]
        qh = [jnp.where(own[hh], q2, jnp.where((lane >= spare[hh]) & (lane < spare[hh] + 3), 1.0, 0.0)
                        .astype(q2.dtype)) for hh in range(2)]

        def logits(j, hh, qh=qh):
            return _dot_nt(kaug[hh, pl.ds(*span(j)), :], qh[hh])

        out = _two_pass_softmax_pv(
            qi, span, logits, span(qi), lambda hh, qi=qi, logits=logits: jnp.where(krow <= qcol, logits(qi, hh), NEG_INF),
            lambda st, size, hh: vt[hh * ATT_V_ROWS:(hh + 1) * ATT_V_ROWS, pl.ds(st, size)], s_scr, tq)
        o_ref[rows, :] = (out * jax.nn.sigmoid(fo_ref[rows, :])).astype(o_ref.dtype)


def _fox_attention(qkv, ogate, gcol, tq=ATT_TILE):
    b, s, _ = qkv.shape
    np_ = ATT_PAIRS
    return pl.pallas_call(
        _fox_kernel,
        grid=(b, np_),
        in_specs=[
            pl.BlockSpec((None, s, LANES), lambda bi, hp: (bi, 0, hp)),
            pl.BlockSpec((None, s, LANES), lambda bi, hp: (bi, 0, np_ + hp)),
            pl.BlockSpec((None, s, LANES), lambda bi, hp: (bi, 0, 2 * np_ + hp)),
            pl.BlockSpec((None, s, LANES), lambda bi, hp: (bi, 0, hp)),
            pl.BlockSpec((None, s, LANES), lambda bi, hp: (bi, 0, 0)),
        ],
        out_specs=pl.BlockSpec((None, s, LANES), lambda bi, hp: (bi, 0, hp)),
        out_shape=jax.ShapeDtypeStruct((b, s, HALF), MIX_DTYPE),
        scratch_shapes=[pltpu.VMEM((2 * ATT_V_ROWS, s), BF16), pltpu.VMEM((2, s, LANES), BF16),
                        pltpu.VMEM((2, s, tq), F32)],
        compiler_params=_cparams(("parallel", "parallel"), "attention"),
        name="fox_attention",
    )(qkv, qkv, qkv, ogate, gcol)


def _moba_kernel(q_ref, k_ref, v_ref, o_ref, vt, kmh, kml, kaug, s_scr):
    tq = s_scr.shape[2]
    s_len = k_ref.shape[0]
    blk = MOBA_BLOCK
    nb = s_len // blk
    nbp = kmh.shape[0]
    per_tile = tq // blk
    _stage_v(v_ref, vt)
    means = [jnp.mean(k_ref[n * blk:(n + 1) * blk, :].astype(F32), axis=0, keepdims=True) for n in range(nb)]
    km = jnp.concatenate(means + [jnp.zeros((nbp - nb, LANES), F32)], axis=0)
    kmh[...], kml[...] = _split2(km)
    lane = lax.broadcasted_iota(jnp.int32, (1, LANES), 1)
    key_blk = _block_index(lax.broadcasted_iota(jnp.int32, (s_len, LANES), 0), blk)
    k2 = k_ref[...]
    own = [lane < ATT_HEAD_DIM, lane >= ATT_HEAD_DIM]
    spare = [ATT_HEAD_DIM, 0]
    for hh in range(2):
        kaug[hh] = jnp.where(own[hh], k2, jnp.where(lane - spare[hh] == key_blk, 1.0, 0.0).astype(k2.dtype))

    qcol1 = lax.broadcasted_iota(jnp.int32, (1, tq), 1)
    nrow = lax.broadcasted_iota(jnp.int32, (nbp, tq), 0)
    krow = lax.broadcasted_iota(jnp.int32, (tq, tq), 0)
    qcol = lax.broadcasted_iota(jnp.int32, (tq, tq), 1)
    same_blk = _block_index(krow, blk) == _block_index(qcol, blk)

    def v_rows(st, size, hh):
        return vt[hh * ATT_V_ROWS:(hh + 1) * ATT_V_ROWS, pl.ds(st, size)]

    def span(j):
        return j * tq, tq

    for qi in range(s_len // tq):
        rows = slice(qi * tq, (qi + 1) * tq)
        q2 = q_ref[rows, :]
        qh = _head_queries(q2)
        valid = nrow < per_tile * qi + _block_index(qcol1, blk)
        selected, q_aug = [], []
        for hh in range(2):
            gate = _dot_nt(kmh[...], qh[hh]) + _dot_nt(kml[...], qh[hh])
            gate = jnp.where(valid, gate, NEG_INF)
            rank = jnp.zeros((nbp, tq), F32)
            for m in range(nb):
                gm = gate[m:m + 1, :]
                rank = rank + jnp.where((gm > gate) | ((gm == gate) & (nrow > m)), 1.0, 0.0)
            selected.append((rank < float(MOBA_TOPK)) & valid)
            pieces = [jnp.where(selected[hh], 0.0, MASKED_LOGIT), jnp.zeros((LANES - spare[hh] - nbp, tq), F32)]
            if spare[hh]:
                pieces.insert(0, jnp.zeros((spare[hh], tq), F32))
            bias_rows = jnp.concatenate(pieces, axis=0)
            q_aug.append(jnp.where(own[hh], q2, bias_rows.T.astype(q2.dtype)))

        def own_logits(hh):
            sel = jnp.concatenate(
                [jnp.broadcast_to(selected[hh][per_tile * qi + n:per_tile * qi + n + 1, :], (blk, tq))
                 for n in range(per_tile)], axis=0)
            s_t = _dot_nt(k_ref[pl.ds(*span(qi)), :], qh[hh])
            return jnp.where((krow <= qcol) & (same_blk | sel), s_t, NEG_INF)

        def past_logits(j, hh):
            return _dot_nt(kaug[hh, pl.ds(*span(j)), :], q_aug[hh])

        out = _two_pass_softmax_pv(qi, span, past_logits, span(qi), own_logits, v_rows, s_scr, tq)
        o_ref[rows, :] = out.astype(o_ref.dtype)


def _moba_attention(qkv, tq=ATT_TILE):
    b, s, _ = qkv.shape
    np_ = ATT_PAIRS
    nb = s // MOBA_BLOCK
    nbp = -(-nb // 16) * 16
    return pl.pallas_call(
        _moba_kernel,
        grid=(b, np_),
        in_specs=[
            pl.BlockSpec((None, s, LANES), lambda bi, hp: (bi, 0, hp)),
            pl.BlockSpec((None, s, LANES), lambda bi, hp: (bi, 0, np_ + hp)),
            pl.BlockSpec((None, s, LANES), lambda bi, hp: (bi, 0, 2 * np_ + hp)),
        ],
        out_specs=pl.BlockSpec((None, s, LANES), lambda bi, hp: (bi, 0, hp)),
        out_shape=jax.ShapeDtypeStruct((b, s, HALF), MIX_DTYPE),
        scratch_shapes=[pltpu.VMEM((2 * ATT_V_ROWS, s), BF16),
                        pltpu.VMEM((nbp, LANES), BF16), pltpu.VMEM((nbp, LANES), BF16),
                        pltpu.VMEM((2, s, LANES), BF16), pltpu.VMEM((2, s, tq), F32)],
        compiler_params=_cparams(("parallel", "parallel"), "attention"),
        name="moba_attention",
    )(qkv, qkv, qkv)


def _conv_silu(x_ref, w_ref, b_ref, o_ref, scale):
    s_len, ch = x_ref.shape
    lc = SEQ_TILE
    w = w_ref[...]
    b = b_ref[...]
    row = lax.broadcasted_iota(jnp.int32, (8, ch), 0)
    for c in range(s_len // lc):
        cur = x_ref[c * lc:(c + 1) * lc, :]
        tail = x_ref[c * lc - 8:c * lc, :] if c > 0 else jnp.zeros((8, ch), F32)
        y = b
        for j in range(CONV_WIDTH - 1):
            sh = CONV_WIDTH - 1 - j
            rolled = pltpu.roll(cur, sh, 0)
            head = jnp.where(row < sh, pltpu.roll(tail, sh, 0), rolled[:8])
            y = y + jnp.concatenate([head, rolled[8:]], axis=0) * w[j:j + 1]
        y = y + cur * w[CONV_WIDTH - 1:CONV_WIDTH]
        y = y * jax.nn.sigmoid(y)
        o_ref[c * lc:(c + 1) * lc, :] = y * scale if scale != 1.0 else y


def _mlstm_kernel(q_ref, k_ref, v_ref, og_ref, cwq_ref, cwk_ref, cbq_ref, cbk_ref, g_ref, r_ref, ng_ref,
                  o_ref, qc, kc):
    hh = pl.program_id(1)
    s_len, d = q_ref.shape
    ln = SEQ_TILE
    _conv_silu(q_ref, cwq_ref, cbq_ref, qc, 1.0)
    _conv_silu(k_ref, cwk_ref, cbk_ref, kc, d ** -0.5)
    row = lax.broadcasted_iota(jnp.int32, (ln, ln), 0)
    col = lax.broadcasted_iota(jnp.int32, (ln, ln), 1)
    tri = col <= row
    lane = lax.broadcasted_iota(jnp.int32, (1, LANES), 1)
    ones_blk = jnp.broadcast_to(jnp.where(lane == 0, 1.0, 0.0), (ln, LANES)).astype(BF16)
    f_lane = ATT_HEADS + MLSTM_HEADS + hh
    c_prev = jnp.zeros((d, 2 * d), F32)
    m_prev = jnp.zeros((1, 1), F32)
    for c in range(s_len // ln):
        rows = slice(c * ln, (c + 1) * ln)
        q = qc[rows, :]
        k = kc[rows, :]
        f_col = _lane_col(g_ref[rows, :], f_lane)
        a_row = r_ref[0:1, rows]
        a_mat = jnp.where(tri, a_row, NEG_INF)
        m_col = jnp.maximum(m_prev, jnp.max(a_mat, axis=1, keepdims=True))
        qb = q.astype(BF16)
        w_qk = _dot_nt(qb, k.astype(BF16)) * jnp.exp(a_mat - m_col)
        v_aug = jnp.concatenate([v_ref[rows, :], ones_blk], axis=1)
        inter = jnp.exp(m_prev - m_col)
        q_c = _dot(qb, c_prev.astype(BF16))
        num = _dot(w_qk.astype(BF16), v_aug[:, :d]) + inter * q_c[:, :d]
        den = jnp.sum(w_qk, axis=1, keepdims=True) + inter * q_c[:, d:d + 1]
        h = num / jnp.maximum(jnp.abs(den), jnp.exp(-f_col - m_col))
        m_end = jnp.maximum(m_prev, jnp.max(a_row, axis=1, keepdims=True))
        k_w = (k.T * jnp.exp(a_row - m_end)).astype(BF16)
        c_prev = jnp.exp(m_prev - m_end) * c_prev + _dot(k_w, v_aug)
        m_prev = m_end
        hn = _rms_rows(h, ng_ref[...])
        o_ref[rows, :] = (hn * jax.nn.sigmoid(og_ref[rows, :])).astype(o_ref.dtype)


def _mlstm(att, rest, conv_w, conv_b, gcol, grow, norm_g):
    b, s, _ = att.shape
    nh = MLSTM_HEADS
    d = MLSTM_HEAD_DIM
    big = lambda off: pl.BlockSpec((None, s, d), lambda bi, h: (bi, 0, off + h))
    return pl.pallas_call(
        _mlstm_kernel,
        grid=(b, nh),
        in_specs=[
            big(nh), big(2 * nh), big(3 * nh), big(3 * nh),
            pl.BlockSpec((CONV_WIDTH, d), lambda bi, h: (0, h)),
            pl.BlockSpec((CONV_WIDTH, d), lambda bi, h: (0, nh + h)),
            pl.BlockSpec((1, d), lambda bi, h: (0, h)),
            pl.BlockSpec((1, d), lambda bi, h: (0, nh + h)),
            pl.BlockSpec((None, s, LANES), lambda bi, h: (bi, 0, 0)),
            pl.BlockSpec((None, None, 8, s), lambda bi, h: (bi, h, 0, 0)),
            pl.BlockSpec((1, d), lambda bi, h: (0, h)),
        ],
        out_specs=pl.BlockSpec((None, s, d), lambda bi, h: (bi, 0, h)),
        out_shape=jax.ShapeDtypeStruct((b, s, HALF), MIX_DTYPE),
        scratch_shapes=[pltpu.VMEM((s, d), F32), pltpu.VMEM((s, d), F32)],
        compiler_params=_cparams(("parallel", "parallel"), "mlstm"),
        name="mlstm",
    )(rest, rest, att, rest, conv_w, conv_w, conv_b, conv_b, gcol, grow, norm_g)


def _ssd_kernel(z_ref, x_ref, b_ref, c_ref, cwx_ref, cwb_ref, cwc_ref, cbx_ref, cbb_ref, cbc_ref,
                g_ref, e_ref, r_ref, dsk_ref, ng_ref, o_ref, xc, bc, cc):
    s_len, gw = x_ref.shape
    ln = SEQ_TILE
    _conv_silu(x_ref, cwx_ref, cbx_ref, xc, 1.0)
    _conv_silu(b_ref, cwb_ref, cbb_ref, bc, 1.0)
    _conv_silu(c_ref, cwc_ref, cbc_ref, cc, 1.0)
    row = lax.broadcasted_iota(jnp.int32, (ln, ln), 0)
    col = lax.broadcasted_iota(jnp.int32, (ln, ln), 1)
    tri = col <= row
    head_of_lane = _block_index(lax.broadcasted_iota(jnp.int32, (1, gw), 1), SSD_HEAD_DIM)
    e_dt2 = jnp.concatenate([e_ref[0], e_ref[0]], axis=0)
    e_ac2 = jnp.concatenate([e_ref[1], e_ref[1]], axis=0)
    prev = jnp.zeros((b_ref.shape[1], gw), F32)
    for c in range(s_len // ln):
        rows = slice(c * ln, (c + 1) * ln)
        x = xc[rows, :]
        bm = bc[rows, :]
        cm = cc[rows, :]
        z = z_ref[rows, :]
        hi, mid, lo = _split3(g_ref[rows, :])
        hi_mid = jnp.concatenate([hi, mid], axis=1)
        dt_e = _dot(hi_mid, e_dt2) + _dot(lo, e_ref[0])
        ac_e = _dot(hi_mid, e_ac2) + _dot(lo, e_ref[1])
        xdt = x * dt_e
        xdt_b = xdt.astype(BF16)
        cmb = cm.astype(BF16)
        cb = _dot_nt(cmb, bm.astype(BF16))
        ac_rows = r_ref[0:SSD_HEADS_PER_GROUP, rows]
        y = jnp.zeros((ln, gw), F32)
        for hh in range(SSD_HEADS_PER_GROUP):
            ac_col = ac_e[:, hh * SSD_HEAD_DIM:hh * SSD_HEAD_DIM + 1]
            l_mat = jnp.exp(jnp.where(tri, ac_col - ac_rows[hh:hh + 1], NEG_INF))
            y = jnp.where(head_of_lane == hh, _dot((cb * l_mat).astype(BF16), xdt_b), y)
        ac_end = ac_e[ln - 1:ln, :]
        y = y + _dot(cmb, prev.astype(BF16)) * jnp.exp(ac_e)
        states = _dot(bm.T.astype(BF16), (xdt * jnp.exp(ac_end - ac_e)).astype(BF16))
        prev = prev * jnp.exp(ac_end) + states
        y = y + dsk_ref[...] * x
        y = y * (z * jax.nn.sigmoid(z))
        o_ref[rows, :] = _rms_rows(y, ng_ref[...]).astype(o_ref.dtype)


def _ssd(zx, bcin, conv_w, conv_b, gcol, expand, grow, d_row, norm_g):
    b, s, _ = zx.shape
    gw = SSD_GROUP_W
    ns = SSD_STATE
    ng = SSD_GROUPS
    xoff = HALF // gw
    boff = HALF // ns
    coff = (HALF + ng * ns) // ns
    return pl.pallas_call(
        _ssd_kernel,
        grid=(b, ng),
        in_specs=[
            pl.BlockSpec((None, s, gw), lambda bi, g: (bi, 0, g)),
            pl.BlockSpec((None, s, gw), lambda bi, g: (bi, 0, xoff + g)),
            pl.BlockSpec((None, s, ns), lambda bi, g: (bi, 0, g)),
            pl.BlockSpec((None, s, ns), lambda bi, g: (bi, 0, ng + g)),
            pl.BlockSpec((CONV_WIDTH, gw), lambda bi, g: (0, g)),
            pl.BlockSpec((CONV_WIDTH, ns), lambda bi, g: (0, boff + g)),
            pl.BlockSpec((CONV_WIDTH, ns), lambda bi, g: (0, coff + g)),
            pl.BlockSpec((1, gw), lambda bi, g: (0, g)),
            pl.BlockSpec((1, ns), lambda bi, g: (0, boff + g)),
            pl.BlockSpec((1, ns), lambda bi, g: (0, coff + g)),
            pl.BlockSpec((None, s, LANES), lambda bi, g: (bi, 0, 0)),
            pl.BlockSpec((None, 2, LANES, gw), lambda bi, g: (g, 0, 0, 0)),
            pl.BlockSpec((None, None, 8, s), lambda bi, g: (bi, g, 0, 0)),
            pl.BlockSpec((1, gw), lambda bi, g: (0, g)),
            pl.BlockSpec((1, gw), lambda bi, g: (0, g)),
        ],
        out_specs=pl.BlockSpec((None, s, gw), lambda bi, g: (bi, 0, g)),
        out_shape=jax.ShapeDtypeStruct((b, s, HALF), MIX_DTYPE),
        scratch_shapes=[pltpu.VMEM((s, gw), F32), pltpu.VMEM((s, ns), F32), pltpu.VMEM((s, ns), F32)],
        compiler_params=_cparams(("parallel", "parallel"), "ssd"),
        name="ssd",
    )(zx, zx, bcin, bcin, conv_w, conv_w, conv_w, conv_b, conv_b, conv_b, gcol, expand, grow, d_row, norm_g)


def _ssd_expand_matrices():
    e = np.zeros((SSD_GROUPS, 2, LANES, SSD_GROUP_W), np.float32)
    for g in range(SSD_GROUPS):
        for h in range(SSD_HEADS_PER_GROUP):
            head = g * SSD_HEADS_PER_GROUP + h
            e[g, 0, head, h * SSD_HEAD_DIM:(h + 1) * SSD_HEAD_DIM] = 1.0
            e[g, 1, SSD_HEADS + head, h * SSD_HEAD_DIM:(h + 1) * SSD_HEAD_DIM] = 1.0
    return jnp.asarray(e, BF16)


def _moe_route(logits_t):
    gl = [logits_t[g:g + 1, :] for g in range(MOE_GROUPS)]
    g_max = functools.reduce(jnp.maximum, gl)
    g_den = sum(jnp.exp(x - g_max) for x in gl)
    g_w = 1.0 / g_den
    taken = jnp.zeros_like(g_max) > 1.0
    is_g = []
    for g in range(MOE_GROUPS):
        hit = (gl[g] == g_max) & jnp.logical_not(taken)
        is_g.append(hit)
        taken = taken | hit
    e_in = []
    for j in range(MOE_EPG):
        v = jnp.zeros_like(g_max)
        for g in range(MOE_GROUPS):
            row = MOE_GROUPS + g * MOE_EPG + j
            v = jnp.where(is_g[g], logits_t[row:row + 1, :], v)
        e_in.append(v)
    rank = []
    for j in range(MOE_EPG):
        r = jnp.zeros_like(g_max)
        for m in range(MOE_EPG):
            if m == j:
                continue
            ahead = (e_in[m] > e_in[j]) | ((e_in[m] == e_in[j]) & (m < j))
            r = r + jnp.where(ahead, 1.0, 0.0)
        rank.append(r)
    v0 = sum(jnp.where(rank[j] == 0.0, e_in[j], 0.0) for j in range(MOE_EPG))
    v1 = sum(jnp.where(rank[j] == 1.0, e_in[j], 0.0) for j in range(MOE_EPG))
    e1 = jnp.exp(v1 - v0)
    w0 = 1.0 / (1.0 + e1)
    w1 = e1 / (1.0 + e1)
    comb = []
    for g in range(MOE_GROUPS):
        for j in range(MOE_EPG):
            wj = jnp.where(rank[j] == 0.0, w0, jnp.where(rank[j] == 1.0, w1, 0.0))
            comb.append(jnp.where(is_g[g], g_w * wj, 0.0))
    return comb, is_g


ROUTE_GROUP, ROUTE_CHUNK, ROUTE_SLOT = 0, 1, 2


def _moe_kernel(x_ref, ma_ref, mb_ref, wo_ref, g_ref, wrh_ref, wrl_ref, rb_ref, tri_ref, wgu_ref, wd_ref,
                p_ref, wp_ref, wg_ref, g1_ref, g2_ref, o_ref, h_sc, rows_sc, cols_sc, cw_sc, cnt_sc):
    grp = pl.program_id(1)
    tm, d = x_ref.shape
    ch = MOE_CHUNK

    @pl.when(grp == 0)
    def _():
        half = ma_ref.shape[1]
        x = x_ref[...] + _dot(ma_ref[...], wo_ref[:half, :]) + _dot(mb_ref[...], wo_ref[half:, :])
        h_hi, h_lo = _split2(_rms_rows(x, g_ref[...]))
        h_sc[...] = h_hi
        logits_t = (_dot_nt(wrh_ref[...], h_hi) + _dot_nt(wrh_ref[...], h_lo) + _dot_nt(wrl_ref[...], h_hi)
                    + rb_ref[...])
        comb, is_g = _moe_route(logits_t)
        member = jnp.concatenate([jnp.where(m, 1.0, 0.0) for m in is_g]
                                 + [jnp.zeros((16 - MOE_GROUPS, tm), F32)], axis=0)
        incl = _dot(member.astype(BF16), tri_ref[...])
        pos = sum(jnp.where(is_g[g], incl[g:g + 1, :] - 1.0, 0.0) for g in range(MOE_GROUPS))
        gid = sum(jnp.where(is_g[g], float(g), 0.0) for g in range(MOE_GROUPS))
        chunk = jnp.floor(pos * (1.0 / ch))
        cnt = sum(jnp.where(is_g[g], incl[g:g + 1, tm - 1:tm], 0.0) for g in range(MOE_GROUPS))
        full = jnp.floor(cnt * (1.0 / ch))
        tail = cnt - ch * full
        appended = (tail > 0.0) & (tail <= MOE_SMALL_CHUNK) & (full >= 1.0) & (chunk == full)
        chunk = jnp.where(appended, chunk - 1.0, chunk)
        rows = jnp.concatenate([gid, chunk, pos - ch * chunk, jnp.zeros((LANES - 3, tm), F32)], axis=0)
        rows_sc[...] = rows[:rows_sc.shape[0]]
        cols_sc[...] = rows.T
        parts = _split3(jnp.concatenate(comb, axis=0))
        cw_rows = jnp.concatenate([p.astype(F32) for p in parts]
                                  + [jnp.zeros((LANES - 3 * MOE_EXPERTS, tm), F32)], axis=0)
        cw_sc[...] = cw_rows.T.astype(BF16)
        cnt_sc[...] = jnp.broadcast_to(incl[:, tm - 1:tm], cnt_sc.shape)
        o_ref[...] = x

    lane = lax.broadcasted_iota(jnp.int32, (1, LANES), 1)
    expert_of_lane = jnp.where(lane < 3 * MOE_EXPERTS, lane & (MOE_EXPERTS - 1), -1)
    grp_row = lax.broadcasted_iota(jnp.int32, cnt_sc.shape, 0) == grp
    n_grp = jnp.sum(jnp.where(grp_row, cnt_sc[...], 0.0), axis=0, keepdims=True)[0, 0].astype(jnp.int32)
    grp_f = grp.astype(F32)

    def run_chunk(c, size):
        rows = rows_sc[...]
        cols = cols_sc[...]
        in_chunk = (rows[ROUTE_GROUP:ROUTE_GROUP + 1, :] == grp_f) & (rows[ROUTE_CHUNK:ROUTE_CHUNK + 1, :] == c)
        slot = lax.broadcasted_iota(jnp.int32, (size, tm), 0).astype(F32)
        gather = jnp.where(in_chunk & (rows[ROUTE_SLOT:ROUTE_SLOT + 1, :] == slot), 1.0, 0.0).astype(BF16)
        in_chunk_t = (cols[:, ROUTE_GROUP:ROUTE_GROUP + 1] == grp_f) & (cols[:, ROUTE_CHUNK:ROUTE_CHUNK + 1] == c)
        slot_t = lax.broadcasted_iota(jnp.int32, (tm, size), 1).astype(F32)
        scatter = jnp.where(in_chunk_t & (cols[:, ROUTE_SLOT:ROUTE_SLOT + 1] == slot_t), 1.0, 0.0).astype(BF16)
        hs = _dot(gather, h_sc[...]).astype(BF16)
        table = _dot(gather, cw_sc[...])
        ab = _dot(hs, wgu_ref[...])
        hid = []
        for j in range(MOE_EPG):
            cw = jnp.sum(jnp.where(expert_of_lane == MOE_EPG * grp + j, table, 0.0), axis=1, keepdims=True)
            a = ab[:, 2 * j * MOE_HIDDEN:(2 * j + 1) * MOE_HIDDEN]
            hid.append(((a * jax.nn.sigmoid(a)) * ab[:, (2 * j + 1) * MOE_HIDDEN:(2 * j + 2) * MOE_HIDDEN]
                        * cw).astype(BF16))
        y = _dot(jnp.concatenate(hid, axis=1), wd_ref[...])
        o_ref[...] += _dot(scatter, y.astype(BF16))

    ext = ch + MOE_SMALL_CHUNK
    for c in range(tm // ch):
        left = n_grp - c * ch
        pl.when((left > ext) | ((left > MOE_SMALL_CHUNK) & (left <= ch)))(functools.partial(run_chunk, c, ch))
        if (c + 1) * ch < tm:
            pl.when((left > ch) & (left <= ext))(functools.partial(run_chunk, c, ext))
        if c == 0:
            pl.when((left > 0) & (left <= MOE_SMALL_CHUNK))(functools.partial(run_chunk, c, MOE_SMALL_CHUNK))

    @pl.when(grp == pl.num_programs(1) - 1)
    def _():
        x2 = o_ref[...]
        e = _dot(p_ref[...].astype(BF16), wp_ref[...])
        gate = jax.nn.sigmoid(_dot(_rms_rows(x2, g1_ref[...]).astype(BF16), wg_ref[...]))
        o_ref[...] = x2 + _rms_rows(e * gate, g2_ref[...])


def _moe(x2d, mix_a, mix_b, w_out, g, wr_hi, wr_lo, rb, wgu, wd, p2d, wp, wg, g1, g2, tm=TAIL_TILE):
    t, d = x2d.shape
    kp = p2d.shape[1]
    half = mix_a.shape[1]
    tri = jnp.triu(jnp.ones((tm, tm), BF16))
    const = lambda shape: pl.BlockSpec(shape, lambda i, gi: (0,) * len(shape), pipeline_mode=pl.Buffered(1))
    return pl.pallas_call(
        _moe_kernel,
        grid=(t // tm, MOE_GROUPS),
        in_specs=[pl.BlockSpec((tm, d), lambda i, gi: (i, 0)),
                  pl.BlockSpec((tm, half), lambda i, gi: (i, 0)),
                  pl.BlockSpec((tm, half), lambda i, gi: (i, 0)),
                  const((2 * half, d)),
                  const((1, d)), const((MOE_ROUTER_ROWS, d)), const((MOE_ROUTER_ROWS, d)),
                  const((MOE_ROUTER_ROWS, 1)), const((tm, tm)),
                  pl.BlockSpec((None, d, MOE_EPG * 2 * MOE_HIDDEN), lambda i, gi: (gi, 0, 0)),
                  pl.BlockSpec((None, MOE_EPG * MOE_HIDDEN, d), lambda i, gi: (gi, 0, 0)),
                  pl.BlockSpec((tm, kp), lambda i, gi: (i, 0)),
                  const((kp, d)), const((d, d)), const((1, d)), const((1, d))],
        out_specs=pl.BlockSpec((tm, d), lambda i, gi: (i, 0)),
        out_shape=jax.ShapeDtypeStruct((t, d), F32),
        scratch_shapes=[pltpu.VMEM((tm, d), BF16), pltpu.VMEM((8, tm), F32), pltpu.VMEM((tm, LANES), F32),
                        pltpu.VMEM((tm, LANES), BF16), pltpu.VMEM((16, LANES), F32)],
        compiler_params=_cparams(("parallel", "arbitrary"), "tail"),
        name="moe",
    )(x2d, mix_a, mix_b, w_out, g, wr_hi, wr_lo, rb, tri, wgu, wd, p2d, wp, wg, g1, g2)


def _pad_lanes(cols, width=LANES):
    return jnp.pad(cols, ((0, 0), (0, width - cols.shape[-1])))


def _row(v):
    return v.reshape(1, -1).astype(F32)


def _even_mix(x2d, bsz, seq, norm_g, w_in, fox_b_f, fox_qn_g, fox_kn_g, conv_w, conv_b, b_i, b_f, mnorm_g):
    hw = HALF
    o = np.cumsum([0, hw, hw, hw, ATT_HEADS, hw, hw, hw, hw, MLSTM_HEADS, MLSTM_HEADS, hw])
    seg = lambda i: w_in[:, o[i]:o[i + 1]]
    fq, fk, fv, ff, fo, mq, mk, mv, mi, mf, mo = [seg(i) for i in range(11)]
    w = jnp.concatenate([fq, fk, fv, mv, fo, mq, mk, mo, _pad_lanes(jnp.concatenate([ff, mi, mf], axis=1))],
                        axis=1).astype(BF16)
    gain = jnp.concatenate([jnp.tile(fox_qn_g, ATT_HEADS) * ATT_Q_SCALE, jnp.tile(fox_kn_g, ATT_HEADS)])
    att, rest, gates = _in_proj(x2d, _row(norm_g), w, _row(gain), ((4 * hw, BF16), (4 * hw, F32), (LANES, F32)),
                                2 * hw)
    att = att.reshape(bsz, seq, 4 * hw)
    rest = rest.reshape(bsz, seq, 4 * hw)
    bias = _pad_lanes(_row(jnp.concatenate([fox_b_f, b_i, b_f])))
    gcol, rm = _prep_even(gates.reshape(bsz, seq, LANES), bias)
    out_a = _fox_attention(att, rest, gcol)
    out_b = _mlstm(att, rest, conv_w, _row(conv_b), gcol, rm, _row(mnorm_g))
    return out_a.reshape(-1, hw), out_b.reshape(-1, hw)


def _odd_mix(x2d, bsz, seq, norm_g, w_in, moba_qn_g, moba_kn_g, conv_w, conv_b, dt_bias, a_log, d_skip, snorm_g):
    hw = HALF
    nbc = SSD_GROUPS * SSD_STATE
    o = np.cumsum([0, hw, hw, hw, hw, hw, nbc, nbc, SSD_HEADS])
    w = jnp.concatenate([w_in[:, :o[7]], _pad_lanes(w_in[:, o[7]:o[8]])], axis=1).astype(BF16)
    gain = jnp.concatenate([jnp.tile(moba_qn_g, ATT_HEADS) * ATT_Q_SCALE, jnp.tile(moba_kn_g, ATT_HEADS)])
    moba_u, zx, bcin, dts = _in_proj(x2d, _row(norm_g), w, _row(gain),
                                     ((3 * hw, BF16), (2 * hw, F32), (2 * nbc, F32), (LANES, F32)), 2 * hw)
    gcol, rs = _prep_odd(dts.reshape(bsz, seq, LANES), _pad_lanes(_row(dt_bias)), _pad_lanes(_row(a_log)))
    out_c = _moba_attention(moba_u.reshape(bsz, seq, 3 * hw))
    out_d = _ssd(zx.reshape(bsz, seq, 2 * hw), bcin.reshape(bsz, seq, 2 * nbc), conv_w, _row(conv_b), gcol,
                 _ssd_expand_matrices(), rs, _row(jnp.repeat(d_skip, SSD_HEAD_DIM)), _row(snorm_g))
    return out_c.reshape(-1, hw), out_d.reshape(-1, hw)


def _layer_tail(x2d, mix_a, mix_b, w_out, norm_g, w_group, b_group, w_router, b_router, w_gate, w_up, w_down,
                p2d, ple_w_proj, ple_w_gate, ple_gate_norm_g, ple_out_norm_g):
    d = x2d.shape[1]
    wr = _pad_lanes(jnp.concatenate([w_group, w_router], axis=1), MOE_ROUTER_ROWS).T
    wr_hi = wr.astype(BF16)
    wr_lo = (wr - wr_hi.astype(F32)).astype(BF16)
    rb = _pad_lanes(_row(jnp.concatenate([b_group, b_router])), MOE_ROUTER_ROWS).T
    wgu = jnp.concatenate([w_gate, w_up], axis=-1).astype(BF16)
    wgu = wgu.transpose(0, 2, 1, 3).reshape(MOE_GROUPS, d, MOE_EPG * 2 * MOE_HIDDEN)
    wd = w_down.reshape(MOE_GROUPS, MOE_EPG * MOE_HIDDEN, d).astype(BF16)
    return _moe(x2d, mix_a, mix_b, w_out.astype(BF16), _row(norm_g), wr_hi, wr_lo, rb, wgu, wd,
                p2d, ple_w_proj.astype(BF16), ple_w_gate.astype(BF16), _row(ple_gate_norm_g), _row(ple_out_norm_g))


def kernel(x, p, norm1_g, norm2_g, ev_w_in, ev_fox_b_f, ev_fox_qn_g, ev_fox_kn_g, ev_mlstm_conv_w, ev_mlstm_conv_b, ev_mlstm_b_i, ev_mlstm_b_f, ev_mlstm_norm_g, ev_w_out, od_w_in, od_moba_qn_g, od_moba_kn_g, od_ssd_conv_w, od_ssd_conv_b, od_ssd_dt_bias, od_ssd_A_log, od_ssd_D, od_ssd_norm_g, od_w_out, moe_w_group, moe_b_group, moe_w_router, moe_b_router, moe_w_gate, moe_w_up, moe_w_down, ple_w_proj, ple_w_gate, ple_gate_norm_g, ple_out_norm_g):
    bsz, seq, d = x.shape
    depth = p.shape[0]
    x2d = x.reshape(bsz * seq, d)
    for i in range(depth):
        j = i // 2
        if i % 2 == 0:
            mix_a, mix_b = _even_mix(x2d, bsz, seq, norm1_g[i], ev_w_in[j], ev_fox_b_f[j], ev_fox_qn_g[j],
                                     ev_fox_kn_g[j], ev_mlstm_conv_w[j], ev_mlstm_conv_b[j], ev_mlstm_b_i[j],
                                     ev_mlstm_b_f[j], ev_mlstm_norm_g[j])
            w_out = ev_w_out[j]
        else:
            mix_a, mix_b = _odd_mix(x2d, bsz, seq, norm1_g[i], od_w_in[j], od_moba_qn_g[j], od_moba_kn_g[j],
                                    od_ssd_conv_w[j], od_ssd_conv_b[j], od_ssd_dt_bias[j], od_ssd_A_log[j],
                                    od_ssd_D[j], od_ssd_norm_g[j])
            w_out = od_w_out[j]
        x2d = _layer_tail(x2d, mix_a, mix_b, w_out, norm2_g[i], moe_w_group[i], moe_b_group[i], moe_w_router[i],
                          moe_b_router[i], moe_w_gate[i], moe_w_up[i], moe_w_down[i], p[i].reshape(bsz * seq, -1),
                          ple_w_proj[i], ple_w_gate[i], ple_gate_norm_g[i], ple_out_norm_g[i])
    return x2d.reshape(bsz, seq, d)
```
